```python
import math
import jax, jax.numpy as jnp
from jax import lax
import numpy as np

D_MODEL = 1024
BATCH = 2
SEQ = 16384
DEPTH = 2

HEAD_DIM = 64
N_HEADS_A = 4
N_HEADS_B = 4
N_HEADS_C = 4
N_HEADS_D = 4
WA = N_HEADS_A * HEAD_DIM
WB = N_HEADS_B * HEAD_DIM
WC = N_HEADS_C * HEAD_DIM
WD = N_HEADS_D * HEAD_DIM
MIX_WIDTH = WA + WB + WC + WD
DIFF_HALF = HEAD_DIM // 2
Q_BLOCK = 128
MOBA_BLOCK = 256
MOBA_TOPK = 3
DSA_Q_RANK = 128
DSA_KV_RANK = 128
IDX_HEADS = 4
IDX_DIM = 64
DSA_TOPK_MAX = 256
T5_BUCKETS = 32
T5_MAX_DIST = 128
N_BIAS_HEADS = N_HEADS_A + N_HEADS_B + N_HEADS_D
D_FF = 2816
CONV_WIDTH = 3
EPS = 1e-6
NEG = -1e30
SPLIT_SIZES = (WA, WA, WA,
               WB, WB, WB,
               WC, WC, WC, N_HEADS_C, WC,
               DSA_Q_RANK, DSA_KV_RANK, IDX_DIM, IDX_HEADS)
IN_COLS = sum(SPLIT_SIZES)

kernel_name = "hybrid_parallel_heads_diff_moba_fox_dsa_convffn"


def _rms_norm(x, gain=None):
    xf = x.astype(jnp.float32)
    y = xf * lax.rsqrt(jnp.mean(xf * xf, axis=-1, keepdims=True) + EPS)
    if gain is not None:
        y = y * gain.astype(jnp.float32)
    return y.astype(x.dtype)


def _t5_bucket(dist):
    max_exact = T5_BUCKETS // 2
    d = jnp.maximum(dist, 0)
    ratio = jnp.maximum(d, 1).astype(jnp.float32) / max_exact
    large = max_exact + (jnp.log(ratio) / math.log(T5_MAX_DIST / max_exact)
                         * (T5_BUCKETS - max_exact)).astype(jnp.int32)
    large = jnp.minimum(large, T5_BUCKETS - 1)
    return jnp.where(d < max_exact, d, large)


def _sweep(block_fn, n_blocks):
    out = lax.map(block_fn, jnp.arange(n_blocks))
    nb, b, qb, h, d = out.shape
    return out.transpose(1, 0, 2, 3, 4).reshape(b, nb * qb, h, d)


def _diff_attention(q, k, v, lam, subln_gain, lambda_init, tab):
    S = v.shape[1]
    scale = DIFF_HALF ** -0.5
    kpos = jnp.arange(S)

    def block(i):
        q0 = i * Q_BLOCK
        qb = lax.dynamic_slice_in_dim(q, q0, Q_BLOCK, axis=1)
        dist = (q0 + jnp.arange(Q_BLOCK))[:, None] - kpos[None, :]
        bias = tab[:, _t5_bucket(dist)]
        logits = jnp.einsum('bqhcd,bkhcd->bchqk', qb, k).astype(jnp.float32) * scale + bias
        logits = jnp.where(dist >= 0, logits, NEG)
        p = jax.nn.softmax(logits, axis=-1)
        attn = p[:, 0] - lam * p[:, 1]
        return jnp.einsum('bhqk,bkhd->bqhd', attn.astype(v.dtype), v)

    o = _sweep(block, S // Q_BLOCK)
    return _rms_norm(o, subln_gain) * (1.0 - lambda_init)


def _moba_attention(q, k, v, tab):
    B, S, H, dh = q.shape
    scale = dh ** -0.5
    nb = -(-S // MOBA_BLOCK)
    kk = min(MOBA_TOPK, nb)
    pad = nb * MOBA_BLOCK - S
    kpad = jnp.pad(k, ((0, 0), (0, pad), (0, 0), (0, 0)))
    vpad = jnp.pad(v, ((0, 0), (0, pad), (0, 0), (0, 0)))
    kbh = kpad.reshape(B, nb, MOBA_BLOCK, H, dh).transpose(0, 3, 1, 2, 4)
    vbh = vpad.reshape(B, nb, MOBA_BLOCK, H, dh).transpose(0, 3, 1, 2, 4)
    kmean = jnp.mean(kbh.astype(jnp.float32), axis=3).astype(k.dtype)
    bi = jnp.arange(B)[:, None, None, None]
    hi = jnp.arange(H)[None, None, :, None]
    hi5 = jnp.arange(H)[None, None, :, None, None]
    blk_ids = jnp.arange(nb)
    in_blk = jnp.arange(MOBA_BLOCK)

    def block(i):
        q0 = i * Q_BLOCK
        qb = lax.dynamic_slice_in_dim(q, q0, Q_BLOCK, axis=1)
        qpos = q0 + jnp.arange(Q_BLOCK)
        cur = q0 // MOBA_BLOCK
        gate = jnp.einsum('bqhd,bhnd->bqhn', qb, kmean).astype(jnp.float32)
        gate = jnp.where(blk_ids < cur, gate, NEG)
        _, sel = lax.top_k(gate, kk)
        sel_valid = sel < cur
        ksel = kbh[bi, hi, sel]
        vsel = vbh[bi, hi, sel]
        s_sel = jnp.einsum('bqhd,bqhnkd->bqhnk', qb, ksel).astype(jnp.float32) * scale
        kpos_sel = sel[..., None] * MOBA_BLOCK + in_blk
        dist_sel = qpos[None, :, None, None, None] - kpos_sel
        s_sel = s_sel + tab[hi5, _t5_bucket(dist_sel)]
        s_sel = jnp.where(sel_valid[..., None], s_sel, NEG)
        k_own = lax.dynamic_slice_in_dim(kpad, cur * MOBA_BLOCK, MOBA_BLOCK, axis=1)
        v_own = lax.dynamic_slice_in_dim(vpad, cur * MOBA_BLOCK, MOBA_BLOCK, axis=1)
        dist_own = qpos[:, None] - (cur * MOBA_BLOCK + in_blk)[None, :]
        s_own = jnp.einsum('bqhd,bkhd->bqhk', qb, k_own).astype(jnp.float32) * scale
        s_own = s_own + tab[:, _t5_bucket(dist_own)].transpose(1, 0, 2)[None]
        s_own = jnp.where(dist_own[None, :, None, :] >= 0, s_own, NEG)
        logits = jnp.concatenate([s_sel.reshape(B, Q_BLOCK, H, kk * MOBA_BLOCK), s_own], axis=-1)
        p = jax.nn.softmax(logits, axis=-1).astype(v.dtype)
        p_sel = p[..., :kk * MOBA_BLOCK].reshape(B, Q_BLOCK, H, kk, MOBA_BLOCK)
        p_own = p[..., kk * MOBA_BLOCK:]
        return (jnp.einsum('bqhnk,bqhnkd->bqhd', p_sel, vsel)
                + jnp.einsum('bqhk,bkhd->bqhd', p_own, v_own))

    return _sweep(block, S // Q_BLOCK)


def _forgetting_attention(q, k, v, f_logit):
    S = q.shape[1]
    scale = HEAD_DIM ** -0.5
    cf = jnp.cumsum(jax.nn.log_sigmoid(f_logit.astype(jnp.float32)), axis=1).transpose(0, 2, 1)
    kpos = jnp.arange(S)

    def block(i):
        q0 = i * Q_BLOCK
        qb = lax.dynamic_slice_in_dim(q, q0, Q_BLOCK, axis=1)
        cq = lax.dynamic_slice_in_dim(cf, q0, Q_BLOCK, axis=2)
        dist = (q0 + jnp.arange(Q_BLOCK))[:, None] - kpos[None, :]
        logits = (jnp.einsum('bqhd,bkhd->bhqk', qb, k).astype(jnp.float32) * scale
                  + cq[..., None] - cf[:, :, None, :])
        logits = jnp.where(dist >= 0, logits, NEG)
        p = jax.nn.softmax(logits, axis=-1)
        return jnp.einsum('bhqk,bkhd->bqhd', p.astype(v.dtype), v)

    return _sweep(block, S // Q_BLOCK)


def _dsa_attention(q, k, v, q_idx, k_idx, w_idx, tab):
    B, S, H, dh = q.shape
    scale = dh ** -0.5
    topk = min(DSA_TOPK_MAX, S // 4)
    kpos = jnp.arange(S)

    def block(i):
        q0 = i * Q_BLOCK
        qpos = q0 + jnp.arange(Q_BLOCK)
        qb = lax.dynamic_slice_in_dim(q, q0, Q_BLOCK, axis=1)
        qi = lax.dynamic_slice_in_dim(q_idx, q0, Q_BLOCK, axis=1)
        wi = lax.dynamic_slice_in_dim(w_idx, q0, Q_BLOCK, axis=1)
        score = jnp.einsum('bqhd,bkd->bqhk', qi, k_idx).astype(jnp.float32) * IDX_DIM ** -0.5
        score = jnp.einsum('bqhk,bqh->bqk', jax.nn.relu(score), wi.astype(jnp.float32))
        score = jnp.where((qpos[:, None] >= kpos[None, :])[None], score, NEG)
        _, sel = lax.top_k(score, topk)
        valid = sel <= qpos[None, :, None]
        ksel = jax.vmap(lambda kb, ib: kb[ib])(k, sel)
        vsel = jax.vmap(lambda vb, ib: vb[ib])(v, sel)
        s = jnp.einsum('bqhd,bqkhd->bqhk', qb, ksel).astype(jnp.float32) * scale
        dist = qpos[None, :, None] - sel
        s = s + tab[:, _t5_bucket(dist)].transpose(1, 2, 0, 3)
        s = jnp.where(valid[:, :, None, :], s, NEG)
        p = jax.nn.softmax(s, axis=-1).astype(v.dtype)
        return jnp.einsum('bqhk,bqkhd->bqhd', p, vsel)

    return _sweep(block, S // Q_BLOCK)


def _causal_depthwise_conv(u, w, b):
    C = u.shape[-1]
    out = lax.conv_general_dilated(u, w[:, None, :].astype(u.dtype), window_strides=(1,),
                                   padding=[(CONV_WIDTH - 1, 0)],
                                   dimension_numbers=('NWC', 'WIO', 'NWC'),
                                   feature_group_count=C)
    return out + b


def _conv_glu_ffn(h, w_up, conv_w, conv_b, w_down):
    u = _causal_depthwise_conv(h @ w_up, conv_w, conv_b)
    gate, val = jnp.split(u, 2, axis=-1)
    return (jax.nn.silu(gate) * val) @ w_down


def setup_inputs(seed: int = 0) -> dict:
    key = jax.random.key(seed)
    ks = jax.random.split(key, 20)
    f32 = jnp.float32

    def nrm(k, shape, scale):
        return jax.random.normal(k, shape, f32) * scale

    return {
        "x": nrm(ks[0], (BATCH, SEQ, D_MODEL), 1.0),
        "c": nrm(ks[1], (BATCH, D_MODEL), 1.0),
        "w_ada": nrm(ks[2], (DEPTH, D_MODEL, 6 * D_MODEL), D_MODEL ** -0.5),
        "b_ada": nrm(ks[3], (DEPTH, 6 * D_MODEL), 0.02),
        "norm1_gain": 1.0 + nrm(ks[4], (DEPTH, D_MODEL), 0.02),
        "w_in": nrm(ks[5], (DEPTH, D_MODEL, IN_COLS), D_MODEL ** -0.5),
        "f_bias": 2.0 + nrm(ks[6], (DEPTH, N_HEADS_C), 0.5),
        "diff_lambda": nrm(ks[7], (DEPTH, 4, DIFF_HALF), 0.1),
        "diff_subln_gain": 1.0 + nrm(ks[8], (DEPTH, HEAD_DIM), 0.02),
        "w_dq_up": nrm(ks[9], (DEPTH, DSA_Q_RANK, WD), DSA_Q_RANK ** -0.5),
        "w_didx_q": nrm(ks[10], (DEPTH, DSA_Q_RANK, IDX_HEADS * IDX_DIM), DSA_Q_RANK ** -0.5),
        "w_dkv_up": nrm(ks[11], (DEPTH, DSA_KV_RANK, 2 * WD), DSA_KV_RANK ** -0.5),
        "w_out": nrm(ks[12], (DEPTH, MIX_WIDTH, D_MODEL), MIX_WIDTH ** -0.5),
        "t5_table": nrm(ks[13], (T5_BUCKETS, N_BIAS_HEADS), 0.5),
        "norm2_gain": 1.0 + nrm(ks[14], (DEPTH, D_MODEL), 0.02),
        "w_ffn_up": nrm(ks[15], (DEPTH, D_MODEL, 2 * D_FF), D_MODEL ** -0.5),
        "ffn_conv_w": nrm(ks[16], (DEPTH, CONV_WIDTH, 2 * D_FF), CONV_WIDTH ** -0.5),
        "ffn_conv_b": nrm(ks[17], (DEPTH, 2 * D_FF), 0.02),
        "w_ffn_down": nrm(ks[18], (DEPTH, D_FF, D_MODEL), D_FF ** -0.5),
        "final_gain": 1.0 + nrm(ks[19], (D_MODEL,), 0.02),
    }


def reference(x, c, w_ada, b_ada, norm1_gain, w_in, f_bias, diff_lambda, diff_subln_gain,
              w_dq_up, w_didx_q, w_dkv_up, w_out, t5_table, norm2_gain, w_ffn_up,
              ffn_conv_w, ffn_conv_b, w_ffn_down, final_gain):
    B, S, _ = x.shape
    split_points = np.cumsum(SPLIT_SIZES)[:-1].tolist()
    tab_a = t5_table[:, :N_HEADS_A].T
    tab_b = t5_table[:, N_HEADS_A:N_HEADS_A + N_HEADS_B].T
    tab_d = t5_table[:, N_HEADS_A + N_HEADS_B:].T
    cond = jax.nn.silu(c)

    def heads(t, n):
        return t.reshape(B, S, n, HEAD_DIM)

    for l in range(DEPTH):
        mod = cond @ w_ada[l] + b_ada[l]
        sh1, sc1, g1, sh2, sc2, g2 = [m[:, None, :] for m in jnp.split(mod, 6, axis=-1)]

        h = _rms_norm(x, norm1_gain[l]) * (1.0 + sc1) + sh1
        proj = h @ w_in[l]
        (qa, ka, va, qb, kb, vb, qc, kc, vc, fc, gc,
         qlat, kvlat, kidx, widx) = jnp.split(proj, split_points, axis=-1)

        lambda_init = 0.8 - 0.6 * math.exp(-0.3 * l)
        lq1, lk1, lq2, lk2 = diff_lambda[l]
        lam = jnp.exp(jnp.sum(lq1 * lk1)) - jnp.exp(jnp.sum(lq2 * lk2)) + lambda_init
        o_a = _diff_attention(qa.reshape(B, S, N_HEADS_A, 2, DIFF_HALF),
                              ka.reshape(B, S, N_HEADS_A, 2, DIFF_HALF),
                              heads(va, N_HEADS_A), lam, diff_subln_gain[l], lambda_init, tab_a)

        o_b = _moba_attention(heads(qb, N_HEADS_B), heads(kb, N_HEADS_B), heads(vb, N_HEADS_B), tab_b)

        o_c = _forgetting_attention(heads(qc, N_HEADS_C), heads(kc, N_HEADS_C), heads(vc, N_HEADS_C),
                                    fc + f_bias[l])
        o_c = o_c.reshape(B, S, WC) * jax.nn.sigmoid(gc)

        q_lat = _rms_norm(qlat)
        kv_lat = _rms_norm(kvlat)
        q_d = heads(q_lat @ w_dq_up[l], N_HEADS_D)
        q_i = (q_lat @ w_didx_q[l]).reshape(B, S, IDX_HEADS, IDX_DIM)
        k_d, v_d = jnp.split(kv_lat @ w_dkv_up[l], 2, axis=-1)
        o_d = _dsa_attention(q_d, heads(k_d, N_HEADS_D), heads(v_d, N_HEADS_D), q_i, kidx,
                             widx * IDX_HEADS ** -0.5, tab_d)

        o = jnp.concatenate([o_a.reshape(B, S, WA), o_b.reshape(B, S, WB), o_c,
                             o_d.reshape(B, S, WD)], axis=-1)
        x = x + g1 * (o @ w_out[l])

        h = _rms_norm(x, norm2_gain[l]) * (1.0 + sc2) + sh2
        x = x + g2 * _conv_glu_ffn(h, w_ffn_up[l], ffn_conv_w[l], ffn_conv_b[l], w_ffn_down[l])

    return _rms_norm(x, final_gain)
```

```python
import functools
import math

import numpy as np
import jax
import jax.numpy as jnp
from jax import lax
from jax.experimental import pallas as pl
from jax.experimental.pallas import tpu as pltpu

F32 = jnp.float32
BF16 = jnp.bfloat16
I32 = jnp.int32

HEAD_DIM = 64
N_HEADS = 4
GROUP_W = N_HEADS * HEAD_DIM
DIFF_HALF = HEAD_DIM // 2
MOBA_BLOCK = 256
MOBA_TOPK = 3
DSA_RANK = 128
IDX_HEADS = 4
IDX_DIM = 64
DSA_TOPK_MAX = 256
T5_BUCKETS = 32
T5_MAX_DIST = 128
N_BIAS_HEADS = 12
D_FF = 2816
EPS = 1e-6
NEG = -1e30

V7X_VMEM_LIMIT_BYTES = 56 * 1024 * 1024

ATTN_TILE = 512
SPARSE_TILE = 256
ROW_TILE = 512
FFN_ROW_TILE = 1024
FFN_CHUNK = 256


def _t5_thresholds():
    d = np.arange(0, 4 * T5_MAX_DIST)
    max_exact = T5_BUCKETS // 2
    ratio = np.maximum(d, 1).astype(np.float32) / max_exact
    large = max_exact + (np.log(ratio) / math.log(T5_MAX_DIST / max_exact) * (T5_BUCKETS - max_exact)).astype(np.int32)
    bucket = np.where(d < max_exact, d, np.minimum(large, T5_BUCKETS - 1))
    return [int(np.argmax(bucket >= b)) for b in range(T5_BUCKETS)]


T5_THRESH = _t5_thresholds()
T5_FAR_DIST = T5_THRESH[-1]


def _cparams(n_axes):
    return pltpu.CompilerParams(dimension_semantics=("arbitrary",) * n_axes,
                                vmem_limit_bytes=V7X_VMEM_LIMIT_BYTES)


def _resident(block_shape, index_map):
    return pl.BlockSpec(block_shape, index_map, pipeline_mode=pl.Buffered(1))


def _smem():
    return pl.BlockSpec(memory_space=pltpu.SMEM)


def _fill_t5_tile(bias_ref, slot, t5_ref, head, delta, tile):
    rows = 64

    def body(r, carry):
        r0 = pl.multiple_of(r * rows, rows)
        kr = lax.broadcasted_iota(I32, (rows, tile), 0) + r0
        qc = lax.broadcasted_iota(I32, (rows, tile), 1)
        d = delta + qc - kr
        val = jnp.full((rows, tile), t5_ref[head], F32)
        for b in range(1, T5_BUCKETS):
            val = jnp.where(d >= T5_THRESH[b], t5_ref[b * N_BIAS_HEADS + head], val)
        bias_ref[slot, pl.ds(r0, rows), :] = val
        return carry

    lax.fori_loop(0, tile // rows, body, 0)


def _causal_mask(tile):
    kr = lax.broadcasted_iota(I32, (tile, tile), 0)
    qc = lax.broadcasted_iota(I32, (tile, tile), 1)
    return kr <= qc


def _flash_update(s, v_t, m_ref, l_ref, acc_ref, idx, weight=None):
    m_old = m_ref[idx]
    m_new = jnp.maximum(m_old, jnp.max(s, axis=0, keepdims=True))
    p = jnp.exp(s - m_new)
    if weight is not None:
        p = p * weight
    alpha = jnp.exp(m_old - m_new)
    l_ref[idx] = alpha * l_ref[idx] + jnp.sum(p, axis=0, keepdims=True)
    acc_ref[idx] = alpha * acc_ref[idx] + jnp.dot(v_t, p.astype(BF16), preferred_element_type=F32)
    m_ref[idx] = m_new


def _init_stats(m_ref, l_ref, acc_ref):
    m_ref[...] = jnp.full(m_ref.shape, NEG, F32)
    l_ref[...] = jnp.zeros(l_ref.shape, F32)
    acc_ref[...] = jnp.zeros(acc_ref.shape, F32)


def _mod_kernel(c_ref, w_ref, b_ref, o_ref):
    c = c_ref[...]
    cond = c * (1.0 / (1.0 + jnp.exp(-c)))
    o_ref[...] = jnp.dot(cond.astype(BF16), w_ref[...].astype(BF16), preferred_element_type=F32) + b_ref[...]


def _modulation(c, w_ada, b_ada):
    depth, d, n = w_ada.shape
    bsz = c.shape[0]
    rows = 8
    c_pad = jnp.zeros((rows, d), F32).at[:bsz].set(c)
    tn = 1024
    out = pl.pallas_call(
        _mod_kernel,
        grid=(depth, n // tn),
        in_specs=[pl.BlockSpec((rows, d), lambda l, j: (0, 0)),
                  pl.BlockSpec((None, d, tn), lambda l, j: (l, 0, j)),
                  pl.BlockSpec((None, 1, tn), lambda l, j: (l, 0, j))],
        out_specs=pl.BlockSpec((None, rows, tn), lambda l, j: (l, 0, j)),
        out_shape=jax.ShapeDtypeStruct((depth, rows, n), F32),
        compiler_params=_cparams(2),
    )(c_pad, w_ada, b_ada.reshape(depth, 1, n))
    return out[:, :bsz]


def _rms(x):
    return x * lax.rsqrt(jnp.mean(x * x, axis=-1, keepdims=True) + EPS)


def _inproj_kernel(x_ref, gain_ref, sc_ref, sh_ref, w_ref, wq_ref, wkv_ref,
                   pa_ref, pb_ref, pc_ref, gc_ref, dq_ref, dkv_ref, misc_ref):
    x = x_ref[...]
    h = (_rms(x) * gain_ref[...]) * (1.0 + sc_ref[...]) + sh_ref[...]
    hb = h.astype(BF16)
    w3 = 3 * GROUP_W

    def proj(c0, c1):
        return jnp.dot(hb, w_ref[:, c0:c1], preferred_element_type=F32)

    pa_ref[...] = proj(0, w3).astype(BF16)
    pb_ref[...] = proj(w3, 2 * w3).astype(BF16)
    pc_ref[...] = proj(2 * w3, 3 * w3).astype(BF16)
    c0 = 3 * w3
    gc_ref[...] = proj(c0, c0 + GROUP_W)
    c0 += GROUP_W
    q_lat = _rms(proj(c0, c0 + DSA_RANK))
    kv_lat = _rms(proj(c0 + DSA_RANK, c0 + 2 * DSA_RANK))
    dq_ref[...] = jnp.dot(q_lat.astype(BF16), wq_ref[...], preferred_element_type=F32).astype(BF16)
    dkv_ref[...] = jnp.dot(kv_lat.astype(BF16), wkv_ref[...], preferred_element_type=F32).astype(BF16)
    c0 += 2 * DSA_RANK
    misc_ref[...] = proj(c0, c0 + 128)


def _input_projection(x, gain, sc, sh, w_in, w_dq_up, w_didx_q, w_dkv_up):
    bsz, seq, d = x.shape
    w3 = 3 * GROUP_W
    o_c = 2 * w3
    o_fc = o_c + w3
    o_gc = o_fc + N_HEADS
    o_ql = o_gc + GROUP_W
    o_kv = o_ql + DSA_RANK
    o_ki = o_kv + DSA_RANK
    o_wi = o_ki + IDX_DIM
    pad = jnp.zeros((d, 128 - IDX_DIM - N_HEADS - IDX_HEADS), w_in.dtype)
    w = jnp.concatenate([w_in[:, :o_fc], w_in[:, o_gc:o_ql], w_in[:, o_ql:o_ki],
                         w_in[:, o_ki:o_wi], w_in[:, o_fc:o_gc], w_in[:, o_wi:], pad], axis=1).astype(BF16)
    n_cols = w.shape[1]
    wq = jnp.concatenate([w_dq_up, w_didx_q], axis=1).astype(BF16)
    wkv = w_dkv_up.astype(BF16)
    tm = ROW_TILE
    row = lambda width: pl.BlockSpec((None, tm, width), lambda b, i: (b, i, 0))
    vec = pl.BlockSpec((None, 1, d), lambda b, i: (b, 0, 0))
    outs = pl.pallas_call(
        _inproj_kernel,
        grid=(bsz, seq // tm),
        in_specs=[row(d), pl.BlockSpec((1, d), lambda b, i: (0, 0)), vec, vec,
                  _resident((d, n_cols), lambda b, i: (0, 0)),
                  _resident(wq.shape, lambda b, i: (0, 0)),
                  _resident(wkv.shape, lambda b, i: (0, 0))],
        out_specs=[row(w3), row(w3), row(w3), row(GROUP_W), row(512), row(512), row(128)],
        out_shape=[jax.ShapeDtypeStruct((bsz, seq, w3), BF16)] * 3
                  + [jax.ShapeDtypeStruct((bsz, seq, GROUP_W), F32)]
                  + [jax.ShapeDtypeStruct((bsz, seq, 512), BF16)] * 2
                  + [jax.ShapeDtypeStruct((bsz, seq, 128), F32)],
        compiler_params=_cparams(2),
    )(x, gain.reshape(1, d), sc, sh, w, wq, wkv)
    return outs


def _heads_t(t):
    bsz, seq, w = t.shape
    return t.reshape(bsz, seq, N_HEADS, w // N_HEADS).transpose(0, 2, 3, 1)


def _heads(t):
    bsz, seq, w = t.shape
    return t.reshape(bsz, seq, N_HEADS, w // N_HEADS).transpose(0, 2, 1, 3)


def _unheads_t(o_t):
    bsz, h, dh, seq = o_t.shape
    return o_t.transpose(0, 3, 1, 2).reshape(bsz, seq, h * dh)


def _diff_kernel(t5_ref, lam_ref, q_ref, k_ref, v_ref, g_ref, o_ref, bias_ref, m_ref, l_ref, acc_ref,
                 *, tile, head0, out_scale):
    h = pl.program_id(1)
    i = pl.program_id(2)
    head = head0 + h
    scale = DIFF_HALF ** -0.5

    @pl.when(i == 0)
    def _():
        _fill_t5_tile(bias_ref, 0, t5_ref, head, 0, tile)
        _fill_t5_tile(bias_ref, 1, t5_ref, head, tile, tile)

    _init_stats(m_ref, l_ref, acc_ref)

    def step(j, bias, mask):
        k0 = pl.multiple_of(j * tile, tile)
        kt = k_ref[pl.ds(k0, tile), :]
        vt = v_ref[:, pl.ds(k0, tile)]
        for c in range(2):
            s = jnp.dot(kt, q_ref[c], preferred_element_type=F32) * scale + bias
            if mask is not None:
                s = jnp.where(mask, s, NEG)
            _flash_update(s, vt, m_ref, l_ref, acc_ref, c)

    step(i, bias_ref[0], _causal_mask(tile))

    @pl.when(i >= 1)
    def _():
        step(i - 1, bias_ref[1], None)

    far_bias = t5_ref[(T5_BUCKETS - 1) * N_BIAS_HEADS + head]

    def far(j, carry):
        step(j, far_bias, None)
        return carry

    lax.fori_loop(0, jnp.maximum(i - 1, 0), far, 0)

    o = acc_ref[0] / l_ref[0] - lam_ref[0] * (acc_ref[1] / l_ref[1])
    y = o * lax.rsqrt(jnp.mean(o * o, axis=0, keepdims=True) + EPS)
    o_ref[...] = ((y * g_ref[...]) * out_scale).astype(o_ref.dtype)


def _diff_attention(q_t, k, v_t, lam, subln_gain, t5_flat, out_scale):
    bsz, nh, _, dh, seq = q_t.shape
    tile = min(ATTN_TILE, seq)
    assert tile >= T5_FAR_DIST and seq % tile == 0
    kern = functools.partial(_diff_kernel, tile=tile, head0=0, out_scale=out_scale)
    return pl.pallas_call(
        kern,
        grid=(bsz, nh, seq // tile),
        in_specs=[_smem(), _smem(),
                  pl.BlockSpec((None, None, 2, dh, tile), lambda b, h, i: (b, h, 0, 0, i)),
                  pl.BlockSpec((None, None, seq, dh), lambda b, h, i: (b, h, 0, 0)),
                  pl.BlockSpec((None, None, dh, seq), lambda b, h, i: (b, h, 0, 0)),
                  pl.BlockSpec((dh, 1), lambda b, h, i: (0, 0))],
        out_specs=pl.BlockSpec((None, None, dh, tile), lambda b, h, i: (b, h, 0, i)),
        out_shape=jax.ShapeDtypeStruct((bsz, nh, dh, seq), BF16),
        scratch_shapes=[pltpu.VMEM((2, tile, tile), F32),
                        pltpu.VMEM((2, 1, tile), F32), pltpu.VMEM((2, 1, tile), F32),
                        pltpu.VMEM((2, dh, tile), F32)],
        compiler_params=_cparams(3),
    )(t5_flat, lam.reshape(1), q_t, k, v_t, subln_gain.reshape(dh, 1))


def _kmean_kernel(k_ref, o_ref, *, blocks):
    k = k_ref[...].astype(F32)
    o_ref[...] = jnp.mean(k.reshape(blocks, MOBA_BLOCK, k.shape[-1]), axis=1)


def _moba_kmean(k):
    bsz, nh, seq, dh = k.shape
    nb = seq // MOBA_BLOCK
    blocks = min(8, nb)
    return pl.pallas_call(
        functools.partial(_kmean_kernel, blocks=blocks),
        grid=(bsz, nh, nb // blocks),
        in_specs=[pl.BlockSpec((None, None, blocks * MOBA_BLOCK, dh), lambda b, h, i: (b, h, i, 0))],
        out_specs=pl.BlockSpec((None, None, blocks, dh), lambda b, h, i: (b, h, i, 0)),
        out_shape=jax.ShapeDtypeStruct((bsz, nh, nb, dh), F32),
        compiler_params=_cparams(3),
    )(k)


def _moba_kernel(t5_ref, q_ref, k_ref, v_ref, km_ref, o_ref, bias_ref, sel_ref, m_ref, l_ref, acc_ref,
                 *, tile, head0, n_blocks):
    h = pl.program_id(1)
    i = pl.program_id(2)
    head = head0 + h
    scale = HEAD_DIM ** -0.5

    @pl.when(i == 0)
    def _():
        _fill_t5_tile(bias_ref, 0, t5_ref, head, 0, tile)
        _fill_t5_tile(bias_ref, 1, t5_ref, head, tile, tile)

    _init_stats(m_ref, l_ref, acc_ref)
    q = q_ref[...]

    gate = jnp.dot(km_ref[...], q, preferred_element_type=F32)
    nidx = lax.broadcasted_iota(I32, gate.shape, 0)
    g = jnp.where(nidx < i, gate, NEG)
    sel = jnp.zeros(gate.shape, F32)
    for _ in range(min(MOBA_TOPK, n_blocks)):
        mx = jnp.max(g, axis=0, keepdims=True)
        first = jnp.min(jnp.where(g == mx, nidx, n_blocks), axis=0, keepdims=True)
        pick = nidx == first
        sel = jnp.where(pick, 1.0, sel)
        g = jnp.where(pick, -jnp.inf, g)
    sel_ref[...] = jnp.where(nidx < i, sel, 0.0)

    def step(j, bias, mask):
        k0 = pl.multiple_of(j * tile, tile)
        s = jnp.dot(k_ref[pl.ds(k0, tile), :], q, preferred_element_type=F32) * scale + bias
        s = jnp.where(mask, s, NEG)
        _flash_update(s, v_ref[:, pl.ds(k0, tile)], m_ref, l_ref, acc_ref, 0)

    step(i, bias_ref[0], _causal_mask(tile))

    @pl.when(i >= 1)
    def _():
        step(i - 1, bias_ref[1], sel_ref[pl.ds(i - 1, 1), :] > 0.5)

    far_bias = t5_ref[(T5_BUCKETS - 1) * N_BIAS_HEADS + head]

    def far(j, carry):
        step(j, far_bias, sel_ref[pl.ds(j, 1), :] > 0.5)
        return carry

    lax.fori_loop(0, jnp.maximum(i - 1, 0), far, 0)
    o_ref[...] = (acc_ref[0] / l_ref[0]).astype(o_ref.dtype)


def _moba_attention(q_t, k, v_t, kmean, t5_flat):
    bsz, nh, dh, seq = q_t.shape
    tile = MOBA_BLOCK
    assert SPARSE_TILE == MOBA_BLOCK and seq % tile == 0 and tile >= T5_FAR_DIST
    nb = seq // tile
    kern = functools.partial(_moba_kernel, tile=tile, head0=N_HEADS, n_blocks=nb)
    return pl.pallas_call(
        kern,
        grid=(bsz, nh, nb),
        in_specs=[_smem(),
                  pl.BlockSpec((None, None, dh, tile), lambda b, h, i: (b, h, 0, i)),
                  pl.BlockSpec((None, None, seq, dh), lambda b, h, i: (b, h, 0, 0)),
                  pl.BlockSpec((None, None, dh, seq), lambda b, h, i: (b, h, 0, 0)),
                  pl.BlockSpec((None, None, nb, dh), lambda b, h, i: (b, h, 0, 0))],
        out_specs=pl.BlockSpec((None, None, dh, tile), lambda b, h, i: (b, h, 0, i)),
        out_shape=jax.ShapeDtypeStruct((bsz, nh, dh, seq), BF16),
        scratch_shapes=[pltpu.VMEM((2, tile, tile), F32), pltpu.VMEM((nb, tile), F32),
                        pltpu.VMEM((1, 1, tile), F32), pltpu.VMEM((1, 1, tile), F32),
                        pltpu.VMEM((1, dh, tile), F32)],
        compiler_params=_cparams(3),
    )(t5_flat, q_t, k, v_t, kmean)


def _split3(x):
    def trunc(v):
        bits = lax.bitcast_convert_type(v, I32)
        return lax.bitcast_convert_type(bits & jnp.int32(-65536), F32)
    hi = trunc(x)
    r1 = x - hi
    mid = trunc(r1)
    lo = trunc(r1 - mid)
    return hi, mid, lo


def _forget_cumsum_kernel(f_ref, hi_ref, mid_ref, lo_ref):
    x = f_ref[...]
    rows = x.shape[0]
    ls = jnp.minimum(x, 0.0) - jnp.log(1.0 + jnp.exp(-jnp.abs(x)))
    upper = (lax.broadcasted_iota(I32, (128, 128), 0) <= lax.broadcasted_iota(I32, (128, 128), 1)).astype(F32)
    within = jnp.dot(ls, upper, preferred_element_type=F32, precision=lax.Precision.HIGHEST)
    strict = (lax.broadcasted_iota(I32, (rows, rows), 1) < lax.broadcasted_iota(I32, (rows, rows), 0)).astype(F32)
    before = jnp.dot(strict, within, preferred_element_type=F32, precision=lax.Precision.HIGHEST)
    cf = within + before[:, 127:128]
    hi, mid, lo = _split3(cf)
    hi_ref[...] = hi
    mid_ref[...] = mid
    lo_ref[...] = lo


def _forget_cumsum(f_logit):
    bsz, seq, nh = f_logit.shape
    rows = seq // 128
    f = f_logit.transpose(0, 2, 1).reshape(bsz, nh, rows, 128)
    spec = pl.BlockSpec((None, None, rows, 128), lambda b, h: (b, h, 0, 0))
    parts = pl.pallas_call(
        _forget_cumsum_kernel,
        grid=(bsz, nh),
        in_specs=[spec],
        out_specs=[spec] * 3,
        out_shape=[jax.ShapeDtypeStruct((bsz, nh, rows, 128), F32)] * 3,
        compiler_params=_cparams(2),
    )(f)
    return [p.reshape(bsz, nh, seq) for p in parts]


def _forget_kernel(q_ref, k_ref, v_ref, gate_ref, o_ref, m_ref, l_ref, acc_ref, *, tile):
    i = pl.program_id(2)
    _init_stats(m_ref, l_ref, acc_ref)
    q = q_ref[...]

    def step(j, mask):
        k0 = pl.multiple_of(j * tile, tile)
        s = jnp.dot(k_ref[pl.ds(k0, tile), :], q, preferred_element_type=F32)
        if mask is not None:
            s = jnp.where(mask, s, NEG)
        _flash_update(s, v_ref[:, pl.ds(k0, tile)], m_ref, l_ref, acc_ref, 0)

    step(i, _causal_mask(tile))

    def past(j, carry):
        step(j, None)
        return carry

    lax.fori_loop(0, i, past, 0)
    g = gate_ref[...]
    o_ref[...] = ((acc_ref[0] / l_ref[0]) * (1.0 / (1.0 + jnp.exp(-g)))).astype(o_ref.dtype)


def _forgetting_attention(q_aug_t, k_aug, v_t, gate_t):
    bsz, nh, kdim, seq = q_aug_t.shape
    dh = v_t.shape[2]
    tile = min(ATTN_TILE, seq)
    assert seq % tile == 0
    return pl.pallas_call(
        functools.partial(_forget_kernel, tile=tile),
        grid=(bsz, nh, seq // tile),
        in_specs=[pl.BlockSpec((None, None, kdim, tile), lambda b, h, i: (b, h, 0, i)),
                  pl.BlockSpec((None, None, seq, kdim), lambda b, h, i: (b, h, 0, 0)),
                  pl.BlockSpec((None, None, dh, seq), lambda b, h, i: (b, h, 0, 0)),
                  pl.BlockSpec((None, None, dh, tile), lambda b, h, i: (b, h, 0, i))],
        out_specs=pl.BlockSpec((None, None, dh, tile), lambda b, h, i: (b, h, 0, i)),
        out_shape=jax.ShapeDtypeStruct((bsz, nh, dh, seq), BF16),
        scratch_shapes=[pltpu.VMEM((1, 1, tile), F32), pltpu.VMEM((1, 1, tile), F32),
                        pltpu.VMEM((1, dh, tile), F32)],
        compiler_params=_cparams(3),
    )(q_aug_t, k_aug, v_t, gate_t)


def _key_to_float(u):
    ks = u ^ jnp.int32(-2147483648)
    bits = jnp.where(ks < 0, ks ^ jnp.int32(2147483647), ks)
    return lax.bitcast_convert_type(bits, F32)


def _dsa_kernel(t5_ref, qi_ref, w_ref, ki_ref, q_ref, k_ref, v_ref, o_ref,
                sc_ref, bias_ref, tri_ref, m_ref, l_ref, acc_ref, *, tile, head0, topk):
    b = pl.program_id(0)
    i = pl.program_id(1)

    @pl.when((b == 0) & (i == 0))
    def _():
        for h in range(N_HEADS):
            _fill_t5_tile(bias_ref, 2 * h, t5_ref, head0 + h, 0, tile)
            _fill_t5_tile(bias_ref, 2 * h + 1, t5_ref, head0 + h, tile, tile)
        tri_ref[...] = jnp.where(lax.broadcasted_iota(I32, (tile, tile), 1) <= lax.broadcasted_iota(I32, (tile, tile), 0),
                                 1.0, 0.0).astype(BF16)

    _init_stats(m_ref, l_ref, acc_ref)
    causal = _causal_mask(tile)

    def index_scores(j):
        k0 = pl.multiple_of(j * tile, tile)
        kt = ki_ref[pl.ds(k0, tile), :]
        sc = jnp.zeros((tile, tile), F32)
        for h in range(IDX_HEADS):
            raw = jnp.dot(kt, qi_ref[h * IDX_DIM:(h + 1) * IDX_DIM, :], preferred_element_type=F32)
            sc = sc + jnp.maximum(raw, 0.0) * w_ref[h:h + 1, :]
        return k0, sc

    def score_body(j, carry):
        k0, sc = index_scores(j)
        sc_ref[pl.ds(k0, tile), :] = sc
        return carry

    lax.fori_loop(0, i, score_body, 0)
    k0, sc = index_scores(i)
    sc_ref[pl.ds(k0, tile), :] = jnp.where(causal, sc, NEG)

    def count(pred):
        def body(j, cnt):
            k0 = pl.multiple_of(j * tile, tile)
            return cnt + jnp.sum(jnp.where(pred(sc_ref[pl.ds(k0, tile), :]), 1.0, 0.0), axis=0, keepdims=True)
        return lax.fori_loop(0, i + 1, body, jnp.zeros((1, tile), F32))

    def bit_body(it, ans):
        cand = ans | jnp.left_shift(jnp.int32(1), 31 - it)
        thr = _key_to_float(cand)
        cnt = count(lambda s: s >= thr)
        return jnp.where(cnt >= topk, cand, ans)

    ans = lax.fori_loop(0, 32, bit_body, jnp.zeros((1, tile), I32))
    tau = _key_to_float(ans)
    need = topk - count(lambda s: s > tau)

    qs = [q_ref[h] for h in range(N_HEADS)]

    def attend(j, run, kind):
        k0 = pl.multiple_of(j * tile, tile)
        sc = sc_ref[pl.ds(k0, tile), :]
        eq = sc == tau
        eqf = jnp.where(eq, 1.0, 0.0)
        rank = run + jnp.dot(tri_ref[...], eqf.astype(BF16), preferred_element_type=F32)
        wgt = jnp.where(eq, jnp.where(rank <= need, 1.0, 0.0), jnp.where(sc > tau, 1.0, 0.0))
        if kind == "diag":
            wgt = jnp.where(causal, wgt, 0.0)
        keep = wgt > 0.5
        kt = k_ref[pl.ds(k0, tile), :]
        for h in range(N_HEADS):
            if kind == "far":
                bias = t5_ref[(T5_BUCKETS - 1) * N_BIAS_HEADS + head0 + h]
            else:
                bias = bias_ref[2 * h + (1 if kind == "near" else 0)]
            s = jnp.dot(kt, qs[h], preferred_element_type=F32) + bias
            s = jnp.where(keep, s, NEG)
            _flash_update(s, v_ref[h * HEAD_DIM:(h + 1) * HEAD_DIM, pl.ds(k0, tile)], m_ref, l_ref, acc_ref, h, weight=wgt)
        return run + jnp.sum(eqf, axis=0, keepdims=True)

    run = lax.fori_loop(0, jnp.maximum(i - 1, 0), lambda j, r: attend(j, r, "far"), jnp.zeros((1, tile), F32))
    run = lax.cond(i >= 1, lambda r: attend(i - 1, r, "near"), lambda r: r, run)
    attend(i, run, "diag")
    for h in range(N_HEADS):
        o_ref[h * HEAD_DIM:(h + 1) * HEAD_DIM, :] = (acc_ref[h] / l_ref[h]).astype(o_ref.dtype)


def _dsa_attention(qi_t, w_t, k_idx, q_t, k, v_t, t5_flat):
    bsz, _, width, seq = q_t.shape
    tile = min(SPARSE_TILE, seq)
    topk = min(DSA_TOPK_MAX, seq // 4)
    assert seq % tile == 0 and tile >= topk and tile >= T5_FAR_DIST
    kern = functools.partial(_dsa_kernel, tile=tile, head0=2 * N_HEADS, topk=topk)
    return pl.pallas_call(
        kern,
        grid=(bsz, seq // tile),
        in_specs=[_smem(),
                  pl.BlockSpec((None, qi_t.shape[1], tile), lambda b, i: (b, 0, i)),
                  pl.BlockSpec((None, 8, tile), lambda b, i: (b, 0, i)),
                  _resident((None, seq, k_idx.shape[2]), lambda b, i: (b, 0, 0)),
                  pl.BlockSpec((None, N_HEADS, width, tile), lambda b, i: (b, 0, 0, i)),
                  _resident((None, seq, width), lambda b, i: (b, 0, 0)),
                  _resident((None, width, seq), lambda b, i: (b, 0, 0))],
        out_specs=pl.BlockSpec((None, width, tile), lambda b, i: (b, 0, i)),
        out_shape=jax.ShapeDtypeStruct((bsz, width, seq), BF16),
        scratch_shapes=[pltpu.VMEM((seq, tile), F32),
                        pltpu.VMEM((2 * N_HEADS, tile, tile), F32),
                        pltpu.VMEM((tile, tile), BF16),
                        pltpu.VMEM((N_HEADS, 1, tile), F32), pltpu.VMEM((N_HEADS, 1, tile), F32),
                        pltpu.VMEM((N_HEADS, HEAD_DIM, tile), F32)],
        compiler_params=_cparams(2),
    )(t5_flat, qi_t, w_t, k_idx, q_t, k, v_t)


def _outproj_kernel(x_ref, o_ref, w_ref, g_ref, y_ref):
    y_ref[...] = x_ref[...] + g_ref[...] * jnp.dot(o_ref[...], w_ref[...], preferred_element_type=F32)


def _output_projection(x, o, w_out, gate):
    bsz, seq, d = x.shape
    tm = ROW_TILE
    row = pl.BlockSpec((None, tm, d), lambda b, i: (b, i, 0))
    return pl.pallas_call(
        _outproj_kernel,
        grid=(bsz, seq // tm),
        in_specs=[row, pl.BlockSpec((None, tm, o.shape[2]), lambda b, i: (b, i, 0)),
                  _resident(w_out.shape, lambda b, i: (0, 0)),
                  pl.BlockSpec((None, 1, d), lambda b, i: (b, 0, 0))],
        out_specs=row,
        out_shape=jax.ShapeDtypeStruct(x.shape, F32),
        compiler_params=_cparams(2),
    )(x, o, w_out.astype(BF16), gate)


HALO = 8


def _ffn_kernel(x_ref, xp_ref, gain_ref, sc_ref, sh_ref, g_ref, wg_ref, wv_ref, cw_ref, cb_ref, wd_ref, fg_ref,
                y_ref, h_ref, acc_ref, *, tm, n_chunks, final_norm):
    i = pl.program_id(1)
    c = pl.program_id(2)

    @pl.when(c == 0)
    def _():
        def modulated(x):
            return ((_rms(x) * gain_ref[...]) * (1.0 + sc_ref[...]) + sh_ref[...]).astype(BF16)
        h_ref[HALO:, :] = modulated(x_ref[...])
        halo = modulated(xp_ref[...])
        h_ref[:HALO, :] = jnp.where(i > 0, halo, jnp.zeros_like(halo))
        acc_ref[...] = jnp.zeros(acc_ref.shape, F32)

    h = h_ref[...]

    def conv(w_ref, half):
        u = jnp.dot(h, w_ref[...], preferred_element_type=F32)
        cw = cw_ref[half]
        return (cw[0:1] * u[HALO - 2:HALO - 2 + tm] + cw[1:2] * u[HALO - 1:HALO - 1 + tm]
                + cw[2:3] * u[HALO:HALO + tm]) + cb_ref[half]

    gate = conv(wg_ref, 0)
    val = conv(wv_ref, 1)
    a = (gate * (1.0 / (1.0 + jnp.exp(-gate)))) * val
    acc_ref[...] += jnp.dot(a.astype(BF16), wd_ref[...], preferred_element_type=F32)

    @pl.when(c == n_chunks - 1)
    def _():
        y = x_ref[...] + g_ref[...] * acc_ref[...]
        if final_norm:
            y = _rms(y) * fg_ref[...]
        y_ref[...] = y


def _ffn(x, gain, sc, sh, gate, w_up, conv_w, conv_b, w_down, final_gain, final_norm):
    bsz, seq, d = x.shape
    tm = min(FFN_ROW_TILE, seq)
    fc = FFN_CHUNK
    n_chunks = D_FF // fc
    w_up_b = w_up.astype(BF16)
    cw = conv_w.reshape(3, 2, D_FF).transpose(1, 0, 2)
    cb = conv_b.reshape(2, 1, D_FF)
    vec = pl.BlockSpec((None, 1, d), lambda b, i, c: (b, 0, 0))
    one = pl.BlockSpec((1, d), lambda b, i, c: (0, 0))
    row = pl.BlockSpec((None, tm, d), lambda b, i, c: (b, i, 0))
    return pl.pallas_call(
        functools.partial(_ffn_kernel, tm=tm, n_chunks=n_chunks, final_norm=final_norm),
        grid=(bsz, seq // tm, n_chunks),
        in_specs=[row,
                  pl.BlockSpec((None, HALO, d), lambda b, i, c: (b, jnp.maximum(i * (tm // HALO) - 1, 0), 0)),
                  one, vec, vec, vec,
                  pl.BlockSpec((d, fc), lambda b, i, c: (0, c)),
                  pl.BlockSpec((d, fc), lambda b, i, c: (0, n_chunks + c)),
                  pl.BlockSpec((2, 3, fc), lambda b, i, c: (0, 0, c)),
                  pl.BlockSpec((2, 1, fc), lambda b, i, c: (0, 0, c)),
                  pl.BlockSpec((fc, d), lambda b, i, c: (c, 0)),
                  one],
        out_specs=row,
        out_shape=jax.ShapeDtypeStruct(x.shape, F32),
        scratch_shapes=[pltpu.VMEM((HALO + tm, d), BF16), pltpu.VMEM((tm, d), F32)],
        compiler_params=_cparams(3),
    )(x, x, gain.reshape(1, d), sc, sh, gate, w_up_b, w_up_b, cw, cb, w_down.astype(BF16),
      final_gain.reshape(1, d))


def kernel(x, c, w_ada, b_ada, norm1_gain, w_in, f_bias, diff_lambda, diff_subln_gain, w_dq_up, w_didx_q,
           w_dkv_up, w_out, t5_table, norm2_gain, w_ffn_up, ffn_conv_w, ffn_conv_b, w_ffn_down, final_gain):
    bsz, seq, d = x.shape
    depth = w_ada.shape[0]
    t5_flat = t5_table.reshape(-1)
    mod = _modulation(c, w_ada, b_ada)
    w = GROUP_W

    for l in range(depth):
        sh1, sc1, g1, sh2, sc2, g2 = [m[:, None, :] for m in jnp.split(mod[l], 6, axis=-1)]
        pa, pb, pc, gc, dq, dkv, misc = _input_projection(x, norm1_gain[l], sc1, sh1, w_in[l],
                                                          w_dq_up[l], w_didx_q[l], w_dkv_up[l])
        k_idx = misc[..., :IDX_DIM].astype(BF16)
        fc = misc[..., IDX_DIM:IDX_DIM + N_HEADS]
        w_idx = misc[..., IDX_DIM + N_HEADS:IDX_DIM + N_HEADS + IDX_HEADS]

        lambda_init = 0.8 - 0.6 * math.exp(-0.3 * l)
        lq1, lk1, lq2, lk2 = diff_lambda[l]
        lam = jnp.exp(jnp.sum(lq1 * lk1)) - jnp.exp(jnp.sum(lq2 * lk2)) + lambda_init
        qa_t = _heads_t(pa[..., :w]).reshape(bsz, N_HEADS, 2, DIFF_HALF, seq)
        zeros = jnp.zeros_like(qa_t)
        qa_t = jnp.stack([jnp.concatenate([qa_t[:, :, 0], zeros[:, :, 0]], axis=2),
                          jnp.concatenate([zeros[:, :, 1], qa_t[:, :, 1]], axis=2)], axis=2)
        o_a = _diff_attention(qa_t, _heads(pa[..., w:2 * w]), _heads_t(pa[..., 2 * w:]), lam,
                              diff_subln_gain[l], t5_flat, 1.0 - lambda_init)

        kb = _heads(pb[..., w:2 * w])
        o_b = _moba_attention(_heads_t(pb[..., :w]), kb, _heads_t(pb[..., 2 * w:]),
                              _moba_kmean(kb).astype(BF16), t5_flat)

        hi, mid, lo = _forget_cumsum(fc + f_bias[l])
        ones = jnp.ones_like(hi)
        cparts = jnp.stack([hi, mid, lo], axis=2)
        onep = jnp.stack([ones] * 3, axis=2)
        padw = 2 * HEAD_DIM - HEAD_DIM - 6
        qc_t = _heads_t(pc[..., :w]) * jnp.asarray(HEAD_DIM ** -0.5, BF16)
        qc_aug = jnp.concatenate([qc_t, onep.astype(BF16), cparts.astype(BF16),
                                  jnp.zeros((bsz, N_HEADS, padw, seq), BF16)], axis=2)
        kc_aug = jnp.concatenate([_heads(pc[..., w:2 * w]), (-cparts).transpose(0, 1, 3, 2).astype(BF16),
                                  onep.transpose(0, 1, 3, 2).astype(BF16),
                                  jnp.zeros((bsz, N_HEADS, seq, padw), BF16)], axis=3)
        o_c = _forgetting_attention(qc_aug, kc_aug, _heads_t(pc[..., 2 * w:]), _heads_t(gc))

        qd_t = (dq[..., :w] * jnp.asarray(HEAD_DIM ** -0.5, BF16)).transpose(0, 2, 1)
        head_of_row = jnp.arange(w)[None, :, None] // HEAD_DIM
        qd_t = jnp.stack([jnp.where(head_of_row == h, qd_t, jnp.zeros((), BF16)) for h in range(N_HEADS)],
                         axis=1)
        qi_t = dq[..., w:].transpose(0, 2, 1)
        w_t = (w_idx * (IDX_HEADS ** -0.5 * IDX_DIM ** -0.5)).transpose(0, 2, 1)
        w_t = jnp.concatenate([w_t, jnp.zeros((bsz, 8 - IDX_HEADS, seq), F32)], axis=1)
        o_d = _dsa_attention(qi_t, w_t, k_idx, qd_t, dkv[..., :w], dkv[..., w:].transpose(0, 2, 1), t5_flat)

        o = jnp.concatenate([_unheads_t(o_a), _unheads_t(o_b), _unheads_t(o_c), o_d.transpose(0, 2, 1)], axis=-1)
        x = _output_projection(x, o, w_out[l], g1)
        x = _ffn(x, norm2_gain[l], sc2, sh2, g2, w_ffn_up[l], ffn_conv_w[l], ffn_conv_b[l], w_ffn_down[l],
                 final_gain, final_norm=(l == depth - 1))
    return x
```

```python
import functools
import math

import numpy as np
import jax
import jax.numpy as jnp
from jax import lax
from jax.experimental import pallas as pl
from jax.experimental.pallas import tpu as pltpu

F32 = jnp.float32
BF16 = jnp.bfloat16
I32 = jnp.int32

HEAD_DIM = 64
N_HEADS = 4
GROUP_W = N_HEADS * HEAD_DIM
DIFF_HALF = HEAD_DIM // 2
MOBA_BLOCK = 256
MOBA_TOPK = 3
DSA_RANK = 128
IDX_HEADS = 4
IDX_DIM = 64
DSA_TOPK_MAX = 256
T5_BUCKETS = 32
T5_MAX_DIST = 128
N_BIAS_HEADS = 12
D_FF = 2816
EPS = 1e-6
NEG = -1e30
LOG2E = math.log2(math.e)

V7X_VMEM_LIMIT_BYTES = 56 * 1024 * 1024

ATTN_TILE = 512
SPARSE_TILE = 256
ROW_TILE = 512
FFN_ROW_TILE = 1024
FFN_CHUNK = 256
FORGET_HEADS_PER_STEP = 4
DIFF_HEADS_PER_STEP = 2


def _t5_thresholds():
    d = np.arange(0, 4 * T5_MAX_DIST)
    max_exact = T5_BUCKETS // 2
    ratio = np.maximum(d, 1).astype(np.float32) / max_exact
    large = max_exact + (np.log(ratio) / math.log(T5_MAX_DIST / max_exact) * (T5_BUCKETS - max_exact)).astype(np.int32)
    bucket = np.where(d < max_exact, d, np.minimum(large, T5_BUCKETS - 1))
    return [int(np.argmax(bucket >= b)) for b in range(T5_BUCKETS)]


T5_THRESH = _t5_thresholds()
T5_FAR_DIST = T5_THRESH[-1]


def _cparams(n_axes):
    return pltpu.CompilerParams(dimension_semantics=("arbitrary",) * n_axes,
                                vmem_limit_bytes=V7X_VMEM_LIMIT_BYTES)


def _resident(block_shape, index_map):
    return pl.BlockSpec(block_shape, index_map, pipeline_mode=pl.Buffered(1))


def _smem():
    return pl.BlockSpec(memory_space=pltpu.SMEM)


def _fill_t5_tile(bias_ref, slot, t5_ref, head, delta, tile):
    rows = 64

    def body(r, carry):
        r0 = pl.multiple_of(r * rows, rows)
        kr = lax.broadcasted_iota(I32, (rows, tile), 0) + r0
        qc = lax.broadcasted_iota(I32, (rows, tile), 1)
        d = delta + qc - kr
        val = jnp.full((rows, tile), LOG2E * t5_ref[head], F32)
        for b in range(1, T5_BUCKETS):
            val = jnp.where(d >= T5_THRESH[b], LOG2E * t5_ref[b * N_BIAS_HEADS + head], val)
        bias_ref[slot, pl.ds(r0, rows), :] = val
        return carry

    lax.fori_loop(0, tile // rows, body, 0)


def _causal_mask(tile):
    kr = lax.broadcasted_iota(I32, (tile, tile), 0)
    qc = lax.broadcasted_iota(I32, (tile, tile), 1)
    return kr <= qc


def _flash_update(s, v_t, m_ref, l_ref, acc_ref, idx, weight=None):
    m_old = m_ref[idx]
    m_new = jnp.maximum(m_old, jnp.max(s, axis=0, keepdims=True))
    p = jnp.exp2(s - m_new)
    if weight is not None:
        p = p * weight
    alpha = jnp.exp2(m_old - m_new)
    l_ref[idx] = alpha * l_ref[idx] + jnp.sum(p, axis=0, keepdims=True)
    acc_ref[idx] = alpha * acc_ref[idx] + jnp.dot(v_t, p.astype(BF16), preferred_element_type=F32)
    m_ref[idx] = m_new


def _init_stats(m_ref, l_ref, acc_ref):
    m_ref[...] = jnp.full(m_ref.shape, NEG, F32)
    l_ref[...] = jnp.zeros(l_ref.shape, F32)
    acc_ref[...] = jnp.zeros(acc_ref.shape, F32)


def _mod_kernel(c_ref, w_ref, b_ref, o_ref):
    c = c_ref[...]
    cond = c * (1.0 / (1.0 + jnp.exp(-c)))
    o_ref[...] = jnp.dot(cond.astype(BF16), w_ref[...].astype(BF16), preferred_element_type=F32) + b_ref[...]


def _modulation(c, w_ada, b_ada):
    depth, d, n = w_ada.shape
    bsz = c.shape[0]
    rows = 8
    c_pad = jnp.zeros((rows, d), F32).at[:bsz].set(c)
    tn = 1024
    out = pl.pallas_call(
        _mod_kernel,
        grid=(depth, n // tn),
        in_specs=[pl.BlockSpec((rows, d), lambda l, j: (0, 0)),
                  pl.BlockSpec((None, d, tn), lambda l, j: (l, 0, j)),
                  pl.BlockSpec((None, 1, tn), lambda l, j: (l, 0, j))],
        out_specs=pl.BlockSpec((None, rows, tn), lambda l, j: (l, 0, j)),
        out_shape=jax.ShapeDtypeStruct((depth, rows, n), F32),
        compiler_params=_cparams(2),
        name="adaln_modulation",
    )(c_pad, w_ada, b_ada.reshape(depth, 1, n))
    return out[:, :bsz]


def _rms(x):
    return x * lax.rsqrt(jnp.mean(x * x, axis=-1, keepdims=True) + EPS)


def _inproj_kernel(x_ref, gain_ref, sc_ref, sh_ref, w_ref, wq_ref, wkv_ref,
                   pa_ref, pb_ref, pc_ref, gc_ref, dq_ref, dkv_ref, misc_ref):
    x = x_ref[...]
    h = (_rms(x) * gain_ref[...]) * (1.0 + sc_ref[...]) + sh_ref[...]
    hb = h.astype(BF16)
    w3 = 3 * GROUP_W

    def proj(c0, c1):
        return jnp.dot(hb, w_ref[:, c0:c1], preferred_element_type=F32)

    pa_ref[...] = proj(0, w3).astype(BF16)
    pb_ref[...] = proj(w3, 2 * w3).astype(BF16)
    pc_ref[...] = proj(2 * w3, 3 * w3).astype(BF16)
    c0 = 3 * w3
    gc_ref[...] = proj(c0, c0 + GROUP_W)
    c0 += GROUP_W
    q_lat = _rms(proj(c0, c0 + DSA_RANK))
    kv_lat = _rms(proj(c0 + DSA_RANK, c0 + 2 * DSA_RANK))
    dq_ref[...] = jnp.dot(q_lat.astype(BF16), wq_ref[...], preferred_element_type=F32).astype(BF16)
    dkv_ref[...] = jnp.dot(kv_lat.astype(BF16), wkv_ref[...], preferred_element_type=F32).astype(BF16)
    c0 += 2 * DSA_RANK
    misc_ref[...] = proj(c0, c0 + 128)


def _input_projection(x, gain, sc, sh, w_in, w_dq_up, w_didx_q, w_dkv_up):
    bsz, seq, d = x.shape
    w3 = 3 * GROUP_W
    o_c = 2 * w3
    o_fc = o_c + w3
    o_gc = o_fc + N_HEADS
    o_ql = o_gc + GROUP_W
    o_kv = o_ql + DSA_RANK
    o_ki = o_kv + DSA_RANK
    o_wi = o_ki + IDX_DIM
    pad = jnp.zeros((d, 128 - IDX_DIM - N_HEADS - IDX_HEADS), w_in.dtype)
    col = jnp.arange(o_fc)
    is_q = (col % w3) < GROUP_W
    q_scale = jnp.where(is_q, jnp.where(col < w3, DIFF_HALF ** -0.5, HEAD_DIM ** -0.5) * LOG2E, 1.0).astype(F32)
    w = jnp.concatenate([w_in[:, :o_fc] * q_scale[None, :], w_in[:, o_gc:o_ql], w_in[:, o_ql:o_ki],
                         w_in[:, o_ki:o_wi], w_in[:, o_fc:o_gc], w_in[:, o_wi:], pad], axis=1).astype(BF16)
    n_cols = w.shape[1]
    wq = jnp.concatenate([w_dq_up * (HEAD_DIM ** -0.5 * LOG2E), w_didx_q], axis=1).astype(BF16)
    wkv = w_dkv_up.astype(BF16)
    tm = ROW_TILE
    row = lambda width: pl.BlockSpec((None, tm, width), lambda b, i: (b, i, 0))
    vec = pl.BlockSpec((None, 1, d), lambda b, i: (b, 0, 0))
    outs = pl.pallas_call(
        _inproj_kernel,
        grid=(bsz, seq // tm),
        in_specs=[row(d), pl.BlockSpec((1, d), lambda b, i: (0, 0)), vec, vec,
                  _resident((d, n_cols), lambda b, i: (0, 0)),
                  _resident(wq.shape, lambda b, i: (0, 0)),
                  _resident(wkv.shape, lambda b, i: (0, 0))],
        out_specs=[row(w3), row(w3), row(w3), row(GROUP_W), row(512), row(512), row(128)],
        out_shape=[jax.ShapeDtypeStruct((bsz, seq, w3), BF16)] * 3
                  + [jax.ShapeDtypeStruct((bsz, seq, GROUP_W), F32)]
                  + [jax.ShapeDtypeStruct((bsz, seq, 512), BF16)] * 2
                  + [jax.ShapeDtypeStruct((bsz, seq, 128), F32)],
        compiler_params=_cparams(2),
        name="input_projection",
    )(x, gain.reshape(1, d), sc, sh, w, wq, wkv)
    return outs


def _heads_t(t):
    bsz, seq, w = t.shape
    return t.reshape(bsz, seq, N_HEADS, w // N_HEADS).transpose(0, 2, 3, 1)


def _heads(t):
    bsz, seq, w = t.shape
    return t.reshape(bsz, seq, N_HEADS, w // N_HEADS).transpose(0, 2, 1, 3)


def _unheads_t(o_t):
    bsz, h, dh, seq = o_t.shape
    return o_t.transpose(0, 3, 1, 2).reshape(bsz, seq, h * dh)


def _diff_kernel(t5_ref, lam_ref, q_ref, k_ref, v_ref, g_ref, o_ref, bias_ref, m_ref, l_ref, acc_ref,
                 *, tile, head0, heads, out_scale):
    hg = pl.program_id(1)
    i = pl.program_id(2)

    @pl.when(i == 0)
    def _():
        for h in range(heads):
            _fill_t5_tile(bias_ref, 2 * h, t5_ref, head0 + hg * heads + h, 0, tile)
            _fill_t5_tile(bias_ref, 2 * h + 1, t5_ref, head0 + hg * heads + h, tile, tile)

    _init_stats(m_ref, l_ref, acc_ref)

    def step(j, kind):
        k0 = pl.multiple_of(j * tile, tile)
        logits = [jnp.dot(k_ref[h, pl.ds(k0, tile), :], q_ref[h, c], preferred_element_type=F32)
                  for h in range(heads) for c in range(2)]
        for h in range(heads):
            if kind == "far":
                bias = LOG2E * t5_ref[(T5_BUCKETS - 1) * N_BIAS_HEADS + head0 + hg * heads + h]
            else:
                bias = bias_ref[2 * h + (1 if kind == "near" else 0)]
            vt = v_ref[h, :, pl.ds(k0, tile)]
            for c in range(2):
                s = logits[2 * h + c] + bias
                if kind == "diag":
                    s = jnp.where(_causal_mask(tile), s, NEG)
                _flash_update(s, vt, m_ref, l_ref, acc_ref, 2 * h + c)

    step(i, "diag")

    @pl.when(i >= 1)
    def _():
        step(i - 1, "near")

    def far(j, carry):
        step(j, "far")
        return carry

    lax.fori_loop(0, jnp.maximum(i - 1, 0), far, 0)

    for h in range(heads):
        o = acc_ref[2 * h] / l_ref[2 * h] - lam_ref[0] * (acc_ref[2 * h + 1] / l_ref[2 * h + 1])
        y = o * lax.rsqrt(jnp.mean(o * o, axis=0, keepdims=True) + EPS)
        o_ref[h] = ((y * g_ref[...]) * out_scale).astype(o_ref.dtype)


def _diff_attention(q_t, k, v_t, lam, subln_gain, t5_flat, out_scale):
    bsz, nh, _, dh, seq = q_t.shape
    tile = min(ATTN_TILE, seq)
    hps = DIFF_HEADS_PER_STEP
    assert tile >= T5_FAR_DIST and seq % tile == 0 and nh % hps == 0
    kern = functools.partial(_diff_kernel, tile=tile, head0=0, heads=hps, out_scale=out_scale)
    return pl.pallas_call(
        kern,
        grid=(bsz, nh // hps, seq // tile),
        in_specs=[_smem(), _smem(),
                  pl.BlockSpec((None, hps, 2, dh, tile), lambda b, h, i: (b, h, 0, 0, i)),
                  _resident((None, hps, seq, dh), lambda b, h, i: (b, h, 0, 0)),
                  _resident((None, hps, dh, seq), lambda b, h, i: (b, h, 0, 0)),
                  pl.BlockSpec((dh, 1), lambda b, h, i: (0, 0))],
        out_specs=pl.BlockSpec((None, hps, dh, tile), lambda b, h, i: (b, h, 0, i)),
        out_shape=jax.ShapeDtypeStruct((bsz, nh, dh, seq), BF16),
        scratch_shapes=[pltpu.VMEM((2 * hps, tile, tile), F32),
                        pltpu.VMEM((2 * hps, 1, tile), F32), pltpu.VMEM((2 * hps, 1, tile), F32),
                        pltpu.VMEM((2 * hps, dh, tile), F32)],
        compiler_params=_cparams(3),
        name="diff_attention",
    )(t5_flat, lam.reshape(1), q_t, k, v_t, subln_gain.reshape(dh, 1))


def _kmean_kernel(k_ref, o_ref, *, blocks):
    k = k_ref[...].astype(F32)
    o_ref[...] = jnp.mean(k.reshape(blocks, MOBA_BLOCK, k.shape[-1]), axis=1)


def _moba_kmean(k):
    bsz, nh, seq, dh = k.shape
    nb = seq // MOBA_BLOCK
    blocks = min(8, nb)
    return pl.pallas_call(
        functools.partial(_kmean_kernel, blocks=blocks),
        grid=(bsz, nh, nb // blocks),
        in_specs=[pl.BlockSpec((None, None, blocks * MOBA_BLOCK, dh), lambda b, h, i: (b, h, i, 0))],
        out_specs=pl.BlockSpec((None, None, blocks, dh), lambda b, h, i: (b, h, i, 0)),
        out_shape=jax.ShapeDtypeStruct((bsz, nh, nb, dh), F32),
        compiler_params=_cparams(3),
    )(k)


def _moba_kernel(t5_ref, q_ref, k_ref, v_ref, km_ref, o_ref, bias_ref, sel_ref, m_ref, l_ref, acc_ref,
                 *, tile, head0, heads, n_blocks):
    i = pl.program_id(1)

    @pl.when(i == 0)
    def _():
        for h in range(heads):
            _fill_t5_tile(bias_ref, 2 * h, t5_ref, head0 + h, 0, tile)
            _fill_t5_tile(bias_ref, 2 * h + 1, t5_ref, head0 + h, tile, tile)

    _init_stats(m_ref, l_ref, acc_ref)

    nidx = lax.broadcasted_iota(I32, (n_blocks, tile), 0)
    for h in range(heads):
        gate = jnp.dot(km_ref[h], q_ref[h], preferred_element_type=F32)
        g = jnp.where(nidx < i, gate, NEG)
        sel = jnp.zeros(gate.shape, F32)
        for _ in range(min(MOBA_TOPK, n_blocks)):
            mx = jnp.max(g, axis=0, keepdims=True)
            first = jnp.min(jnp.where(g == mx, nidx, n_blocks), axis=0, keepdims=True)
            pick = nidx == first
            sel = jnp.where(pick, 1.0, sel)
            g = jnp.where(pick, -jnp.inf, g)
        sel_ref[h] = jnp.where(nidx < i, sel, 0.0)

    def step(j, kind):
        k0 = pl.multiple_of(j * tile, tile)
        logits = [jnp.dot(k_ref[h, pl.ds(k0, tile), :], q_ref[h], preferred_element_type=F32) for h in range(heads)]
        for h in range(heads):
            if kind == "far":
                bias = LOG2E * t5_ref[(T5_BUCKETS - 1) * N_BIAS_HEADS + head0 + h]
            else:
                bias = bias_ref[2 * h + (1 if kind == "near" else 0)]
            mask = _causal_mask(tile) if kind == "diag" else sel_ref[h, pl.ds(j, 1), :] > 0.5
            s = jnp.where(mask, logits[h] + bias, NEG)
            _flash_update(s, v_ref[h, :, pl.ds(k0, tile)], m_ref, l_ref, acc_ref, h)

    step(i, "diag")

    @pl.when(i >= 1)
    def _():
        step(i - 1, "near")

    def far(j, carry):
        step(j, "far")
        return carry

    lax.fori_loop(0, jnp.maximum(i - 1, 0), far, 0)
    for h in range(heads):
        o_ref[h] = (acc_ref[h] / l_ref[h]).astype(o_ref.dtype)


def _moba_attention(q_t, k, v_t, kmean, t5_flat):
    bsz, nh, dh, seq = q_t.shape
    tile = MOBA_BLOCK
    assert SPARSE_TILE == MOBA_BLOCK and seq % tile == 0 and tile >= T5_FAR_DIST
    nb = seq // tile
    kern = functools.partial(_moba_kernel, tile=tile, head0=N_HEADS, heads=nh, n_blocks=nb)
    return pl.pallas_call(
        kern,
        grid=(bsz, nb),
        in_specs=[_smem(),
                  pl.BlockSpec((None, nh, dh, tile), lambda b, i: (b, 0, 0, i)),
                  _resident((None, nh, seq, dh), lambda b, i: (b, 0, 0, 0)),
                  _resident((None, nh, dh, seq), lambda b, i: (b, 0, 0, 0)),
                  _resident((None, nh, nb, dh), lambda b, i: (b, 0, 0, 0))],
        out_specs=pl.BlockSpec((None, nh, dh, tile), lambda b, i: (b, 0, 0, i)),
        out_shape=jax.ShapeDtypeStruct((bsz, nh, dh, seq), BF16),
        scratch_shapes=[pltpu.VMEM((2 * nh, tile, tile), F32), pltpu.VMEM((nh, nb, tile), F32),
                        pltpu.VMEM((nh, 1, tile), F32), pltpu.VMEM((nh, 1, tile), F32),
                        pltpu.VMEM((nh, dh, tile), F32)],
        compiler_params=_cparams(2),
        name="moba_attention",
    )(t5_flat, q_t, k, v_t, kmean)


def _split3(x):
    def trunc(v):
        bits = lax.bitcast_convert_type(v, I32)
        return lax.bitcast_convert_type(bits & jnp.int32(-65536), F32)
    hi = trunc(x)
    r1 = x - hi
    mid = trunc(r1)
    lo = trunc(r1 - mid)
    return hi, mid, lo


def _forget_cumsum_kernel(f_ref, hi_ref, mid_ref, lo_ref):
    x = f_ref[...]
    rows = x.shape[0]
    ls = jnp.minimum(x, 0.0) - jnp.log(1.0 + jnp.exp(-jnp.abs(x)))
    upper = (lax.broadcasted_iota(I32, (128, 128), 0) <= lax.broadcasted_iota(I32, (128, 128), 1)).astype(F32)
    within = jnp.dot(ls, upper, preferred_element_type=F32, precision=lax.Precision.HIGHEST)
    strict = (lax.broadcasted_iota(I32, (rows, rows), 1) < lax.broadcasted_iota(I32, (rows, rows), 0)).astype(F32)
    before = jnp.dot(strict, within, preferred_element_type=F32, precision=lax.Precision.HIGHEST)
    cf = (within + before[:, 127:128]) * LOG2E
    hi, mid, lo = _split3(cf)
    hi_ref[...] = hi
    mid_ref[...] = mid
    lo_ref[...] = lo


def _forget_cumsum(f_logit):
    bsz, seq, nh = f_logit.shape
    rows = seq // 128
    f = f_logit.transpose(0, 2, 1).reshape(bsz, nh, rows, 128)
    spec = pl.BlockSpec((None, None, rows, 128), lambda b, h: (b, h, 0, 0))
    parts = pl.pallas_call(
        _forget_cumsum_kernel,
        grid=(bsz, nh),
        in_specs=[spec],
        out_specs=[spec] * 3,
        out_shape=[jax.ShapeDtypeStruct((bsz, nh, rows, 128), F32)] * 3,
        compiler_params=_cparams(2),
        name="forget_cumsum",
    )(f)
    return [p.reshape(bsz, nh, seq) for p in parts]


def _forget_kernel(q_ref, k_ref, v_ref, gate_ref, o_ref, m_ref, l_ref, acc_ref, *, tile, heads):
    i = pl.program_id(2)
    _init_stats(m_ref, l_ref, acc_ref)

    def step(j, mask):
        k0 = pl.multiple_of(j * tile, tile)
        logits = [jnp.dot(k_ref[h, pl.ds(k0, tile), :], q_ref[h], preferred_element_type=F32) for h in range(heads)]
        for h in range(heads):
            s = logits[h] if mask is None else jnp.where(mask, logits[h], NEG)
            _flash_update(s, v_ref[h, :, pl.ds(k0, tile)], m_ref, l_ref, acc_ref, h)

    step(i, _causal_mask(tile))

    def past(j, carry):
        step(j, None)
        return carry

    lax.fori_loop(0, i, past, 0)
    for h in range(heads):
        g = gate_ref[h]
        o_ref[h] = ((acc_ref[h] / l_ref[h]) * (1.0 / (1.0 + jnp.exp(-g)))).astype(o_ref.dtype)


def _forgetting_attention(q_aug_t, k_aug, v_t, gate_t):
    bsz, nh, kdim, seq = q_aug_t.shape
    dh = v_t.shape[2]
    tile = min(ATTN_TILE, seq)
    hps = FORGET_HEADS_PER_STEP
    assert seq % tile == 0 and nh % hps == 0
    return pl.pallas_call(
        functools.partial(_forget_kernel, tile=tile, heads=hps),
        grid=(bsz, nh // hps, seq // tile),
        in_specs=[pl.BlockSpec((None, hps, kdim, tile), lambda b, h, i: (b, h, 0, i)),
                  _resident((None, hps, seq, kdim), lambda b, h, i: (b, h, 0, 0)),
                  _resident((None, hps, dh, seq), lambda b, h, i: (b, h, 0, 0)),
                  pl.BlockSpec((None, hps, dh, tile), lambda b, h, i: (b, h, 0, i))],
        out_specs=pl.BlockSpec((None, hps, dh, tile), lambda b, h, i: (b, h, 0, i)),
        out_shape=jax.ShapeDtypeStruct((bsz, nh, dh, seq), BF16),
        scratch_shapes=[pltpu.VMEM((hps, 1, tile), F32), pltpu.VMEM((hps, 1, tile), F32),
                        pltpu.VMEM((hps, dh, tile), F32)],
        compiler_params=_cparams(3),
        name="forget_attention",
    )(q_aug_t, k_aug, v_t, gate_t)


def _key_to_float(u):
    ks = u ^ jnp.int32(-2147483648)
    bits = jnp.where(ks < 0, ks ^ jnp.int32(2147483647), ks)
    return lax.bitcast_convert_type(bits, F32)


def _dsa_kernel(t5_ref, qi_ref, w_ref, ki_ref, q_ref, k_ref, v_ref, o_ref,
                sc_ref, bias_ref, tri_ref, m_ref, l_ref, acc_ref, *, tile, head0, topk):
    b = pl.program_id(0)
    i = pl.program_id(1)

    @pl.when((b == 0) & (i == 0))
    def _():
        for h in range(N_HEADS):
            _fill_t5_tile(bias_ref, 2 * h, t5_ref, head0 + h, 0, tile)
            _fill_t5_tile(bias_ref, 2 * h + 1, t5_ref, head0 + h, tile, tile)
        tri_ref[...] = jnp.where(lax.broadcasted_iota(I32, (tile, tile), 1) <= lax.broadcasted_iota(I32, (tile, tile), 0),
                                 1.0, 0.0).astype(BF16)

    _init_stats(m_ref, l_ref, acc_ref)
    causal = _causal_mask(tile)

    def index_scores(j):
        k0 = pl.multiple_of(j * tile, tile)
        kt = ki_ref[pl.ds(k0, tile), :]
        raws = [jnp.dot(kt, qi_ref[h * IDX_DIM:(h + 1) * IDX_DIM, :], preferred_element_type=F32)
                for h in range(IDX_HEADS)]
        sc = jnp.zeros((tile, tile), F32)
        for h in range(IDX_HEADS):
            sc = sc + jnp.maximum(raws[h], 0.0) * w_ref[h:h + 1, :]
        return k0, sc

    def score_body(j, carry):
        k0, sc = index_scores(j)
        sc_ref[pl.ds(k0, tile), :] = sc
        return carry

    lax.fori_loop(0, i, score_body, 0)
    k0, sc = index_scores(i)
    sc_ref[pl.ds(k0, tile), :] = jnp.where(causal, sc, NEG)

    def count(pred):
        def body(j, cnt):
            k0 = pl.multiple_of(j * tile, tile)
            return cnt + jnp.sum(jnp.where(pred(sc_ref[pl.ds(k0, tile), :]), 1.0, 0.0), axis=0, keepdims=True)
        return lax.fori_loop(0, i + 1, body, jnp.zeros((1, tile), F32))

    def bit_body(it, ans):
        cand = ans | jnp.left_shift(jnp.int32(1), 31 - it)
        thr = _key_to_float(cand)
        cnt = count(lambda s: s >= thr)
        return jnp.where(cnt >= topk, cand, ans)

    ans = lax.fori_loop(0, 32, bit_body, jnp.zeros((1, tile), I32))
    tau = _key_to_float(ans)
    need = topk - count(lambda s: s > tau)

    qs = [q_ref[h] for h in range(N_HEADS)]

    def attend(j, run, kind):
        k0 = pl.multiple_of(j * tile, tile)
        sc = sc_ref[pl.ds(k0, tile), :]
        eq = sc == tau
        eqf = jnp.where(eq, 1.0, 0.0)
        rank = run + jnp.dot(tri_ref[...], eqf.astype(BF16), preferred_element_type=F32)
        wgt = jnp.where(eq, jnp.where(rank <= need, 1.0, 0.0), jnp.where(sc > tau, 1.0, 0.0))
        if kind == "diag":
            wgt = jnp.where(causal, wgt, 0.0)
        keep = wgt > 0.5
        kt = k_ref[pl.ds(k0, tile), :]
        logits = [jnp.dot(kt, qs[h], preferred_element_type=F32) for h in range(N_HEADS)]
        for h in range(N_HEADS):
            if kind == "far":
                bias = LOG2E * t5_ref[(T5_BUCKETS - 1) * N_BIAS_HEADS + head0 + h]
            else:
                bias = bias_ref[2 * h + (1 if kind == "near" else 0)]
            s = jnp.where(keep, logits[h] + bias, NEG)
            _flash_update(s, v_ref[h * HEAD_DIM:(h + 1) * HEAD_DIM, pl.ds(k0, tile)], m_ref, l_ref, acc_ref, h, weight=wgt)
        return run + jnp.sum(eqf, axis=0, keepdims=True)

    run = lax.fori_loop(0, jnp.maximum(i - 1, 0), lambda j, r: attend(j, r, "far"), jnp.zeros((1, tile), F32))
    run = lax.cond(i >= 1, lambda r: attend(i - 1, r, "near"), lambda r: r, run)
    attend(i, run, "diag")
    for h in range(N_HEADS):
        o_ref[h * HEAD_DIM:(h + 1) * HEAD_DIM, :] = (acc_ref[h] / l_ref[h]).astype(o_ref.dtype)


def _dsa_attention(qi_t, w_t, k_idx, q_t, k, v_t, t5_flat):
    bsz, _, width, seq = q_t.shape
    tile = min(SPARSE_TILE, seq)
    topk = min(DSA_TOPK_MAX, seq // 4)
    assert seq % tile == 0 and tile >= topk and tile >= T5_FAR_DIST
    kern = functools.partial(_dsa_kernel, tile=tile, head0=2 * N_HEADS, topk=topk)
    return pl.pallas_call(
        kern,
        grid=(bsz, seq // tile),
        in_specs=[_smem(),
                  pl.BlockSpec((None, qi_t.shape[1], tile), lambda b, i: (b, 0, i)),
                  pl.BlockSpec((None, 8, tile), lambda b, i: (b, 0, i)),
                  _resident((None, seq, k_idx.shape[2]), lambda b, i: (b, 0, 0)),
                  pl.BlockSpec((None, N_HEADS, width, tile), lambda b, i: (b, 0, 0, i)),
                  _resident((None, seq, width), lambda b, i: (b, 0, 0)),
                  _resident((None, width, seq), lambda b, i: (b, 0, 0))],
        out_specs=pl.BlockSpec((None, width, tile), lambda b, i: (b, 0, i)),
        out_shape=jax.ShapeDtypeStruct((bsz, width, seq), BF16),
        scratch_shapes=[pltpu.VMEM((seq, tile), F32),
                        pltpu.VMEM((2 * N_HEADS, tile, tile), F32),
                        pltpu.VMEM((tile, tile), BF16),
                        pltpu.VMEM((N_HEADS, 1, tile), F32), pltpu.VMEM((N_HEADS, 1, tile), F32),
                        pltpu.VMEM((N_HEADS, HEAD_DIM, tile), F32)],
        compiler_params=_cparams(2),
        name="dsa_attention",
    )(t5_flat, qi_t, w_t, k_idx, q_t, k, v_t)


def _outproj_kernel(x_ref, o_ref, w_ref, g_ref, y_ref):
    y_ref[...] = x_ref[...] + g_ref[...] * jnp.dot(o_ref[...], w_ref[...], preferred_element_type=F32)


def _output_projection(x, o, w_out, gate):
    bsz, seq, d = x.shape
    tm = ROW_TILE
    row = pl.BlockSpec((None, tm, d), lambda b, i: (b, i, 0))
    return pl.pallas_call(
        _outproj_kernel,
        grid=(bsz, seq // tm),
        in_specs=[row, pl.BlockSpec((None, tm, o.shape[2]), lambda b, i: (b, i, 0)),
                  _resident(w_out.shape, lambda b, i: (0, 0)),
                  pl.BlockSpec((None, 1, d), lambda b, i: (b, 0, 0))],
        out_specs=row,
        out_shape=jax.ShapeDtypeStruct(x.shape, F32),
        compiler_params=_cparams(2),
        name="output_projection",
    )(x, o, w_out.astype(BF16), gate)


HALO = 8


def _ffn_kernel(x_ref, xp_ref, gain_ref, sc_ref, sh_ref, g_ref, wg_ref, wv_ref, cw_ref, cb_ref, wd_ref, fg_ref,
                y_ref, h_ref, acc_ref, *, tm, n_chunks, final_norm):
    i = pl.program_id(1)
    c = pl.program_id(2)

    @pl.when(c == 0)
    def _():
        def modulated(x):
            return ((_rms(x) * gain_ref[...]) * (1.0 + sc_ref[...]) + sh_ref[...]).astype(BF16)
        h_ref[HALO:, :] = modulated(x_ref[...])
        halo = modulated(xp_ref[...])
        h_ref[:HALO, :] = jnp.where(i > 0, halo, jnp.zeros_like(halo))
        acc_ref[...] = jnp.zeros(acc_ref.shape, F32)

    h = h_ref[...]

    def conv(w_ref, half):
        u = jnp.dot(h, w_ref[...], preferred_element_type=F32)
        cw = cw_ref[half]
        return (cw[0:1] * u[HALO - 2:HALO - 2 + tm] + cw[1:2] * u[HALO - 1:HALO - 1 + tm]
                + cw[2:3] * u[HALO:HALO + tm]) + cb_ref[half]

    gate = conv(wg_ref, 0)
    val = conv(wv_ref, 1)
    a = (gate * (1.0 / (1.0 + jnp.exp(-gate)))) * val
    acc_ref[...] += jnp.dot(a.astype(BF16), wd_ref[...], preferred_element_type=F32)

    @pl.when(c == n_chunks - 1)
    def _():
        y = x_ref[...] + g_ref[...] * acc_ref[...]
        if final_norm:
            y = _rms(y) * fg_ref[...]
        y_ref[...] = y


def _ffn(x, gain, sc, sh, gate, w_up, conv_w, conv_b, w_down, final_gain, final_norm):
    bsz, seq, d = x.shape
    tm = min(FFN_ROW_TILE, seq)
    fc = FFN_CHUNK
    n_chunks = D_FF // fc
    w_up_b = w_up.astype(BF16)
    cw = conv_w.reshape(3, 2, D_FF).transpose(1, 0, 2)
    cb = conv_b.reshape(2, 1, D_FF)
    vec = pl.BlockSpec((None, 1, d), lambda b, i, c: (b, 0, 0))
    one = pl.BlockSpec((1, d), lambda b, i, c: (0, 0))
    row = pl.BlockSpec((None, tm, d), lambda b, i, c: (b, i, 0))
    return pl.pallas_call(
        functools.partial(_ffn_kernel, tm=tm, n_chunks=n_chunks, final_norm=final_norm),
        grid=(bsz, seq // tm, n_chunks),
        in_specs=[row,
                  pl.BlockSpec((None, HALO, d), lambda b, i, c: (b, jnp.maximum(i * (tm // HALO) - 1, 0), 0)),
                  one, vec, vec, vec,
                  pl.BlockSpec((d, fc), lambda b, i, c: (0, c)),
                  pl.BlockSpec((d, fc), lambda b, i, c: (0, n_chunks + c)),
                  pl.BlockSpec((2, 3, fc), lambda b, i, c: (0, 0, c)),
                  pl.BlockSpec((2, 1, fc), lambda b, i, c: (0, 0, c)),
                  pl.BlockSpec((fc, d), lambda b, i, c: (c, 0)),
                  one],
        out_specs=row,
        out_shape=jax.ShapeDtypeStruct(x.shape, F32),
        scratch_shapes=[pltpu.VMEM((HALO + tm, d), BF16), pltpu.VMEM((tm, d), F32)],
        compiler_params=_cparams(3),
        name="conv_glu_ffn",
    )(x, x, gain.reshape(1, d), sc, sh, gate, w_up_b, w_up_b, cw, cb, w_down.astype(BF16),
      final_gain.reshape(1, d))


def kernel(x, c, w_ada, b_ada, norm1_gain, w_in, f_bias, diff_lambda, diff_subln_gain, w_dq_up, w_didx_q,
           w_dkv_up, w_out, t5_table, norm2_gain, w_ffn_up, ffn_conv_w, ffn_conv_b, w_ffn_down, final_gain):
    bsz, seq, d = x.shape
    depth = w_ada.shape[0]
    t5_flat = t5_table.reshape(-1)
    mod = _modulation(c, w_ada, b_ada)
    w = GROUP_W

    for l in range(depth):
        sh1, sc1, g1, sh2, sc2, g2 = [m[:, None, :] for m in jnp.split(mod[l], 6, axis=-1)]
        pa, pb, pc, gc, dq, dkv, misc = _input_projection(x, norm1_gain[l], sc1, sh1, w_in[l],
                                                          w_dq_up[l], w_didx_q[l], w_dkv_up[l])
        k_idx = misc[..., :IDX_DIM].astype(BF16)
        fc = misc[..., IDX_DIM:IDX_DIM + N_HEADS]
        w_idx = misc[..., IDX_DIM + N_HEADS:IDX_DIM + N_HEADS + IDX_HEADS]

        lambda_init = 0.8 - 0.6 * math.exp(-0.3 * l)
        lq1, lk1, lq2, lk2 = diff_lambda[l]
        lam = jnp.exp(jnp.sum(lq1 * lk1)) - jnp.exp(jnp.sum(lq2 * lk2)) + lambda_init
        qa_t = _heads_t(pa[..., :w]).reshape(bsz, N_HEADS, 2, DIFF_HALF, seq)
        zeros = jnp.zeros_like(qa_t)
        qa_t = jnp.stack([jnp.concatenate([qa_t[:, :, 0], zeros[:, :, 0]], axis=2),
                          jnp.concatenate([zeros[:, :, 1], qa_t[:, :, 1]], axis=2)], axis=2)
        o_a = _diff_attention(qa_t, _heads(pa[..., w:2 * w]), _heads_t(pa[..., 2 * w:]), lam,
                              diff_subln_gain[l], t5_flat, 1.0 - lambda_init)

        kb = _heads(pb[..., w:2 * w])
        o_b = _moba_attention(_heads_t(pb[..., :w]), kb, _heads_t(pb[..., 2 * w:]),
                              _moba_kmean(kb).astype(BF16), t5_flat)

        hi, mid, lo = _forget_cumsum(fc + f_bias[l])
        ones = jnp.ones_like(hi)
        cparts = jnp.stack([hi, mid, lo], axis=2)
        onep = jnp.stack([ones] * 3, axis=2)
        padw = 2 * HEAD_DIM - HEAD_DIM - 6
        qc_aug = jnp.concatenate([_heads_t(pc[..., :w]), onep.astype(BF16), cparts.astype(BF16),
                                  jnp.zeros((bsz, N_HEADS, padw, seq), BF16)], axis=2)
        kc_aug = jnp.concatenate([_heads(pc[..., w:2 * w]), (-cparts).transpose(0, 1, 3, 2).astype(BF16),
                                  onep.transpose(0, 1, 3, 2).astype(BF16),
                                  jnp.zeros((bsz, N_HEADS, seq, padw), BF16)], axis=3)
        o_c = _forgetting_attention(qc_aug, kc_aug, _heads_t(pc[..., 2 * w:]), _heads_t(gc))

        qd_t = dq[..., :w].transpose(0, 2, 1)
        head_of_row = jnp.arange(w)[None, :, None] // HEAD_DIM
        qd_t = jnp.stack([jnp.where(head_of_row == h, qd_t, jnp.zeros((), BF16)) for h in range(N_HEADS)],
                         axis=1)
        qi_t = dq[..., w:].transpose(0, 2, 1)
        w_t = (w_idx * (IDX_HEADS ** -0.5 * IDX_DIM ** -0.5)).transpose(0, 2, 1)
        w_t = jnp.concatenate([w_t, jnp.zeros((bsz, 8 - IDX_HEADS, seq), F32)], axis=1)
        o_d = _dsa_attention(qi_t, w_t, k_idx, qd_t, dkv[..., :w], dkv[..., w:].transpose(0, 2, 1), t5_flat)

        o = jnp.concatenate([_unheads_t(o_a), _unheads_t(o_b), _unheads_t(o_c), o_d.transpose(0, 2, 1)], axis=-1)
        x = _output_projection(x, o, w_out[l], g1)
        x = _ffn(x, norm2_gain[l], sc2, sh2, g2, w_ffn_up[l], ffn_conv_w[l], ffn_conv_b[l], w_ffn_down[l],
                 final_gain, final_norm=(l == depth - 1))
    return x
```

```python
import functools
import math

import numpy as np
import jax
import jax.numpy as jnp
from jax import lax
from jax.experimental import pallas as pl
from jax.experimental.pallas import tpu as pltpu

F32 = jnp.float32
BF16 = jnp.bfloat16
I32 = jnp.int32

HEAD_DIM = 64
N_HEADS = 4
GROUP_W = N_HEADS * HEAD_DIM
DIFF_HALF = HEAD_DIM // 2
MOBA_BLOCK = 256
MOBA_TOPK = 3
DSA_RANK = 128
IDX_HEADS = 4
IDX_DIM = 64
DSA_TOPK_MAX = 256
T5_BUCKETS = 32
T5_MAX_DIST = 128
N_BIAS_HEADS = 12
D_FF = 2816
EPS = 1e-6
NEG = -1e30
LOG2E = math.log2(math.e)
V_ROWS = HEAD_DIM + 16
BOUND_SLACK = 1.0 + 2.0 ** -6
MIN_DENOMINATOR = 2.0 ** -60

V7X_VMEM_LIMIT_BYTES = 56 * 1024 * 1024

ATTN_TILE = 512
SPARSE_TILE = 256
ROW_TILE = 512
FFN_ROW_TILE = 1024
FFN_CHUNK = 256
FORGET_HEADS_PER_STEP = 4
DIFF_HEADS_PER_STEP = 2
FORGET_BLOCKS_PER_STEP = 2


def _t5_thresholds():
    d = np.arange(0, 4 * T5_MAX_DIST)
    max_exact = T5_BUCKETS // 2
    ratio = np.maximum(d, 1).astype(np.float32) / max_exact
    large = max_exact + (np.log(ratio) / math.log(T5_MAX_DIST / max_exact) * (T5_BUCKETS - max_exact)).astype(np.int32)
    bucket = np.where(d < max_exact, d, np.minimum(large, T5_BUCKETS - 1))
    return [int(np.argmax(bucket >= b)) for b in range(T5_BUCKETS)]


T5_THRESH = _t5_thresholds()
T5_FAR_DIST = T5_THRESH[-1]


def _cparams(n_axes):
    return pltpu.CompilerParams(dimension_semantics=("arbitrary",) * n_axes,
                                vmem_limit_bytes=V7X_VMEM_LIMIT_BYTES)


def _resident(block_shape, index_map):
    return pl.BlockSpec(block_shape, index_map, pipeline_mode=pl.Buffered(1))


def _smem():
    return pl.BlockSpec(memory_space=pltpu.SMEM)


def _fill_t5_tile(bias_ref, slot, t5_ref, head, delta, tile):
    rows = 64

    def body(r, carry):
        r0 = pl.multiple_of(r * rows, rows)
        kr = lax.broadcasted_iota(I32, (rows, tile), 0) + r0
        qc = lax.broadcasted_iota(I32, (rows, tile), 1)
        d = delta + qc - kr
        val = jnp.full((rows, tile), LOG2E * t5_ref[head], F32)
        for b in range(1, T5_BUCKETS):
            val = jnp.where(d >= T5_THRESH[b], LOG2E * t5_ref[b * N_BIAS_HEADS + head], val)
        bias_ref[slot, pl.ds(r0, rows), :] = val
        return carry

    lax.fori_loop(0, tile // rows, body, 0)


def _causal_mask(tile):
    kr = lax.broadcasted_iota(I32, (tile, tile), 0)
    qc = lax.broadcasted_iota(I32, (tile, tile), 1)
    return kr <= qc


def _softmax_accumulate(s, v_aug, acc_ref, idx, *, shift=None, m_ref=None, keep=None):
    if m_ref is None:
        p = jnp.exp2(s - shift)
        acc_ref[idx] += jnp.dot(v_aug, p.astype(BF16), preferred_element_type=F32)
        return
    m_old = m_ref[idx]
    m_new = jnp.maximum(m_old, jnp.max(s, axis=0, keepdims=True))
    p = jnp.exp2(s - m_new)
    if keep is not None:
        p = jnp.where(keep, p, 0.0)
    acc_ref[idx] = jnp.exp2(m_old - m_new) * acc_ref[idx] + jnp.dot(v_aug, p.astype(BF16), preferred_element_type=F32)
    m_ref[idx] = m_new


def _logit_bound(q, k_norm_max, extra):
    qf = q.astype(F32)
    q_norm = jnp.sqrt(jnp.sum(qf * qf, axis=0, keepdims=True))
    return q_norm * (k_norm_max * BOUND_SLACK) + (extra + 1.0)


def _denominators_ok(acc_ref, n_chains):
    low = acc_ref[0, HEAD_DIM:HEAD_DIM + 1, :]
    for idx in range(1, n_chains):
        low = jnp.minimum(low, acc_ref[idx, HEAD_DIM:HEAD_DIM + 1, :])
    return jnp.min(low) > MIN_DENOMINATOR


def _two_pass_attention(run, acc_ref, m_ref, n_chains):
    acc_ref[...] = jnp.zeros(acc_ref.shape, F32)
    run(False)

    @pl.when(jnp.logical_not(_denominators_ok(acc_ref, n_chains)))
    def _():
        acc_ref[...] = jnp.zeros(acc_ref.shape, F32)
        m_ref[...] = jnp.full(m_ref.shape, NEG, F32)
        run(True)


def _normalized(acc_ref, idx):
    return acc_ref[idx, :HEAD_DIM, :] / acc_ref[idx, HEAD_DIM:HEAD_DIM + 1, :]


def _mod_kernel(c_ref, w_ref, b_ref, o_ref):
    c = c_ref[...]
    cond = c * (1.0 / (1.0 + jnp.exp(-c)))
    o_ref[...] = jnp.dot(cond.astype(BF16), w_ref[...].astype(BF16), preferred_element_type=F32) + b_ref[...]


def _modulation(c, w_ada, b_ada):
    depth, d, n = w_ada.shape
    bsz = c.shape[0]
    rows = 8
    c_pad = jnp.zeros((rows, d), F32).at[:bsz].set(c)
    tn = 1024
    out = pl.pallas_call(
        _mod_kernel,
        grid=(depth, n // tn),
        in_specs=[pl.BlockSpec((rows, d), lambda l, j: (0, 0)),
                  pl.BlockSpec((None, d, tn), lambda l, j: (l, 0, j)),
                  pl.BlockSpec((None, 1, tn), lambda l, j: (l, 0, j))],
        out_specs=pl.BlockSpec((None, rows, tn), lambda l, j: (l, 0, j)),
        out_shape=jax.ShapeDtypeStruct((depth, rows, n), F32),
        compiler_params=_cparams(2),
        name="adaln_modulation",
    )(c_pad, w_ada, b_ada.reshape(depth, 1, n))
    return out[:, :bsz]


def _rms(x):
    return x * lax.rsqrt(jnp.mean(x * x, axis=-1, keepdims=True) + EPS)


def _inproj_kernel(x_ref, gain_ref, sc_ref, sh_ref, w_ref, wq_ref, wkv_ref,
                   pa_ref, pb_ref, pc_ref, gc_ref, dq_ref, dkv_ref, misc_ref):
    x = x_ref[...]
    h = (_rms(x) * gain_ref[...]) * (1.0 + sc_ref[...]) + sh_ref[...]
    hb = h.astype(BF16)
    w3 = 3 * GROUP_W

    def proj(c0, c1):
        return jnp.dot(hb, w_ref[:, c0:c1], preferred_element_type=F32)

    pa_ref[...] = proj(0, w3).astype(BF16)
    pb_ref[...] = proj(w3, 2 * w3).astype(BF16)
    pc_ref[...] = proj(2 * w3, 3 * w3).astype(BF16)
    c0 = 3 * w3
    gc_ref[...] = proj(c0, c0 + GROUP_W)
    c0 += GROUP_W
    q_lat = _rms(proj(c0, c0 + DSA_RANK))
    kv_lat = _rms(proj(c0 + DSA_RANK, c0 + 2 * DSA_RANK))
    dq_ref[...] = jnp.dot(q_lat.astype(BF16), wq_ref[...], preferred_element_type=F32).astype(BF16)
    dkv_ref[...] = jnp.dot(kv_lat.astype(BF16), wkv_ref[...], preferred_element_type=F32).astype(BF16)
    c0 += 2 * DSA_RANK
    misc_ref[...] = proj(c0, c0 + 128)


def _input_projection(x, gain, sc, sh, w_in, w_dq_up, w_didx_q, w_dkv_up):
    bsz, seq, d = x.shape
    w3 = 3 * GROUP_W
    o_c = 2 * w3
    o_fc = o_c + w3
    o_gc = o_fc + N_HEADS
    o_ql = o_gc + GROUP_W
    o_kv = o_ql + DSA_RANK
    o_ki = o_kv + DSA_RANK
    o_wi = o_ki + IDX_DIM
    pad = jnp.zeros((d, 128 - IDX_DIM - N_HEADS - IDX_HEADS), w_in.dtype)
    col = jnp.arange(o_fc)
    is_q = (col % w3) < GROUP_W
    q_scale = jnp.where(is_q, jnp.where(col < w3, DIFF_HALF ** -0.5, HEAD_DIM ** -0.5) * LOG2E, 1.0).astype(F32)
    w = jnp.concatenate([w_in[:, :o_fc] * q_scale[None, :], w_in[:, o_gc:o_ql], w_in[:, o_ql:o_ki],
                         w_in[:, o_ki:o_wi], w_in[:, o_fc:o_gc], w_in[:, o_wi:], pad], axis=1).astype(BF16)
    n_cols = w.shape[1]
    wq = jnp.concatenate([w_dq_up * (HEAD_DIM ** -0.5 * LOG2E), w_didx_q], axis=1).astype(BF16)
    wkv = w_dkv_up.astype(BF16)
    tm = ROW_TILE
    row = lambda width: pl.BlockSpec((None, tm, width), lambda b, i: (b, i, 0))
    vec = pl.BlockSpec((None, 1, d), lambda b, i: (b, 0, 0))
    outs = pl.pallas_call(
        _inproj_kernel,
        grid=(bsz, seq // tm),
        in_specs=[row(d), pl.BlockSpec((1, d), lambda b, i: (0, 0)), vec, vec,
                  _resident((d, n_cols), lambda b, i: (0, 0)),
                  _resident(wq.shape, lambda b, i: (0, 0)),
                  _resident(wkv.shape, lambda b, i: (0, 0))],
        out_specs=[row(w3), row(w3), row(w3), row(GROUP_W), row(512), row(512), row(128)],
        out_shape=[jax.ShapeDtypeStruct((bsz, seq, w3), BF16)] * 3
                  + [jax.ShapeDtypeStruct((bsz, seq, GROUP_W), F32)]
                  + [jax.ShapeDtypeStruct((bsz, seq, 512), BF16)] * 2
                  + [jax.ShapeDtypeStruct((bsz, seq, 128), F32)],
        compiler_params=_cparams(2),
        name="input_projection",
    )(x, gain.reshape(1, d), sc, sh, w, wq, wkv)
    return outs


def _heads_t(t):
    bsz, seq, w = t.shape
    return t.reshape(bsz, seq, N_HEADS, w // N_HEADS).transpose(0, 2, 3, 1)


def _heads(t):
    bsz, seq, w = t.shape
    return t.reshape(bsz, seq, N_HEADS, w // N_HEADS).transpose(0, 2, 1, 3)


def _values_aug(v):
    v_t = _heads_t(v)
    bsz, nh, dh, seq = v_t.shape
    return jnp.concatenate([v_t, jnp.ones((bsz, nh, 1, seq), v.dtype),
                            jnp.zeros((bsz, nh, V_ROWS - dh - 1, seq), v.dtype)], axis=2)


def _key_norm_max(k):
    bsz, seq, w = k.shape
    kf = k.astype(F32).reshape(bsz, seq, N_HEADS, w // N_HEADS)
    return jnp.max(jnp.sqrt(jnp.sum(kf * kf, axis=-1)), axis=1).reshape(-1)


def _unheads_t(o_t):
    bsz, h, dh, seq = o_t.shape
    return o_t.transpose(0, 3, 1, 2).reshape(bsz, seq, h * dh)


def _diff_kernel(t5_ref, bmax_ref, kmax_ref, lam_ref, q_ref, k_ref, v_ref, g_ref, o_ref, bias_ref, m_ref, acc_ref,
                 *, tile, head0, heads, group_heads, out_scale):
    b = pl.program_id(0)
    hg = pl.program_id(1)
    i = pl.program_id(2)

    @pl.when(i == 0)
    def _():
        for h in range(heads):
            _fill_t5_tile(bias_ref, 2 * h, t5_ref, head0 + hg * heads + h, 0, tile)
            _fill_t5_tile(bias_ref, 2 * h + 1, t5_ref, head0 + hg * heads + h, tile, tile)

    far_bias = [LOG2E * t5_ref[(T5_BUCKETS - 1) * N_BIAS_HEADS + head0 + hg * heads + h] for h in range(heads)]
    bounds = [_logit_bound(q_ref[h, c], kmax_ref[b * group_heads + hg * heads + h], bmax_ref[head0 + hg * heads + h])
              for h in range(heads) for c in range(2)]

    def run(online):
        def step(j, kind):
            k0 = pl.multiple_of(j * tile, tile)
            logits = [jnp.dot(k_ref[h, pl.ds(k0, tile), :], q_ref[h, c], preferred_element_type=F32)
                      for h in range(heads) for c in range(2)]
            for h in range(heads):
                vt = v_ref[h, :, pl.ds(k0, tile)]
                for c in range(2):
                    idx = 2 * h + c
                    s, shift = logits[idx], bounds[idx]
                    if kind != "far":
                        s = s + bias_ref[2 * h + (1 if kind == "near" else 0)]
                    elif online:
                        s = s + far_bias[h]
                    else:
                        shift = shift - far_bias[h]
                    if kind == "diag":
                        s = jnp.where(_causal_mask(tile), s, NEG)
                    _softmax_accumulate(s, vt, acc_ref, idx, shift=shift, m_ref=m_ref if online else None)

        step(i, "diag")

        @pl.when(i >= 1)
        def _():
            step(i - 1, "near")

        def far(j, carry):
            step(j, "far")
            return carry

        lax.fori_loop(0, jnp.maximum(i - 1, 0), far, 0)

    _two_pass_attention(run, acc_ref, m_ref, 2 * heads)

    for h in range(heads):
        o = _normalized(acc_ref, 2 * h) - lam_ref[0] * _normalized(acc_ref, 2 * h + 1)
        y = o * lax.rsqrt(jnp.mean(o * o, axis=0, keepdims=True) + EPS)
        o_ref[h] = ((y * g_ref[...]) * out_scale).astype(o_ref.dtype)


def _diff_attention(q_t, k, v_aug, lam, subln_gain, t5_flat, bias_max, k_norm_max, out_scale):
    bsz, nh, _, dh, seq = q_t.shape
    tile = min(ATTN_TILE, seq)
    hps = DIFF_HEADS_PER_STEP
    assert tile >= T5_FAR_DIST and seq % tile == 0 and nh % hps == 0
    kern = functools.partial(_diff_kernel, tile=tile, head0=0, heads=hps, group_heads=nh, out_scale=out_scale)
    return pl.pallas_call(
        kern,
        grid=(bsz, nh // hps, seq // tile),
        in_specs=[_smem(), _smem(), _smem(), _smem(),
                  pl.BlockSpec((None, hps, 2, dh, tile), lambda b, h, i: (b, h, 0, 0, i)),
                  _resident((None, hps, seq, dh), lambda b, h, i: (b, h, 0, 0)),
                  _resident((None, hps, V_ROWS, seq), lambda b, h, i: (b, h, 0, 0)),
                  pl.BlockSpec((dh, 1), lambda b, h, i: (0, 0))],
        out_specs=pl.BlockSpec((None, hps, dh, tile), lambda b, h, i: (b, h, 0, i)),
        out_shape=jax.ShapeDtypeStruct((bsz, nh, dh, seq), BF16),
        scratch_shapes=[pltpu.VMEM((2 * hps, tile, tile), F32),
                        pltpu.VMEM((2 * hps, 1, tile), F32),
                        pltpu.VMEM((2 * hps, V_ROWS, tile), F32)],
        compiler_params=_cparams(3),
        name="diff_attention",
    )(t5_flat, bias_max, k_norm_max, lam.reshape(1), q_t, k, v_aug, subln_gain.reshape(dh, 1))


def _kmean_kernel(k_ref, o_ref, *, blocks):
    k = k_ref[...].astype(F32)
    o_ref[...] = jnp.mean(k.reshape(blocks, MOBA_BLOCK, k.shape[-1]), axis=1)


def _moba_kmean(k):
    bsz, nh, seq, dh = k.shape
    nb = seq // MOBA_BLOCK
    blocks = min(8, nb)
    return pl.pallas_call(
        functools.partial(_kmean_kernel, blocks=blocks),
        grid=(bsz, nh, nb // blocks),
        in_specs=[pl.BlockSpec((None, None, blocks * MOBA_BLOCK, dh), lambda b, h, i: (b, h, i, 0))],
        out_specs=pl.BlockSpec((None, None, blocks, dh), lambda b, h, i: (b, h, i, 0)),
        out_shape=jax.ShapeDtypeStruct((bsz, nh, nb, dh), F32),
        compiler_params=_cparams(3),
    )(k)


def _moba_kernel(t5_ref, bmax_ref, kmax_ref, q_ref, k_ref, v_ref, km_ref, o_ref, bias_ref, sel_ref, m_ref, acc_ref,
                 *, tile, head0, heads, n_blocks):
    b = pl.program_id(0)
    i = pl.program_id(1)

    @pl.when(i == 0)
    def _():
        for h in range(heads):
            _fill_t5_tile(bias_ref, 2 * h, t5_ref, head0 + h, 0, tile)
            _fill_t5_tile(bias_ref, 2 * h + 1, t5_ref, head0 + h, tile, tile)

    far_bias = [LOG2E * t5_ref[(T5_BUCKETS - 1) * N_BIAS_HEADS + head0 + h] for h in range(heads)]
    bounds = [_logit_bound(q_ref[h], kmax_ref[b * heads + h], bmax_ref[head0 + h]) for h in range(heads)]

    nidx = lax.broadcasted_iota(I32, (n_blocks, tile), 0)
    for h in range(heads):
        gate = jnp.dot(km_ref[h], q_ref[h], preferred_element_type=F32)
        g = jnp.where(nidx < i, gate, NEG)
        sel = jnp.zeros(gate.shape, F32)
        for _ in range(min(MOBA_TOPK, n_blocks)):
            mx = jnp.max(g, axis=0, keepdims=True)
            first = jnp.min(jnp.where(g == mx, nidx, n_blocks), axis=0, keepdims=True)
            pick = nidx == first
            sel = jnp.where(pick, 1.0, sel)
            g = jnp.where(pick, -jnp.inf, g)
        sel_ref[h] = jnp.where(nidx < i, sel, 0.0)

    def run(online):
        def step(j, kind):
            k0 = pl.multiple_of(j * tile, tile)
            logits = [jnp.dot(k_ref[h, pl.ds(k0, tile), :], q_ref[h], preferred_element_type=F32)
                      for h in range(heads)]
            for h in range(heads):
                s, shift = logits[h], bounds[h]
                if kind != "far":
                    s = s + bias_ref[2 * h + (1 if kind == "near" else 0)]
                elif online:
                    s = s + far_bias[h]
                else:
                    shift = shift - far_bias[h]
                mask = _causal_mask(tile) if kind == "diag" else sel_ref[h, pl.ds(j, 1), :] > 0.5
                s = jnp.where(mask, s, NEG)
                _softmax_accumulate(s, v_ref[h, :, pl.ds(k0, tile)], acc_ref, h, shift=shift,
                                    m_ref=m_ref if online else None)

        step(i, "diag")

        @pl.when(i >= 1)
        def _():
            step(i - 1, "near")

        def far(j, carry):
            step(j, "far")
            return carry

        lax.fori_loop(0, jnp.maximum(i - 1, 0), far, 0)

    _two_pass_attention(run, acc_ref, m_ref, heads)
    for h in range(heads):
        o_ref[h] = _normalized(acc_ref, h).astype(o_ref.dtype)


def _moba_attention(q_t, k, v_aug, kmean, t5_flat, bias_max, k_norm_max):
    bsz, nh, dh, seq = q_t.shape
    tile = MOBA_BLOCK
    assert SPARSE_TILE == MOBA_BLOCK and seq % tile == 0 and tile >= T5_FAR_DIST
    nb = seq // tile
    kern = functools.partial(_moba_kernel, tile=tile, head0=N_HEADS, heads=nh, n_blocks=nb)
    return pl.pallas_call(
        kern,
        grid=(bsz, nb),
        in_specs=[_smem(), _smem(), _smem(),
                  pl.BlockSpec((None, nh, dh, tile), lambda b, i: (b, 0, 0, i)),
                  _resident((None, nh, seq, dh), lambda b, i: (b, 0, 0, 0)),
                  _resident((None, nh, V_ROWS, seq), lambda b, i: (b, 0, 0, 0)),
                  _resident((None, nh, nb, dh), lambda b, i: (b, 0, 0, 0))],
        out_specs=pl.BlockSpec((None, nh, dh, tile), lambda b, i: (b, 0, 0, i)),
        out_shape=jax.ShapeDtypeStruct((bsz, nh, dh, seq), BF16),
        scratch_shapes=[pltpu.VMEM((2 * nh, tile, tile), F32), pltpu.VMEM((nh, nb, tile), F32),
                        pltpu.VMEM((nh, 1, tile), F32),
                        pltpu.VMEM((nh, V_ROWS, tile), F32)],
        compiler_params=_cparams(2),
        name="moba_attention",
    )(t5_flat, bias_max, k_norm_max, q_t, k, v_aug, kmean)


def _split3(x):
    def trunc(v):
        bits = lax.bitcast_convert_type(v, I32)
        return lax.bitcast_convert_type(bits & jnp.int32(-65536), F32)
    hi = trunc(x)
    r1 = x - hi
    mid = trunc(r1)
    lo = trunc(r1 - mid)
    return hi, mid, lo


def _forget_cumsum_kernel(f_ref, hi_ref, mid_ref, lo_ref):
    x = f_ref[...]
    rows = x.shape[0]
    ls = jnp.minimum(x, 0.0) - jnp.log(1.0 + jnp.exp(-jnp.abs(x)))
    upper = (lax.broadcasted_iota(I32, (128, 128), 0) <= lax.broadcasted_iota(I32, (128, 128), 1)).astype(F32)
    within = jnp.dot(ls, upper, preferred_element_type=F32, precision=lax.Precision.HIGHEST)
    strict = (lax.broadcasted_iota(I32, (rows, rows), 1) < lax.broadcasted_iota(I32, (rows, rows), 0)).astype(F32)
    before = jnp.dot(strict, within, preferred_element_type=F32, precision=lax.Precision.HIGHEST)
    cf = (within + before[:, 127:128]) * LOG2E
    hi, mid, lo = _split3(cf)
    hi_ref[...] = hi
    mid_ref[...] = mid
    lo_ref[...] = lo


def _forget_cumsum(f_logit):
    bsz, seq, nh = f_logit.shape
    rows = seq // 128
    f = f_logit.transpose(0, 2, 1).reshape(bsz, nh, rows, 128)
    spec = pl.BlockSpec((None, None, rows, 128), lambda b, h: (b, h, 0, 0))
    parts = pl.pallas_call(
        _forget_cumsum_kernel,
        grid=(bsz, nh),
        in_specs=[spec],
        out_specs=[spec] * 3,
        out_shape=[jax.ShapeDtypeStruct((bsz, nh, rows, 128), F32)] * 3,
        compiler_params=_cparams(2),
        name="forget_cumsum",
    )(f)
    return [p.reshape(bsz, nh, seq) for p in parts]


def _forget_kernel(kmax_ref, q_ref, k_ref, v_ref, gate_ref, o_ref, m_ref, acc_ref, *, tile, heads, group_heads):
    b = pl.program_id(0)
    hg = pl.program_id(1)
    i = pl.program_id(2)
    bounds = [_logit_bound(q_ref[h, :HEAD_DIM, :], kmax_ref[b * group_heads + hg * heads + h], 0.0)
              for h in range(heads)]

    def run(online):
        def step(j, blocks, mask):
            k0 = pl.multiple_of(j * tile, tile)
            logits = [jnp.dot(k_ref[h, pl.ds(k0, blocks * tile), :], q_ref[h], preferred_element_type=F32)
                      for h in range(heads)]
            for h in range(heads):
                s = logits[h] if mask is None else jnp.where(mask, logits[h], NEG)
                _softmax_accumulate(s, v_ref[h, :, pl.ds(k0, blocks * tile)], acc_ref, h, shift=bounds[h],
                                    m_ref=m_ref if online else None)

        step(i, 1, _causal_mask(tile))
        n_big = i // FORGET_BLOCKS_PER_STEP

        def past_big(j, carry):
            step(j * FORGET_BLOCKS_PER_STEP, FORGET_BLOCKS_PER_STEP, None)
            return carry

        def past_single(j, carry):
            step(j, 1, None)
            return carry

        lax.fori_loop(0, n_big, past_big, 0)
        lax.fori_loop(n_big * FORGET_BLOCKS_PER_STEP, i, past_single, 0)

    _two_pass_attention(run, acc_ref, m_ref, heads)
    for h in range(heads):
        g = gate_ref[h]
        o_ref[h] = (_normalized(acc_ref, h) * (1.0 / (1.0 + jnp.exp(-g)))).astype(o_ref.dtype)


def _forgetting_attention(q_aug_t, k_aug, v_aug, gate_t, k_norm_max):
    bsz, nh, kdim, seq = q_aug_t.shape
    dh = gate_t.shape[2]
    tile = min(ATTN_TILE, seq)
    hps = FORGET_HEADS_PER_STEP
    assert seq % tile == 0 and nh % hps == 0
    return pl.pallas_call(
        functools.partial(_forget_kernel, tile=tile, heads=hps, group_heads=nh),
        grid=(bsz, nh // hps, seq // tile),
        in_specs=[_smem(),
                  pl.BlockSpec((None, hps, kdim, tile), lambda b, h, i: (b, h, 0, i)),
                  _resident((None, hps, seq, kdim), lambda b, h, i: (b, h, 0, 0)),
                  _resident((None, hps, V_ROWS, seq), lambda b, h, i: (b, h, 0, 0)),
                  pl.BlockSpec((None, hps, dh, tile), lambda b, h, i: (b, h, 0, i))],
        out_specs=pl.BlockSpec((None, hps, dh, tile), lambda b, h, i: (b, h, 0, i)),
        out_shape=jax.ShapeDtypeStruct((bsz, nh, dh, seq), BF16),
        scratch_shapes=[pltpu.VMEM((hps, 1, tile), F32),
                        pltpu.VMEM((hps, V_ROWS, tile), F32)],
        compiler_params=_cparams(3),
        name="forget_attention",
    )(k_norm_max, q_aug_t, k_aug, v_aug, gate_t)


def _key_to_float(u):
    ks = u ^ jnp.int32(-2147483648)
    bits = jnp.where(ks < 0, ks ^ jnp.int32(2147483647), ks)
    return lax.bitcast_convert_type(bits, F32)


def _dsa_kernel(t5_ref, bmax_ref, kmax_ref, qi_ref, w_ref, ki_ref, q_ref, k_ref, v_ref, o_ref,
                sc_ref, bias_ref, tri_ref, m_ref, acc_ref, *, tile, head0, topk):
    b = pl.program_id(0)
    i = pl.program_id(1)

    @pl.when((b == 0) & (i == 0))
    def _():
        for h in range(N_HEADS):
            _fill_t5_tile(bias_ref, 2 * h, t5_ref, head0 + h, 0, tile)
            _fill_t5_tile(bias_ref, 2 * h + 1, t5_ref, head0 + h, tile, tile)
        tri_ref[...] = jnp.where(lax.broadcasted_iota(I32, (tile, tile), 1) <= lax.broadcasted_iota(I32, (tile, tile), 0),
                                 1.0, 0.0).astype(BF16)

    causal = _causal_mask(tile)

    def index_scores(j):
        k0 = pl.multiple_of(j * tile, tile)
        kt = ki_ref[pl.ds(k0, tile), :]
        raws = [jnp.dot(kt, qi_ref[h * IDX_DIM:(h + 1) * IDX_DIM, :], preferred_element_type=F32)
                for h in range(IDX_HEADS)]
        sc = jnp.zeros((tile, tile), F32)
        for h in range(IDX_HEADS):
            sc = sc + jnp.maximum(raws[h], 0.0) * w_ref[h:h + 1, :]
        return k0, sc

    def score_body(j, carry):
        k0, sc = index_scores(j)
        sc_ref[pl.ds(k0, tile), :] = sc
        return carry

    lax.fori_loop(0, i, score_body, 0)
    k0, sc = index_scores(i)
    sc_ref[pl.ds(k0, tile), :] = jnp.where(causal, sc, NEG)

    def count(pred):
        def body(j, cnt):
            k0 = pl.multiple_of(j * tile, tile)
            return cnt + jnp.sum(jnp.where(pred(sc_ref[pl.ds(k0, tile), :]), 1.0, 0.0), axis=0, keepdims=True)
        return lax.fori_loop(0, i + 1, body, jnp.zeros((1, tile), F32))

    def bit_body(it, ans):
        cand = ans | jnp.left_shift(jnp.int32(1), 31 - it)
        thr = _key_to_float(cand)
        cnt = count(lambda s: s >= thr)
        return jnp.where(cnt >= topk, cand, ans)

    ans = lax.fori_loop(0, 32, bit_body, jnp.zeros((1, tile), I32))
    tau = _key_to_float(ans)
    need = topk - count(lambda s: s > tau)

    qs = [q_ref[h] for h in range(N_HEADS)]

    far_bias = [LOG2E * t5_ref[(T5_BUCKETS - 1) * N_BIAS_HEADS + head0 + h] for h in range(N_HEADS)]
    bounds = [_logit_bound(qs[h], kmax_ref[b * N_HEADS + h], bmax_ref[head0 + h]) for h in range(N_HEADS)]

    def run(online):
        def attend(j, ties_seen, kind):
            k0 = pl.multiple_of(j * tile, tile)
            sc = sc_ref[pl.ds(k0, tile), :]
            eq = sc == tau
            eqf = jnp.where(eq, 1.0, 0.0)
            rank = ties_seen + jnp.dot(tri_ref[...], eqf.astype(BF16), preferred_element_type=F32)
            wgt = jnp.where(eq, jnp.where(rank <= need, 1.0, 0.0), jnp.where(sc > tau, 1.0, 0.0))
            if kind == "diag":
                wgt = jnp.where(causal, wgt, 0.0)
            keep = wgt > 0.5
            kt = k_ref[pl.ds(k0, tile), :]
            logits = [jnp.dot(kt, qs[h], preferred_element_type=F32) for h in range(N_HEADS)]
            for h in range(N_HEADS):
                s, shift = logits[h], bounds[h]
                if kind != "far":
                    s = s + bias_ref[2 * h + (1 if kind == "near" else 0)]
                elif online:
                    s = s + far_bias[h]
                else:
                    shift = shift - far_bias[h]
                s = jnp.where(keep, s, NEG)
                _softmax_accumulate(s, v_ref[h, :, pl.ds(k0, tile)], acc_ref, h, shift=shift,
                                    m_ref=m_ref if online else None, keep=keep)
            return ties_seen + jnp.sum(eqf, axis=0, keepdims=True)

        seen = lax.fori_loop(0, jnp.maximum(i - 1, 0), lambda j, r: attend(j, r, "far"), jnp.zeros((1, tile), F32))
        seen = lax.cond(i >= 1, lambda r: attend(i - 1, r, "near"), lambda r: r, seen)
        attend(i, seen, "diag")

    _two_pass_attention(run, acc_ref, m_ref, N_HEADS)
    for h in range(N_HEADS):
        o_ref[h * HEAD_DIM:(h + 1) * HEAD_DIM, :] = _normalized(acc_ref, h).astype(o_ref.dtype)


def _dsa_attention(qi_t, w_t, k_idx, q_t, k, v_aug, t5_flat, bias_max, k_norm_max):
    bsz, _, width, seq = q_t.shape
    tile = min(SPARSE_TILE, seq)
    topk = min(DSA_TOPK_MAX, seq // 4)
    assert seq % tile == 0 and tile >= topk and tile >= T5_FAR_DIST
    kern = functools.partial(_dsa_kernel, tile=tile, head0=2 * N_HEADS, topk=topk)
    return pl.pallas_call(
        kern,
        grid=(bsz, seq // tile),
        in_specs=[_smem(), _smem(), _smem(),
                  pl.BlockSpec((None, qi_t.shape[1], tile), lambda b, i: (b, 0, i)),
                  pl.BlockSpec((None, 8, tile), lambda b, i: (b, 0, i)),
                  _resident((None, seq, k_idx.shape[2]), lambda b, i: (b, 0, 0)),
                  pl.BlockSpec((None, N_HEADS, width, tile), lambda b, i: (b, 0, 0, i)),
                  _resident((None, seq, width), lambda b, i: (b, 0, 0)),
                  _resident((None, N_HEADS, V_ROWS, seq), lambda b, i: (b, 0, 0, 0))],
        out_specs=pl.BlockSpec((None, width, tile), lambda b, i: (b, 0, i)),
        out_shape=jax.ShapeDtypeStruct((bsz, width, seq), BF16),
        scratch_shapes=[pltpu.VMEM((seq, tile), F32),
                        pltpu.VMEM((2 * N_HEADS, tile, tile), F32),
                        pltpu.VMEM((tile, tile), BF16),
                        pltpu.VMEM((N_HEADS, 1, tile), F32),
                        pltpu.VMEM((N_HEADS, V_ROWS, tile), F32)],
        compiler_params=_cparams(2),
        name="dsa_attention",
    )(t5_flat, bias_max, k_norm_max, qi_t, w_t, k_idx, q_t, k, v_aug)


def _outproj_kernel(x_ref, o_ref, w_ref, g_ref, y_ref):
    y_ref[...] = x_ref[...] + g_ref[...] * jnp.dot(o_ref[...], w_ref[...], preferred_element_type=F32)


def _output_projection(x, o, w_out, gate):
    bsz, seq, d = x.shape
    tm = ROW_TILE
    row = pl.BlockSpec((None, tm, d), lambda b, i: (b, i, 0))
    return pl.pallas_call(
        _outproj_kernel,
        grid=(bsz, seq // tm),
        in_specs=[row, pl.BlockSpec((None, tm, o.shape[2]), lambda b, i: (b, i, 0)),
                  _resident(w_out.shape, lambda b, i: (0, 0)),
                  pl.BlockSpec((None, 1, d), lambda b, i: (b, 0, 0))],
        out_specs=row,
        out_shape=jax.ShapeDtypeStruct(x.shape, F32),
        compiler_params=_cparams(2),
        name="output_projection",
    )(x, o, w_out.astype(BF16), gate)


HALO = 8


def _ffn_kernel(x_ref, xp_ref, gain_ref, sc_ref, sh_ref, g_ref, wg_ref, wv_ref, cw_ref, cb_ref, wd_ref, fg_ref,
                y_ref, h_ref, acc_ref, *, tm, n_chunks, final_norm):
    i = pl.program_id(1)
    c = pl.program_id(2)

    @pl.when(c == 0)
    def _():
        def modulated(x):
            return ((_rms(x) * gain_ref[...]) * (1.0 + sc_ref[...]) + sh_ref[...]).astype(BF16)
        h_ref[HALO:, :] = modulated(x_ref[...])
        halo = modulated(xp_ref[...])
        h_ref[:HALO, :] = jnp.where(i > 0, halo, jnp.zeros_like(halo))
        acc_ref[...] = jnp.zeros(acc_ref.shape, F32)

    h = h_ref[...]

    def conv(w_ref, half):
        u = jnp.dot(h, w_ref[...], preferred_element_type=F32)
        cw = cw_ref[half]
        return (cw[0:1] * u[HALO - 2:HALO - 2 + tm] + cw[1:2] * u[HALO - 1:HALO - 1 + tm]
                + cw[2:3] * u[HALO:HALO + tm]) + cb_ref[half]

    gate = conv(wg_ref, 0)
    val = conv(wv_ref, 1)
    a = (gate * (1.0 / (1.0 + jnp.exp(-gate)))) * val
    acc_ref[...] += jnp.dot(a.astype(BF16), wd_ref[...], preferred_element_type=F32)

    @pl.when(c == n_chunks - 1)
    def _():
        y = x_ref[...] + g_ref[...] * acc_ref[...]
        if final_norm:
            y = _rms(y) * fg_ref[...]
        y_ref[...] = y


def _ffn(x, gain, sc, sh, gate, w_up, conv_w, conv_b, w_down, final_gain, final_norm):
    bsz, seq, d = x.shape
    tm = min(FFN_ROW_TILE, seq)
    fc = FFN_CHUNK
    n_chunks = D_FF // fc
    w_up_b = w_up.astype(BF16)
    cw = conv_w.reshape(3, 2, D_FF).transpose(1, 0, 2)
    cb = conv_b.reshape(2, 1, D_FF)
    vec = pl.BlockSpec((None, 1, d), lambda b, i, c: (b, 0, 0))
    one = pl.BlockSpec((1, d), lambda b, i, c: (0, 0))
    row = pl.BlockSpec((None, tm, d), lambda b, i, c: (b, i, 0))
    return pl.pallas_call(
        functools.partial(_ffn_kernel, tm=tm, n_chunks=n_chunks, final_norm=final_norm),
        grid=(bsz, seq // tm, n_chunks),
        in_specs=[row,
                  pl.BlockSpec((None, HALO, d), lambda b, i, c: (b, jnp.maximum(i * (tm // HALO) - 1, 0), 0)),
                  one, vec, vec, vec,
                  pl.BlockSpec((d, fc), lambda b, i, c: (0, c)),
                  pl.BlockSpec((d, fc), lambda b, i, c: (0, n_chunks + c)),
                  pl.BlockSpec((2, 3, fc), lambda b, i, c: (0, 0, c)),
                  pl.BlockSpec((2, 1, fc), lambda b, i, c: (0, 0, c)),
                  pl.BlockSpec((fc, d), lambda b, i, c: (c, 0)),
                  one],
        out_specs=row,
        out_shape=jax.ShapeDtypeStruct(x.shape, F32),
        scratch_shapes=[pltpu.VMEM((HALO + tm, d), BF16), pltpu.VMEM((tm, d), F32)],
        compiler_params=_cparams(3),
        name="conv_glu_ffn",
    )(x, x, gain.reshape(1, d), sc, sh, gate, w_up_b, w_up_b, cw, cb, w_down.astype(BF16),
      final_gain.reshape(1, d))


def kernel(x, c, w_ada, b_ada, norm1_gain, w_in, f_bias, diff_lambda, diff_subln_gain, w_dq_up, w_didx_q,
           w_dkv_up, w_out, t5_table, norm2_gain, w_ffn_up, ffn_conv_w, ffn_conv_b, w_ffn_down, final_gain):
    bsz, seq, d = x.shape
    depth = w_ada.shape[0]
    t5_flat = t5_table.reshape(-1)
    bias_max = jnp.max(t5_table, axis=0) * LOG2E
    mod = _modulation(c, w_ada, b_ada)
    w = GROUP_W

    for l in range(depth):
        sh1, sc1, g1, sh2, sc2, g2 = [m[:, None, :] for m in jnp.split(mod[l], 6, axis=-1)]
        pa, pb, pc, gc, dq, dkv, misc = _input_projection(x, norm1_gain[l], sc1, sh1, w_in[l],
                                                          w_dq_up[l], w_didx_q[l], w_dkv_up[l])
        k_idx = misc[..., :IDX_DIM].astype(BF16)
        fc = misc[..., IDX_DIM:IDX_DIM + N_HEADS]
        w_idx = misc[..., IDX_DIM + N_HEADS:IDX_DIM + N_HEADS + IDX_HEADS]

        lambda_init = 0.8 - 0.6 * math.exp(-0.3 * l)
        lq1, lk1, lq2, lk2 = diff_lambda[l]
        lam = jnp.exp(jnp.sum(lq1 * lk1)) - jnp.exp(jnp.sum(lq2 * lk2)) + lambda_init
        qa_t = _heads_t(pa[..., :w]).reshape(bsz, N_HEADS, 2, DIFF_HALF, seq)
        zeros = jnp.zeros_like(qa_t)
        qa_t = jnp.stack([jnp.concatenate([qa_t[:, :, 0], zeros[:, :, 0]], axis=2),
                          jnp.concatenate([zeros[:, :, 1], qa_t[:, :, 1]], axis=2)], axis=2)
        o_a = _diff_attention(qa_t, _heads(pa[..., w:2 * w]), _values_aug(pa[..., 2 * w:]), lam,
                              diff_subln_gain[l], t5_flat, bias_max, _key_norm_max(pa[..., w:2 * w]),
                              1.0 - lambda_init)

        kb = _heads(pb[..., w:2 * w])
        o_b = _moba_attention(_heads_t(pb[..., :w]), kb, _values_aug(pb[..., 2 * w:]),
                              _moba_kmean(kb).astype(BF16), t5_flat, bias_max, _key_norm_max(pb[..., w:2 * w]))

        hi, mid, lo = _forget_cumsum(fc + f_bias[l])
        ones = jnp.ones_like(hi)
        cparts = jnp.stack([hi, mid, lo], axis=2)
        onep = jnp.stack([ones] * 3, axis=2)
        padw = 2 * HEAD_DIM - HEAD_DIM - 6
        qc_aug = jnp.concatenate([_heads_t(pc[..., :w]), onep.astype(BF16), cparts.astype(BF16),
                                  jnp.zeros((bsz, N_HEADS, padw, seq), BF16)], axis=2)
        kc_aug = jnp.concatenate([_heads(pc[..., w:2 * w]), (-cparts).transpose(0, 1, 3, 2).astype(BF16),
                                  onep.transpose(0, 1, 3, 2).astype(BF16),
                                  jnp.zeros((bsz, N_HEADS, seq, padw), BF16)], axis=3)
        o_c = _forgetting_attention(qc_aug, kc_aug, _values_aug(pc[..., 2 * w:]), _heads_t(gc),
                                    _key_norm_max(pc[..., w:2 * w]))

        qd_t = dq[..., :w].transpose(0, 2, 1)
        head_of_row = jnp.arange(w)[None, :, None] // HEAD_DIM
        qd_t = jnp.stack([jnp.where(head_of_row == h, qd_t, jnp.zeros((), BF16)) for h in range(N_HEADS)],
                         axis=1)
        qi_t = dq[..., w:].transpose(0, 2, 1)
        w_t = (w_idx * (IDX_HEADS ** -0.5 * IDX_DIM ** -0.5)).transpose(0, 2, 1)
        w_t = jnp.concatenate([w_t, jnp.zeros((bsz, 8 - IDX_HEADS, seq), F32)], axis=1)
        o_d = _dsa_attention(qi_t, w_t, k_idx, qd_t, dkv[..., :w], _values_aug(dkv[..., w:]), t5_flat, bias_max,
                             _key_norm_max(dkv[..., :w]))

        o = jnp.concatenate([_unheads_t(o_a), _unheads_t(o_b), _unheads_t(o_c), o_d.transpose(0, 2, 1)], axis=-1)
        x = _output_projection(x, o, w_out[l], g1)
        x = _ffn(x, norm2_gain[l], sc2, sh2, g2, w_ffn_up[l], ffn_conv_w[l], ffn_conv_b[l], w_ffn_down[l],
                 final_gain, final_norm=(l == depth - 1))
    return x
```

```python
import functools
import math

import numpy as np
import jax
import jax.numpy as jnp
from jax import lax
from jax.experimental import pallas as pl
from jax.experimental.pallas import tpu as pltpu

F32 = jnp.float32
BF16 = jnp.bfloat16
I32 = jnp.int32

HEAD_DIM = 64
N_HEADS = 4
GROUP_W = N_HEADS * HEAD_DIM
DIFF_HALF = HEAD_DIM // 2
MOBA_BLOCK = 256
MOBA_TOPK = 3
DSA_RANK = 128
IDX_HEADS = 4
IDX_DIM = 64
DSA_TOPK_MAX = 256
T5_BUCKETS = 32
T5_MAX_DIST = 128
N_BIAS_HEADS = 12
D_FF = 2816
EPS = 1e-6
NEG = -1e30
LOG2E = math.log2(math.e)
V_ROWS = HEAD_DIM + 16
BOUND_SLACK = 1.0 + 2.0 ** -6
MIN_DENOMINATOR = 2.0 ** -60
BF16_ROWS = 16

V7X_VMEM_LIMIT_BYTES = 56 * 1024 * 1024

ATTN_TILE = 512
SPARSE_TILE = 256
ROW_TILE = 512
FFN_ROW_TILE = 1024
FFN_CHUNK = 256
FORGET_HEADS_PER_STEP = 4
DIFF_HEADS_PER_STEP = 2
FORGET_BLOCKS_PER_STEP = 2


def _t5_thresholds():
    d = np.arange(0, 4 * T5_MAX_DIST)
    max_exact = T5_BUCKETS // 2
    ratio = np.maximum(d, 1).astype(np.float32) / max_exact
    large = max_exact + (np.log(ratio) / math.log(T5_MAX_DIST / max_exact) * (T5_BUCKETS - max_exact)).astype(np.int32)
    bucket = np.where(d < max_exact, d, np.minimum(large, T5_BUCKETS - 1))
    return [int(np.argmax(bucket >= b)) for b in range(T5_BUCKETS)]


T5_THRESH = _t5_thresholds()
T5_FAR_DIST = T5_THRESH[-1]


def _cparams(n_axes):
    return pltpu.CompilerParams(dimension_semantics=("arbitrary",) * n_axes,
                                vmem_limit_bytes=V7X_VMEM_LIMIT_BYTES)


def _resident(block_shape, index_map):
    return pl.BlockSpec(block_shape, index_map, pipeline_mode=pl.Buffered(1))


def _smem():
    return pl.BlockSpec(memory_space=pltpu.SMEM)


def _fill_t5_tile(bias_ref, slot, t5_ref, head, delta, tile):
    rows = 64

    def body(r, carry):
        r0 = pl.multiple_of(r * rows, rows)
        kr = lax.broadcasted_iota(I32, (rows, tile), 0) + r0
        qc = lax.broadcasted_iota(I32, (rows, tile), 1)
        d = delta + qc - kr
        val = jnp.full((rows, tile), LOG2E * t5_ref[head], F32)
        for b in range(1, T5_BUCKETS):
            val = jnp.where(d >= T5_THRESH[b], LOG2E * t5_ref[b * N_BIAS_HEADS + head], val)
        bias_ref[slot, pl.ds(r0, rows), :] = val
        return carry

    lax.fori_loop(0, tile // rows, body, 0)


def _causal_mask(tile):
    kr = lax.broadcasted_iota(I32, (tile, tile), 0)
    qc = lax.broadcasted_iota(I32, (tile, tile), 1)
    return kr <= qc


def _softmax_accumulate(s, v_aug, acc_ref, idx, *, shift=None, m_ref=None, keep=None):
    if m_ref is None:
        p = jnp.exp2(s - shift)
        acc_ref[idx] += jnp.dot(v_aug, p.astype(BF16), preferred_element_type=F32)
        return
    m_old = m_ref[idx]
    m_new = jnp.maximum(m_old, jnp.max(s, axis=0, keepdims=True))
    p = jnp.exp2(s - m_new)
    if keep is not None:
        p = jnp.where(keep, p, 0.0)
    acc_ref[idx] = jnp.exp2(m_old - m_new) * acc_ref[idx] + jnp.dot(v_aug, p.astype(BF16), preferred_element_type=F32)
    m_ref[idx] = m_new


def _logit_bound(q, k_norm_max, extra):
    qf = q.astype(F32)
    q_norm = jnp.sqrt(jnp.sum(qf * qf, axis=0, keepdims=True))
    return q_norm * (k_norm_max * BOUND_SLACK) + (extra + 1.0)


def _denominators_ok(acc_ref, n_chains):
    low = acc_ref[0, HEAD_DIM:HEAD_DIM + 1, :]
    for idx in range(1, n_chains):
        low = jnp.minimum(low, acc_ref[idx, HEAD_DIM:HEAD_DIM + 1, :])
    return jnp.min(low) > MIN_DENOMINATOR


def _two_pass_attention(run, acc_ref, m_ref, n_chains):
    acc_ref[...] = jnp.zeros(acc_ref.shape, F32)
    run(False)

    @pl.when(jnp.logical_not(_denominators_ok(acc_ref, n_chains)))
    def _():
        acc_ref[...] = jnp.zeros(acc_ref.shape, F32)
        m_ref[...] = jnp.full(m_ref.shape, NEG, F32)
        run(True)


def _normalized(acc_ref, idx):
    return acc_ref[idx, :HEAD_DIM, :] / acc_ref[idx, HEAD_DIM:HEAD_DIM + 1, :]


def _mod_kernel(c_ref, w_ref, b_ref, o_ref):
    c = c_ref[...]
    cond = c * (1.0 / (1.0 + jnp.exp(-c)))
    o_ref[...] = jnp.dot(cond.astype(BF16), w_ref[...].astype(BF16), preferred_element_type=F32) + b_ref[...]


def _modulation(c, w_ada, b_ada):
    depth, d, n = w_ada.shape
    bsz = c.shape[0]
    rows = 8
    c_pad = jnp.zeros((rows, d), F32).at[:bsz].set(c)
    tn = 1024
    out = pl.pallas_call(
        _mod_kernel,
        grid=(depth, n // tn),
        in_specs=[pl.BlockSpec((rows, d), lambda l, j: (0, 0)),
                  pl.BlockSpec((None, d, tn), lambda l, j: (l, 0, j)),
                  pl.BlockSpec((None, 1, tn), lambda l, j: (l, 0, j))],
        out_specs=pl.BlockSpec((None, rows, tn), lambda l, j: (l, 0, j)),
        out_shape=jax.ShapeDtypeStruct((depth, rows, n), F32),
        compiler_params=_cparams(2),
        name="adaln_modulation",
    )(c_pad, w_ada, b_ada.reshape(depth, 1, n))
    return out[:, :bsz]


def _rms(x):
    return x * lax.rsqrt(jnp.mean(x * x, axis=-1, keepdims=True) + EPS)


def _inproj_kernel(x_ref, gain_ref, sc_ref, sh_ref, w_ref, wq_ref, wkv_ref,
                   pa_ref, pb_ref, pc_ref, gc_ref, dq_ref, dkv_ref, misc_ref):
    x = x_ref[...]
    h = (_rms(x) * gain_ref[...]) * (1.0 + sc_ref[...]) + sh_ref[...]
    hb = h.astype(BF16)
    w3 = 3 * GROUP_W

    def proj(c0, c1):
        return jnp.dot(hb, w_ref[:, c0:c1], preferred_element_type=F32)

    pa_ref[...] = proj(0, w3).astype(BF16)
    pb_ref[...] = proj(w3, 2 * w3).astype(BF16)
    pc_ref[...] = proj(2 * w3, 3 * w3).astype(BF16)
    c0 = 3 * w3
    gc_ref[...] = proj(c0, c0 + GROUP_W)
    c0 += GROUP_W
    q_lat = _rms(proj(c0, c0 + DSA_RANK))
    kv_lat = _rms(proj(c0 + DSA_RANK, c0 + 2 * DSA_RANK))
    dq_ref[...] = jnp.dot(q_lat.astype(BF16), wq_ref[...], preferred_element_type=F32).astype(BF16)
    dkv_ref[...] = jnp.dot(kv_lat.astype(BF16), wkv_ref[...], preferred_element_type=F32).astype(BF16)
    c0 += 2 * DSA_RANK
    misc_ref[...] = proj(c0, c0 + 128)


def _input_projection(x, gain, sc, sh, w_in, w_dq_up, w_didx_q, w_dkv_up):
    bsz, seq, d = x.shape
    w3 = 3 * GROUP_W
    o_c = 2 * w3
    o_fc = o_c + w3
    o_gc = o_fc + N_HEADS
    o_ql = o_gc + GROUP_W
    o_kv = o_ql + DSA_RANK
    o_ki = o_kv + DSA_RANK
    o_wi = o_ki + IDX_DIM
    pad = jnp.zeros((d, 128 - IDX_DIM - N_HEADS - IDX_HEADS), w_in.dtype)
    col = jnp.arange(o_fc)
    is_q = (col % w3) < GROUP_W
    q_scale = jnp.where(is_q, jnp.where(col < w3, DIFF_HALF ** -0.5, HEAD_DIM ** -0.5) * LOG2E, 1.0).astype(F32)
    w = jnp.concatenate([w_in[:, :o_fc] * q_scale[None, :], w_in[:, o_gc:o_ql], w_in[:, o_ql:o_ki],
                         w_in[:, o_ki:o_wi], w_in[:, o_fc:o_gc], w_in[:, o_wi:], pad], axis=1).astype(BF16)
    n_cols = w.shape[1]
    wq = jnp.concatenate([w_dq_up * (HEAD_DIM ** -0.5 * LOG2E), w_didx_q], axis=1).astype(BF16)
    wkv = w_dkv_up.astype(BF16)
    tm = ROW_TILE
    row = lambda width: pl.BlockSpec((None, tm, width), lambda b, i: (b, i, 0))
    vec = pl.BlockSpec((None, 1, d), lambda b, i: (b, 0, 0))
    outs = pl.pallas_call(
        _inproj_kernel,
        grid=(bsz, seq // tm),
        in_specs=[row(d), pl.BlockSpec((1, d), lambda b, i: (0, 0)), vec, vec,
                  _resident((d, n_cols), lambda b, i: (0, 0)),
                  _resident(wq.shape, lambda b, i: (0, 0)),
                  _resident(wkv.shape, lambda b, i: (0, 0))],
        out_specs=[row(w3), row(w3), row(w3), row(GROUP_W), row(512), row(512), row(128)],
        out_shape=[jax.ShapeDtypeStruct((bsz, seq, w3), BF16)] * 3
                  + [jax.ShapeDtypeStruct((bsz, seq, GROUP_W), F32)]
                  + [jax.ShapeDtypeStruct((bsz, seq, 512), BF16)] * 2
                  + [jax.ShapeDtypeStruct((bsz, seq, 128), F32)],
        compiler_params=_cparams(2),
        name="input_projection",
    )(x, gain.reshape(1, d), sc, sh, w, wq, wkv)
    return outs


def _heads_t(t):
    bsz, seq, w = t.shape
    return t.reshape(bsz, seq, N_HEADS, w // N_HEADS).transpose(0, 2, 3, 1)


def _heads(t):
    bsz, seq, w = t.shape
    return t.reshape(bsz, seq, N_HEADS, w // N_HEADS).transpose(0, 2, 1, 3)


def _values_aug(v):
    v_t = _heads_t(v)
    bsz, nh, dh, seq = v_t.shape
    return jnp.concatenate([v_t, jnp.ones((bsz, nh, 1, seq), v.dtype),
                            jnp.zeros((bsz, nh, V_ROWS - dh - 1, seq), v.dtype)], axis=2)


def _key_norm_max(k):
    bsz, seq, w = k.shape
    kf = k.astype(F32).reshape(bsz, seq, N_HEADS, w // N_HEADS)
    return jnp.max(jnp.sqrt(jnp.sum(kf * kf, axis=-1)), axis=1).reshape(-1)


def _unheads_t(o_t):
    bsz, h, dh, seq = o_t.shape
    return o_t.transpose(0, 3, 1, 2).reshape(bsz, seq, h * dh)


def _diff_kernel(t5_ref, bmax_ref, kmax_ref, lam_ref, q_ref, k_ref, v_ref, g_ref, o_ref, bias_ref, m_ref, acc_ref,
                 *, tile, head0, heads, group_heads, out_scale):
    b = pl.program_id(0)
    hg = pl.program_id(1)
    i = pl.program_id(2)

    @pl.when(i == 0)
    def _():
        for h in range(heads):
            _fill_t5_tile(bias_ref, 2 * h, t5_ref, head0 + hg * heads + h, 0, tile)
            _fill_t5_tile(bias_ref, 2 * h + 1, t5_ref, head0 + hg * heads + h, tile, tile)

    far_bias = [LOG2E * t5_ref[(T5_BUCKETS - 1) * N_BIAS_HEADS + head0 + hg * heads + h] for h in range(heads)]
    bounds = [_logit_bound(q_ref[h, c], kmax_ref[b * group_heads + hg * heads + h], bmax_ref[head0 + hg * heads + h])
              for h in range(heads) for c in range(2)]

    def run(online):
        def step(j, kind):
            k0 = pl.multiple_of(j * tile, tile)
            logits = [jnp.dot(k_ref[h, pl.ds(k0, tile), :], q_ref[h, c], preferred_element_type=F32)
                      for h in range(heads) for c in range(2)]
            for h in range(heads):
                vt = v_ref[h, :, pl.ds(k0, tile)]
                for c in range(2):
                    idx = 2 * h + c
                    s, shift = logits[idx], bounds[idx]
                    if kind != "far":
                        s = s + bias_ref[2 * h + (1 if kind == "near" else 0)]
                    elif online:
                        s = s + far_bias[h]
                    else:
                        shift = shift - far_bias[h]
                    if kind == "diag":
                        s = jnp.where(_causal_mask(tile), s, NEG)
                    _softmax_accumulate(s, vt, acc_ref, idx, shift=shift, m_ref=m_ref if online else None)

        step(i, "diag")

        @pl.when(i >= 1)
        def _():
            step(i - 1, "near")

        def far(j, carry):
            step(j, "far")
            return carry

        lax.fori_loop(0, jnp.maximum(i - 1, 0), far, 0)

    _two_pass_attention(run, acc_ref, m_ref, 2 * heads)

    for h in range(heads):
        o = _normalized(acc_ref, 2 * h) - lam_ref[0] * _normalized(acc_ref, 2 * h + 1)
        y = o * lax.rsqrt(jnp.mean(o * o, axis=0, keepdims=True) + EPS)
        o_ref[h] = ((y * g_ref[...]) * out_scale).astype(o_ref.dtype)


def _diff_attention(q_t, k, v_aug, lam, subln_gain, t5_flat, bias_max, k_norm_max, out_scale):
    bsz, nh, _, dh, seq = q_t.shape
    tile = min(ATTN_TILE, seq)
    hps = DIFF_HEADS_PER_STEP
    assert tile >= T5_FAR_DIST and seq % tile == 0 and nh % hps == 0
    kern = functools.partial(_diff_kernel, tile=tile, head0=0, heads=hps, group_heads=nh, out_scale=out_scale)
    return pl.pallas_call(
        kern,
        grid=(bsz, nh // hps, seq // tile),
        in_specs=[_smem(), _smem(), _smem(), _smem(),
                  pl.BlockSpec((None, hps, 2, dh, tile), lambda b, h, i: (b, h, 0, 0, i)),
                  _resident((None, hps, seq, dh), lambda b, h, i: (b, h, 0, 0)),
                  _resident((None, hps, V_ROWS, seq), lambda b, h, i: (b, h, 0, 0)),
                  pl.BlockSpec((dh, 1), lambda b, h, i: (0, 0))],
        out_specs=pl.BlockSpec((None, hps, dh, tile), lambda b, h, i: (b, h, 0, i)),
        out_shape=jax.ShapeDtypeStruct((bsz, nh, dh, seq), BF16),
        scratch_shapes=[pltpu.VMEM((2 * hps, tile, tile), F32),
                        pltpu.VMEM((2 * hps, 1, tile), F32),
                        pltpu.VMEM((2 * hps, V_ROWS, tile), F32)],
        compiler_params=_cparams(3),
        name="diff_attention",
    )(t5_flat, bias_max, k_norm_max, lam.reshape(1), q_t, k, v_aug, subln_gain.reshape(dh, 1))


def _kmean_kernel(k_ref, o_ref, *, blocks):
    k = k_ref[...].astype(F32)
    o_ref[...] = jnp.mean(k.reshape(blocks, MOBA_BLOCK, k.shape[-1]), axis=1)


def _moba_kmean(k):
    bsz, nh, seq, dh = k.shape
    nb = seq // MOBA_BLOCK
    blocks = min(8, nb)
    return pl.pallas_call(
        functools.partial(_kmean_kernel, blocks=blocks),
        grid=(bsz, nh, nb // blocks),
        in_specs=[pl.BlockSpec((None, None, blocks * MOBA_BLOCK, dh), lambda b, h, i: (b, h, i, 0))],
        out_specs=pl.BlockSpec((None, None, blocks, dh), lambda b, h, i: (b, h, i, 0)),
        out_shape=jax.ShapeDtypeStruct((bsz, nh, nb, dh), F32),
        compiler_params=_cparams(3),
    )(k)


def _moba_kernel(t5_ref, bmax_ref, kmax_ref, q_ref, k_ref, v_ref, km_ref, o_ref, bias_ref, sel_ref, m_ref, acc_ref,
                 *, tile, head0, heads, n_blocks):
    b = pl.program_id(0)
    i = pl.program_id(1)

    @pl.when(i == 0)
    def _():
        for h in range(heads):
            _fill_t5_tile(bias_ref, 2 * h, t5_ref, head0 + h, 0, tile)
            _fill_t5_tile(bias_ref, 2 * h + 1, t5_ref, head0 + h, tile, tile)

    far_bias = [LOG2E * t5_ref[(T5_BUCKETS - 1) * N_BIAS_HEADS + head0 + h] for h in range(heads)]
    bounds = [_logit_bound(q_ref[h], kmax_ref[b * heads + h], bmax_ref[head0 + h]) for h in range(heads)]

    nidx = lax.broadcasted_iota(I32, (n_blocks, tile), 0)
    for h in range(heads):
        gate = jnp.dot(km_ref[h], q_ref[h], preferred_element_type=F32)
        g = jnp.where(nidx < i, gate, NEG)
        sel = jnp.zeros(gate.shape, F32)
        for _ in range(min(MOBA_TOPK, n_blocks)):
            mx = jnp.max(g, axis=0, keepdims=True)
            first = jnp.min(jnp.where(g == mx, nidx, n_blocks), axis=0, keepdims=True)
            pick = nidx == first
            sel = jnp.where(pick, 1.0, sel)
            g = jnp.where(pick, -jnp.inf, g)
        sel_ref[h] = jnp.where(nidx < i, sel, 0.0)

    def run(online):
        def step(j, kind):
            k0 = pl.multiple_of(j * tile, tile)
            logits = [jnp.dot(k_ref[h, pl.ds(k0, tile), :], q_ref[h], preferred_element_type=F32)
                      for h in range(heads)]
            for h in range(heads):
                s, shift = logits[h], bounds[h]
                if kind != "far":
                    s = s + bias_ref[2 * h + (1 if kind == "near" else 0)]
                elif online:
                    s = s + far_bias[h]
                else:
                    shift = shift - far_bias[h]
                mask = _causal_mask(tile) if kind == "diag" else sel_ref[h, pl.ds(j, 1), :] > 0.5
                s = jnp.where(mask, s, NEG)
                _softmax_accumulate(s, v_ref[h, :, pl.ds(k0, tile)], acc_ref, h, shift=shift,
                                    m_ref=m_ref if online else None)

        step(i, "diag")

        @pl.when(i >= 1)
        def _():
            step(i - 1, "near")

        def far(j, carry):
            step(j, "far")
            return carry

        lax.fori_loop(0, jnp.maximum(i - 1, 0), far, 0)

    _two_pass_attention(run, acc_ref, m_ref, heads)
    for h in range(heads):
        o_ref[h] = _normalized(acc_ref, h).astype(o_ref.dtype)


def _moba_attention(q_t, k, v_aug, kmean, t5_flat, bias_max, k_norm_max):
    bsz, nh, dh, seq = q_t.shape
    tile = MOBA_BLOCK
    assert SPARSE_TILE == MOBA_BLOCK and seq % tile == 0 and tile >= T5_FAR_DIST
    nb = seq // tile
    kern = functools.partial(_moba_kernel, tile=tile, head0=N_HEADS, heads=nh, n_blocks=nb)
    return pl.pallas_call(
        kern,
        grid=(bsz, nb),
        in_specs=[_smem(), _smem(), _smem(),
                  pl.BlockSpec((None, nh, dh, tile), lambda b, i: (b, 0, 0, i)),
                  _resident((None, nh, seq, dh), lambda b, i: (b, 0, 0, 0)),
                  _resident((None, nh, V_ROWS, seq), lambda b, i: (b, 0, 0, 0)),
                  _resident((None, nh, nb, dh), lambda b, i: (b, 0, 0, 0))],
        out_specs=pl.BlockSpec((None, nh, dh, tile), lambda b, i: (b, 0, 0, i)),
        out_shape=jax.ShapeDtypeStruct((bsz, nh, dh, seq), BF16),
        scratch_shapes=[pltpu.VMEM((2 * nh, tile, tile), F32), pltpu.VMEM((nh, nb, tile), F32),
                        pltpu.VMEM((nh, 1, tile), F32),
                        pltpu.VMEM((nh, V_ROWS, tile), F32)],
        compiler_params=_cparams(2),
        name="moba_attention",
    )(t5_flat, bias_max, k_norm_max, q_t, k, v_aug, kmean)


def _split3(x):
    def trunc(v):
        bits = lax.bitcast_convert_type(v, I32)
        return lax.bitcast_convert_type(bits & jnp.int32(-65536), F32)
    hi = trunc(x)
    r1 = x - hi
    mid = trunc(r1)
    lo = trunc(r1 - mid)
    return hi, mid, lo


def _forget_cumsum_kernel(f_ref, hi_ref, mid_ref, lo_ref):
    x = f_ref[...]
    rows = x.shape[0]
    ls = jnp.minimum(x, 0.0) - jnp.log(1.0 + jnp.exp(-jnp.abs(x)))
    upper = (lax.broadcasted_iota(I32, (128, 128), 0) <= lax.broadcasted_iota(I32, (128, 128), 1)).astype(F32)
    within = jnp.dot(ls, upper, preferred_element_type=F32, precision=lax.Precision.HIGHEST)
    strict = (lax.broadcasted_iota(I32, (rows, rows), 1) < lax.broadcasted_iota(I32, (rows, rows), 0)).astype(F32)
    before = jnp.dot(strict, within, preferred_element_type=F32, precision=lax.Precision.HIGHEST)
    cf = (within + before[:, 127:128]) * LOG2E
    hi, mid, lo = _split3(cf)
    hi_ref[...] = hi
    mid_ref[...] = mid
    lo_ref[...] = lo


def _forget_cumsum(f_logit):
    bsz, seq, nh = f_logit.shape
    rows = seq // 128
    f = f_logit.transpose(0, 2, 1).reshape(bsz, nh, rows, 128)
    spec = pl.BlockSpec((None, None, rows, 128), lambda b, h: (b, h, 0, 0))
    parts = pl.pallas_call(
        _forget_cumsum_kernel,
        grid=(bsz, nh),
        in_specs=[spec],
        out_specs=[spec] * 3,
        out_shape=[jax.ShapeDtypeStruct((bsz, nh, rows, 128), F32)] * 3,
        compiler_params=_cparams(2),
        name="forget_cumsum",
    )(f)
    return [p.reshape(bsz, nh, seq) for p in parts]


def _forget_kernel(kmax_ref, q_ref, k_ref, v_ref, gate_ref, o_ref, m_ref, acc_ref, *, tile, heads, group_heads):
    b = pl.program_id(0)
    hg = pl.program_id(1)
    i = pl.program_id(2)
    bounds = [_logit_bound(q_ref[h, :HEAD_DIM, :], kmax_ref[b * group_heads + hg * heads + h], 0.0)
              for h in range(heads)]

    def run(online):
        def step(j, blocks, mask):
            k0 = pl.multiple_of(j * tile, tile)
            logits = [jnp.dot(k_ref[h, pl.ds(k0, blocks * tile), :], q_ref[h], preferred_element_type=F32)
                      for h in range(heads)]
            for h in range(heads):
                s = logits[h] if mask is None else jnp.where(mask, logits[h], NEG)
                _softmax_accumulate(s, v_ref[h, :, pl.ds(k0, blocks * tile)], acc_ref, h, shift=bounds[h],
                                    m_ref=m_ref if online else None)

        step(i, 1, _causal_mask(tile))
        n_big = i // FORGET_BLOCKS_PER_STEP

        def past_big(j, carry):
            step(j * FORGET_BLOCKS_PER_STEP, FORGET_BLOCKS_PER_STEP, None)
            return carry

        def past_single(j, carry):
            step(j, 1, None)
            return carry

        lax.fori_loop(0, n_big, past_big, 0)
        lax.fori_loop(n_big * FORGET_BLOCKS_PER_STEP, i, past_single, 0)

    _two_pass_attention(run, acc_ref, m_ref, heads)
    for h in range(heads):
        g = gate_ref[h]
        o_ref[h] = (_normalized(acc_ref, h) * (1.0 / (1.0 + jnp.exp(-g)))).astype(o_ref.dtype)


def _forgetting_attention(q_aug_t, k_aug, v_aug, gate_t, k_norm_max):
    bsz, nh, kdim, seq = q_aug_t.shape
    dh = gate_t.shape[2]
    tile = min(ATTN_TILE, seq)
    hps = FORGET_HEADS_PER_STEP
    assert seq % tile == 0 and nh % hps == 0
    return pl.pallas_call(
        functools.partial(_forget_kernel, tile=tile, heads=hps, group_heads=nh),
        grid=(bsz, nh // hps, seq // tile),
        in_specs=[_smem(),
                  pl.BlockSpec((None, hps, kdim, tile), lambda b, h, i: (b, h, 0, i)),
                  _resident((None, hps, seq, kdim), lambda b, h, i: (b, h, 0, 0)),
                  _resident((None, hps, V_ROWS, seq), lambda b, h, i: (b, h, 0, 0)),
                  pl.BlockSpec((None, hps, dh, tile), lambda b, h, i: (b, h, 0, i))],
        out_specs=pl.BlockSpec((None, hps, dh, tile), lambda b, h, i: (b, h, 0, i)),
        out_shape=jax.ShapeDtypeStruct((bsz, nh, dh, seq), BF16),
        scratch_shapes=[pltpu.VMEM((hps, 1, tile), F32),
                        pltpu.VMEM((hps, V_ROWS, tile), F32)],
        compiler_params=_cparams(3),
        name="forget_attention",
    )(k_norm_max, q_aug_t, k_aug, v_aug, gate_t)


def _key_to_float(u):
    ks = u ^ jnp.int32(-2147483648)
    bits = jnp.where(ks < 0, ks ^ jnp.int32(2147483647), ks)
    return lax.bitcast_convert_type(bits, F32)


def _truncate_to_bf16(x):
    bits = lax.bitcast_convert_type(x, I32) & jnp.int32(-65536)
    return lax.bitcast_convert_type(bits, F32).astype(BF16)


def _dsa_kernel(t5_ref, bmax_ref, kmax_ref, qi_ref, w_ref, ki_ref, q_ref, k_ref, v_ref, o_ref,
                sc_ref, hi_ref, bias_ref, tri_ref, m_ref, acc_ref, *, tile, head0, topk):
    b = pl.program_id(0)
    i = pl.program_id(1)

    @pl.when((b == 0) & (i == 0))
    def _():
        for h in range(N_HEADS):
            _fill_t5_tile(bias_ref, 2 * h, t5_ref, head0 + h, 0, tile)
            _fill_t5_tile(bias_ref, 2 * h + 1, t5_ref, head0 + h, tile, tile)
        tri_ref[...] = jnp.where(lax.broadcasted_iota(I32, (tile, tile), 1) <= lax.broadcasted_iota(I32, (tile, tile), 0),
                                 1.0, 0.0).astype(BF16)

    causal = _causal_mask(tile)

    def index_scores(j):
        k0 = pl.multiple_of(j * tile, tile)
        kt = ki_ref[pl.ds(k0, tile), :]
        raws = [jnp.dot(kt, qi_ref[h * IDX_DIM:(h + 1) * IDX_DIM, :], preferred_element_type=F32)
                for h in range(IDX_HEADS)]
        sc = jnp.zeros((tile, tile), F32)
        for h in range(IDX_HEADS):
            sc = sc + jnp.maximum(raws[h], 0.0) * w_ref[h:h + 1, :]
        return k0, sc

    def store_scores(k0, sc):
        sc_ref[pl.ds(k0, tile), :] = sc
        hi_ref[pl.ds(k0, tile), :] = _truncate_to_bf16(sc)

    def score_body(j, carry):
        store_scores(*index_scores(j))
        return carry

    lax.fori_loop(0, i, score_body, 0)
    k0, sc = index_scores(i)
    store_scores(k0, jnp.where(causal, sc, NEG))

    def count(pred):
        def body(j, cnt):
            k0 = pl.multiple_of(j * tile, tile)
            return cnt + jnp.sum(jnp.where(pred(sc_ref[pl.ds(k0, tile), :]), 1.0, 0.0), axis=0, keepdims=True)
        return lax.fori_loop(0, i + 1, body, jnp.zeros((1, tile), F32))

    def count_truncated(thr):
        def body(j, cnt):
            k0 = pl.multiple_of(j * tile, tile)
            ind = jnp.where(hi_ref[pl.ds(k0, tile), :] >= thr, jnp.ones((), BF16), jnp.zeros((), BF16))
            part = ind[0:BF16_ROWS]
            for r in range(1, tile // BF16_ROWS):
                part = part + ind[r * BF16_ROWS:(r + 1) * BF16_ROWS]
            return cnt + part.astype(F32)
        cnt = lax.fori_loop(0, i + 1, body, jnp.zeros((BF16_ROWS, tile), F32))
        return jnp.sum(cnt, axis=0, keepdims=True)

    def bit_body(it, ans, counter):
        cand = ans | jnp.left_shift(jnp.int32(1), 31 - it)
        return jnp.where(counter(_key_to_float(cand)) >= topk, cand, ans)

    ans = lax.fori_loop(0, 16, lambda it, a: bit_body(it, a, lambda t: count_truncated(_truncate_to_bf16(t))),
                        jnp.zeros((1, tile), I32))
    ans = lax.fori_loop(16, 32, lambda it, a: bit_body(it, a, lambda t: count(lambda s: s >= t)), ans)
    tau = _key_to_float(ans)
    need = topk - count(lambda s: s > tau)

    qs = [q_ref[h] for h in range(N_HEADS)]

    far_bias = [LOG2E * t5_ref[(T5_BUCKETS - 1) * N_BIAS_HEADS + head0 + h] for h in range(N_HEADS)]
    bounds = [_logit_bound(qs[h], kmax_ref[b * N_HEADS + h], bmax_ref[head0 + h]) for h in range(N_HEADS)]

    def run(online):
        def attend(j, ties_seen, kind):
            k0 = pl.multiple_of(j * tile, tile)
            sc = sc_ref[pl.ds(k0, tile), :]
            eq = sc == tau
            eqf = jnp.where(eq, 1.0, 0.0)
            rank = ties_seen + jnp.dot(tri_ref[...], eqf.astype(BF16), preferred_element_type=F32)
            wgt = jnp.where(eq, jnp.where(rank <= need, 1.0, 0.0), jnp.where(sc > tau, 1.0, 0.0))
            if kind == "diag":
                wgt = jnp.where(causal, wgt, 0.0)
            keep = wgt > 0.5
            kt = k_ref[pl.ds(k0, tile), :]
            logits = [jnp.dot(kt, qs[h], preferred_element_type=F32) for h in range(N_HEADS)]
            for h in range(N_HEADS):
                s, shift = logits[h], bounds[h]
                if kind != "far":
                    s = s + bias_ref[2 * h + (1 if kind == "near" else 0)]
                elif online:
                    s = s + far_bias[h]
                else:
                    shift = shift - far_bias[h]
                s = jnp.where(keep, s, NEG)
                _softmax_accumulate(s, v_ref[h, :, pl.ds(k0, tile)], acc_ref, h, shift=shift,
                                    m_ref=m_ref if online else None, keep=keep)
            return ties_seen + jnp.sum(eqf, axis=0, keepdims=True)

        seen = lax.fori_loop(0, jnp.maximum(i - 1, 0), lambda j, r: attend(j, r, "far"), jnp.zeros((1, tile), F32))
        seen = lax.cond(i >= 1, lambda r: attend(i - 1, r, "near"), lambda r: r, seen)
        attend(i, seen, "diag")

    _two_pass_attention(run, acc_ref, m_ref, N_HEADS)
    for h in range(N_HEADS):
        o_ref[h * HEAD_DIM:(h + 1) * HEAD_DIM, :] = _normalized(acc_ref, h).astype(o_ref.dtype)


def _dsa_attention(qi_t, w_t, k_idx, q_t, k, v_aug, t5_flat, bias_max, k_norm_max):
    bsz, _, width, seq = q_t.shape
    tile = min(SPARSE_TILE, seq)
    topk = min(DSA_TOPK_MAX, seq // 4)
    assert seq % tile == 0 and tile >= topk and tile >= T5_FAR_DIST
    kern = functools.partial(_dsa_kernel, tile=tile, head0=2 * N_HEADS, topk=topk)
    return pl.pallas_call(
        kern,
        grid=(bsz, seq // tile),
        in_specs=[_smem(), _smem(), _smem(),
                  pl.BlockSpec((None, qi_t.shape[1], tile), lambda b, i: (b, 0, i)),
                  pl.BlockSpec((None, 8, tile), lambda b, i: (b, 0, i)),
                  _resident((None, seq, k_idx.shape[2]), lambda b, i: (b, 0, 0)),
                  pl.BlockSpec((None, N_HEADS, width, tile), lambda b, i: (b, 0, 0, i)),
                  _resident((None, seq, width), lambda b, i: (b, 0, 0)),
                  _resident((None, N_HEADS, V_ROWS, seq), lambda b, i: (b, 0, 0, 0))],
        out_specs=pl.BlockSpec((None, width, tile), lambda b, i: (b, 0, i)),
        out_shape=jax.ShapeDtypeStruct((bsz, width, seq), BF16),
        scratch_shapes=[pltpu.VMEM((seq, tile), F32), pltpu.VMEM((seq, tile), BF16),
                        pltpu.VMEM((2 * N_HEADS, tile, tile), F32),
                        pltpu.VMEM((tile, tile), BF16),
                        pltpu.VMEM((N_HEADS, 1, tile), F32),
                        pltpu.VMEM((N_HEADS, V_ROWS, tile), F32)],
        compiler_params=_cparams(2),
        name="dsa_attention",
    )(t5_flat, bias_max, k_norm_max, qi_t, w_t, k_idx, q_t, k, v_aug)


def _outproj_kernel(x_ref, o_ref, w_ref, g_ref, y_ref):
    y_ref[...] = x_ref[...] + g_ref[...] * jnp.dot(o_ref[...], w_ref[...], preferred_element_type=F32)


def _output_projection(x, o, w_out, gate):
    bsz, seq, d = x.shape
    tm = ROW_TILE
    row = pl.BlockSpec((None, tm, d), lambda b, i: (b, i, 0))
    return pl.pallas_call(
        _outproj_kernel,
        grid=(bsz, seq // tm),
        in_specs=[row, pl.BlockSpec((None, tm, o.shape[2]), lambda b, i: (b, i, 0)),
                  _resident(w_out.shape, lambda b, i: (0, 0)),
                  pl.BlockSpec((None, 1, d), lambda b, i: (b, 0, 0))],
        out_specs=row,
        out_shape=jax.ShapeDtypeStruct(x.shape, F32),
        compiler_params=_cparams(2),
        name="output_projection",
    )(x, o, w_out.astype(BF16), gate)


HALO = 8


def _ffn_kernel(x_ref, xp_ref, gain_ref, sc_ref, sh_ref, g_ref, wg_ref, wv_ref, cw_ref, cb_ref, wd_ref, fg_ref,
                y_ref, h_ref, acc_ref, *, tm, n_chunks, final_norm):
    i = pl.program_id(1)
    c = pl.program_id(2)

    @pl.when(c == 0)
    def _():
        def modulated(x):
            return ((_rms(x) * gain_ref[...]) * (1.0 + sc_ref[...]) + sh_ref[...]).astype(BF16)
        h_ref[HALO:, :] = modulated(x_ref[...])
        halo = modulated(xp_ref[...])
        h_ref[:HALO, :] = jnp.where(i > 0, halo, jnp.zeros_like(halo))
        acc_ref[...] = jnp.zeros(acc_ref.shape, F32)

    h = h_ref[...]

    def conv(w_ref, half):
        u = jnp.dot(h, w_ref[...], preferred_element_type=F32)
        cw = cw_ref[half]
        return (cw[0:1] * u[HALO - 2:HALO - 2 + tm] + cw[1:2] * u[HALO - 1:HALO - 1 + tm]
                + cw[2:3] * u[HALO:HALO + tm]) + cb_ref[half]

    gate = conv(wg_ref, 0)
    val = conv(wv_ref, 1)
    a = (gate * (1.0 / (1.0 + jnp.exp(-gate)))) * val
    acc_ref[...] += jnp.dot(a.astype(BF16), wd_ref[...], preferred_element_type=F32)

    @pl.when(c == n_chunks - 1)
    def _():
        y = x_ref[...] + g_ref[...] * acc_ref[...]
        if final_norm:
            y = _rms(y) * fg_ref[...]
        y_ref[...] = y


def _ffn(x, gain, sc, sh, gate, w_up, conv_w, conv_b, w_down, final_gain, final_norm):
    bsz, seq, d = x.shape
    tm = min(FFN_ROW_TILE, seq)
    fc = FFN_CHUNK
    n_chunks = D_FF // fc
    w_up_b = w_up.astype(BF16)
    cw = conv_w.reshape(3, 2, D_FF).transpose(1, 0, 2)
    cb = conv_b.reshape(2, 1, D_FF)
    vec = pl.BlockSpec((None, 1, d), lambda b, i, c: (b, 0, 0))
    one = pl.BlockSpec((1, d), lambda b, i, c: (0, 0))
    row = pl.BlockSpec((None, tm, d), lambda b, i, c: (b, i, 0))
    return pl.pallas_call(
        functools.partial(_ffn_kernel, tm=tm, n_chunks=n_chunks, final_norm=final_norm),
        grid=(bsz, seq // tm, n_chunks),
        in_specs=[row,
                  pl.BlockSpec((None, HALO, d), lambda b, i, c: (b, jnp.maximum(i * (tm // HALO) - 1, 0), 0)),
                  one, vec, vec, vec,
                  pl.BlockSpec((d, fc), lambda b, i, c: (0, c)),
                  pl.BlockSpec((d, fc), lambda b, i, c: (0, n_chunks + c)),
                  pl.BlockSpec((2, 3, fc), lambda b, i, c: (0, 0, c)),
                  pl.BlockSpec((2, 1, fc), lambda b, i, c: (0, 0, c)),
                  pl.BlockSpec((fc, d), lambda b, i, c: (c, 0)),
                  one],
        out_specs=row,
        out_shape=jax.ShapeDtypeStruct(x.shape, F32),
        scratch_shapes=[pltpu.VMEM((HALO + tm, d), BF16), pltpu.VMEM((tm, d), F32)],
        compiler_params=_cparams(3),
        name="conv_glu_ffn",
    )(x, x, gain.reshape(1, d), sc, sh, gate, w_up_b, w_up_b, cw, cb, w_down.astype(BF16),
      final_gain.reshape(1, d))


def kernel(x, c, w_ada, b_ada, norm1_gain, w_in, f_bias, diff_lambda, diff_subln_gain, w_dq_up, w_didx_q,
           w_dkv_up, w_out, t5_table, norm2_gain, w_ffn_up, ffn_conv_w, ffn_conv_b, w_ffn_down, final_gain):
    bsz, seq, d = x.shape
    depth = w_ada.shape[0]
    t5_flat = t5_table.reshape(-1)
    bias_max = jnp.max(t5_table, axis=0) * LOG2E
    mod = _modulation(c, w_ada, b_ada)
    w = GROUP_W

    for l in range(depth):
        sh1, sc1, g1, sh2, sc2, g2 = [m[:, None, :] for m in jnp.split(mod[l], 6, axis=-1)]
        pa, pb, pc, gc, dq, dkv, misc = _input_projection(x, norm1_gain[l], sc1, sh1, w_in[l],
                                                          w_dq_up[l], w_didx_q[l], w_dkv_up[l])
        k_idx = misc[..., :IDX_DIM].astype(BF16)
        fc = misc[..., IDX_DIM:IDX_DIM + N_HEADS]
        w_idx = misc[..., IDX_DIM + N_HEADS:IDX_DIM + N_HEADS + IDX_HEADS]

        lambda_init = 0.8 - 0.6 * math.exp(-0.3 * l)
        lq1, lk1, lq2, lk2 = diff_lambda[l]
        lam = jnp.exp(jnp.sum(lq1 * lk1)) - jnp.exp(jnp.sum(lq2 * lk2)) + lambda_init
        qa_t = _heads_t(pa[..., :w]).reshape(bsz, N_HEADS, 2, DIFF_HALF, seq)
        zeros = jnp.zeros_like(qa_t)
        qa_t = jnp.stack([jnp.concatenate([qa_t[:, :, 0], zeros[:, :, 0]], axis=2),
                          jnp.concatenate([zeros[:, :, 1], qa_t[:, :, 1]], axis=2)], axis=2)
        o_a = _diff_attention(qa_t, _heads(pa[..., w:2 * w]), _values_aug(pa[..., 2 * w:]), lam,
                              diff_subln_gain[l], t5_flat, bias_max, _key_norm_max(pa[..., w:2 * w]),
                              1.0 - lambda_init)

        kb = _heads(pb[..., w:2 * w])
        o_b = _moba_attention(_heads_t(pb[..., :w]), kb, _values_aug(pb[..., 2 * w:]),
                              _moba_kmean(kb).astype(BF16), t5_flat, bias_max, _key_norm_max(pb[..., w:2 * w]))

        hi, mid, lo = _forget_cumsum(fc + f_bias[l])
        ones = jnp.ones_like(hi)
        cparts = jnp.stack([hi, mid, lo], axis=2)
        onep = jnp.stack([ones] * 3, axis=2)
        padw = 2 * HEAD_DIM - HEAD_DIM - 6
        qc_aug = jnp.concatenate([_heads_t(pc[..., :w]), onep.astype(BF16), cparts.astype(BF16),
                                  jnp.zeros((bsz, N_HEADS, padw, seq), BF16)], axis=2)
        kc_aug = jnp.concatenate([_heads(pc[..., w:2 * w]), (-cparts).transpose(0, 1, 3, 2).astype(BF16),
                                  onep.transpose(0, 1, 3, 2).astype(BF16),
                                  jnp.zeros((bsz, N_HEADS, seq, padw), BF16)], axis=3)
        o_c = _forgetting_attention(qc_aug, kc_aug, _values_aug(pc[..., 2 * w:]), _heads_t(gc),
                                    _key_norm_max(pc[..., w:2 * w]))

        qd_t = dq[..., :w].transpose(0, 2, 1)
        head_of_row = jnp.arange(w)[None, :, None] // HEAD_DIM
        qd_t = jnp.stack([jnp.where(head_of_row == h, qd_t, jnp.zeros((), BF16)) for h in range(N_HEADS)],
                         axis=1)
        qi_t = dq[..., w:].transpose(0, 2, 1)
        w_t = (w_idx * (IDX_HEADS ** -0.5 * IDX_DIM ** -0.5)).transpose(0, 2, 1)
        w_t = jnp.concatenate([w_t, jnp.zeros((bsz, 8 - IDX_HEADS, seq), F32)], axis=1)
        o_d = _dsa_attention(qi_t, w_t, k_idx, qd_t, dkv[..., :w], _values_aug(dkv[..., w:]), t5_flat, bias_max,
                             _key_norm_max(dkv[..., :w]))

        o = jnp.concatenate([_unheads_t(o_a), _unheads_t(o_b), _unheads_t(o_c), o_d.transpose(0, 2, 1)], axis=-1)
        x = _output_projection(x, o, w_out[l], g1)
        x = _ffn(x, norm2_gain[l], sc2, sh2, g2, w_ffn_up[l], ffn_conv_w[l], ffn_conv_b[l], w_ffn_down[l],
                 final_gain, final_norm=(l == depth - 1))
    return x
```

```python
import functools
import math

import numpy as np
import jax
import jax.numpy as jnp
from jax import lax
from jax.experimental import pallas as pl
from jax.experimental.pallas import tpu as pltpu

F32 = jnp.float32
BF16 = jnp.bfloat16
I32 = jnp.int32

HEAD_DIM = 64
N_HEADS = 4
GROUP_W = N_HEADS * HEAD_DIM
DIFF_HALF = HEAD_DIM // 2
MOBA_BLOCK = 256
MOBA_TOPK = 3
DSA_RANK = 128
IDX_HEADS = 4
IDX_DIM = 64
DSA_TOPK_MAX = 256
T5_BUCKETS = 32
T5_MAX_DIST = 128
N_BIAS_HEADS = 12
D_FF = 2816
EPS = 1e-6
NEG = -1e30
LOG2E = math.log2(math.e)
V_ROWS = HEAD_DIM + 16
BOUND_SLACK = 1.0 + 2.0 ** -6
MIN_DENOMINATOR = 2.0 ** -60
BF16_ROWS = 16
TAKE_ALL_TIES = 1e9

V7X_VMEM_LIMIT_BYTES = 56 * 1024 * 1024

ATTN_TILE = 512
SPARSE_TILE = 256
ROW_TILE = 512
FFN_ROW_TILE = 1024
FFN_CHUNK = 256
FORGET_HEADS_PER_STEP = 4
DIFF_HEADS_PER_STEP = 2
FORGET_BLOCKS_PER_STEP = 2


def _t5_thresholds():
    d = np.arange(0, 4 * T5_MAX_DIST)
    max_exact = T5_BUCKETS // 2
    ratio = np.maximum(d, 1).astype(np.float32) / max_exact
    large = max_exact + (np.log(ratio) / math.log(T5_MAX_DIST / max_exact) * (T5_BUCKETS - max_exact)).astype(np.int32)
    bucket = np.where(d < max_exact, d, np.minimum(large, T5_BUCKETS - 1))
    return [int(np.argmax(bucket >= b)) for b in range(T5_BUCKETS)]


T5_THRESH = _t5_thresholds()
T5_FAR_DIST = T5_THRESH[-1]


def _cparams(n_axes):
    return pltpu.CompilerParams(dimension_semantics=("arbitrary",) * n_axes,
                                vmem_limit_bytes=V7X_VMEM_LIMIT_BYTES)


def _resident(block_shape, index_map):
    return pl.BlockSpec(block_shape, index_map, pipeline_mode=pl.Buffered(1))


def _smem():
    return pl.BlockSpec(memory_space=pltpu.SMEM)


def _fill_t5_tile(bias_ref, slot, t5_ref, head, delta, tile):
    rows = 64

    def body(r, carry):
        r0 = pl.multiple_of(r * rows, rows)
        kr = lax.broadcasted_iota(I32, (rows, tile), 0) + r0
        qc = lax.broadcasted_iota(I32, (rows, tile), 1)
        d = delta + qc - kr
        val = jnp.full((rows, tile), LOG2E * t5_ref[head], F32)
        for b in range(1, T5_BUCKETS):
            val = jnp.where(d >= T5_THRESH[b], LOG2E * t5_ref[b * N_BIAS_HEADS + head], val)
        bias_ref[slot, pl.ds(r0, rows), :] = val
        return carry

    lax.fori_loop(0, tile // rows, body, 0)


def _unrolled_loop(n, unroll, body, carry):
    def group(g, c):
        for u in range(unroll):
            c = body(g * unroll + u, c)
        return c

    n_groups = n // unroll
    carry = lax.fori_loop(0, n_groups, group, carry)
    return lax.fori_loop(n_groups * unroll, n, body, carry)


def _causal_mask(tile):
    kr = lax.broadcasted_iota(I32, (tile, tile), 0)
    qc = lax.broadcasted_iota(I32, (tile, tile), 1)
    return kr <= qc


def _softmax_accumulate(s, v_aug, acc_ref, idx, *, shift=None, m_ref=None, keep=None):
    if m_ref is None:
        p = jnp.exp2(s - shift)
        acc_ref[idx] += jnp.dot(v_aug, p.astype(BF16), preferred_element_type=F32)
        return
    m_old = m_ref[idx]
    m_new = jnp.maximum(m_old, jnp.max(s, axis=0, keepdims=True))
    p = jnp.exp2(s - m_new)
    if keep is not None:
        p = jnp.where(keep, p, 0.0)
    acc_ref[idx] = jnp.exp2(m_old - m_new) * acc_ref[idx] + jnp.dot(v_aug, p.astype(BF16), preferred_element_type=F32)
    m_ref[idx] = m_new


def _logit_bound(q, k_norm_max, extra):
    qf = q.astype(F32)
    q_norm = jnp.sqrt(jnp.sum(qf * qf, axis=0, keepdims=True))
    return q_norm * (k_norm_max * BOUND_SLACK) + (extra + 1.0)


def _denominators_ok(acc_ref, n_chains):
    low = acc_ref[0, HEAD_DIM:HEAD_DIM + 1, :]
    for idx in range(1, n_chains):
        low = jnp.minimum(low, acc_ref[idx, HEAD_DIM:HEAD_DIM + 1, :])
    return jnp.min(low) > MIN_DENOMINATOR


def _two_pass_attention(run, acc_ref, m_ref, n_chains):
    acc_ref[...] = jnp.zeros(acc_ref.shape, F32)
    run(False)

    @pl.when(jnp.logical_not(_denominators_ok(acc_ref, n_chains)))
    def _():
        acc_ref[...] = jnp.zeros(acc_ref.shape, F32)
        m_ref[...] = jnp.full(m_ref.shape, NEG, F32)
        run(True)


def _normalized(acc_ref, idx):
    return acc_ref[idx, :HEAD_DIM, :] / acc_ref[idx, HEAD_DIM:HEAD_DIM + 1, :]


def _mod_kernel(c_ref, w_ref, b_ref, o_ref):
    c = c_ref[...]
    cond = c * (1.0 / (1.0 + jnp.exp(-c)))
    o_ref[...] = jnp.dot(cond.astype(BF16), w_ref[...].astype(BF16), preferred_element_type=F32) + b_ref[...]


def _modulation(c, w_ada, b_ada):
    depth, d, n = w_ada.shape
    bsz = c.shape[0]
    rows = 8
    c_pad = jnp.zeros((rows, d), F32).at[:bsz].set(c)
    tn = 1024
    out = pl.pallas_call(
        _mod_kernel,
        grid=(depth, n // tn),
        in_specs=[pl.BlockSpec((rows, d), lambda l, j: (0, 0)),
                  pl.BlockSpec((None, d, tn), lambda l, j: (l, 0, j)),
                  pl.BlockSpec((None, 1, tn), lambda l, j: (l, 0, j))],
        out_specs=pl.BlockSpec((None, rows, tn), lambda l, j: (l, 0, j)),
        out_shape=jax.ShapeDtypeStruct((depth, rows, n), F32),
        compiler_params=_cparams(2),
        name="adaln_modulation",
    )(c_pad, w_ada, b_ada.reshape(depth, 1, n))
    return out[:, :bsz]


def _rms(x):
    return x * lax.rsqrt(jnp.mean(x * x, axis=-1, keepdims=True) + EPS)


def _inproj_kernel(x_ref, gain_ref, sc_ref, sh_ref, w_ref, wq_ref, wkv_ref,
                   pa_ref, pb_ref, pc_ref, gc_ref, dq_ref, dkv_ref, misc_ref):
    x = x_ref[...]
    h = (_rms(x) * gain_ref[...]) * (1.0 + sc_ref[...]) + sh_ref[...]
    hb = h.astype(BF16)
    w3 = 3 * GROUP_W

    def proj(c0, c1):
        return jnp.dot(hb, w_ref[:, c0:c1], preferred_element_type=F32)

    pa_ref[...] = proj(0, w3).astype(BF16)
    pb_ref[...] = proj(w3, 2 * w3).astype(BF16)
    pc_ref[...] = proj(2 * w3, 3 * w3).astype(BF16)
    c0 = 3 * w3
    gc_ref[...] = proj(c0, c0 + GROUP_W)
    c0 += GROUP_W
    q_lat = _rms(proj(c0, c0 + DSA_RANK))
    kv_lat = _rms(proj(c0 + DSA_RANK, c0 + 2 * DSA_RANK))
    dq_ref[...] = jnp.dot(q_lat.astype(BF16), wq_ref[...], preferred_element_type=F32).astype(BF16)
    dkv_ref[...] = jnp.dot(kv_lat.astype(BF16), wkv_ref[...], preferred_element_type=F32).astype(BF16)
    c0 += 2 * DSA_RANK
    misc_ref[...] = proj(c0, c0 + 128)


def _input_projection(x, gain, sc, sh, w_in, w_dq_up, w_didx_q, w_dkv_up):
    bsz, seq, d = x.shape
    w3 = 3 * GROUP_W
    o_c = 2 * w3
    o_fc = o_c + w3
    o_gc = o_fc + N_HEADS
    o_ql = o_gc + GROUP_W
    o_kv = o_ql + DSA_RANK
    o_ki = o_kv + DSA_RANK
    o_wi = o_ki + IDX_DIM
    pad = jnp.zeros((d, 128 - IDX_DIM - N_HEADS - IDX_HEADS), w_in.dtype)
    col = jnp.arange(o_fc)
    is_q = (col % w3) < GROUP_W
    q_scale = jnp.where(is_q, jnp.where(col < w3, DIFF_HALF ** -0.5, HEAD_DIM ** -0.5) * LOG2E, 1.0).astype(F32)
    w = jnp.concatenate([w_in[:, :o_fc] * q_scale[None, :], w_in[:, o_gc:o_ql], w_in[:, o_ql:o_ki],
                         w_in[:, o_ki:o_wi], w_in[:, o_fc:o_gc], w_in[:, o_wi:], pad], axis=1).astype(BF16)
    n_cols = w.shape[1]
    wq = jnp.concatenate([w_dq_up * (HEAD_DIM ** -0.5 * LOG2E), w_didx_q], axis=1).astype(BF16)
    wkv = w_dkv_up.astype(BF16)
    tm = ROW_TILE
    row = lambda width: pl.BlockSpec((None, tm, width), lambda b, i: (b, i, 0))
    vec = pl.BlockSpec((None, 1, d), lambda b, i: (b, 0, 0))
    outs = pl.pallas_call(
        _inproj_kernel,
        grid=(bsz, seq // tm),
        in_specs=[row(d), pl.BlockSpec((1, d), lambda b, i: (0, 0)), vec, vec,
                  _resident((d, n_cols), lambda b, i: (0, 0)),
                  _resident(wq.shape, lambda b, i: (0, 0)),
                  _resident(wkv.shape, lambda b, i: (0, 0))],
        out_specs=[row(w3), row(w3), row(w3), row(GROUP_W), row(512), row(512), row(128)],
        out_shape=[jax.ShapeDtypeStruct((bsz, seq, w3), BF16)] * 3
                  + [jax.ShapeDtypeStruct((bsz, seq, GROUP_W), F32)]
                  + [jax.ShapeDtypeStruct((bsz, seq, 512), BF16)] * 2
                  + [jax.ShapeDtypeStruct((bsz, seq, 128), F32)],
        compiler_params=_cparams(2),
        name="input_projection",
    )(x, gain.reshape(1, d), sc, sh, w, wq, wkv)
    return outs


def _heads_t(t):
    bsz, seq, w = t.shape
    return t.reshape(bsz, seq, N_HEADS, w // N_HEADS).transpose(0, 2, 3, 1)


def _heads(t):
    bsz, seq, w = t.shape
    return t.reshape(bsz, seq, N_HEADS, w // N_HEADS).transpose(0, 2, 1, 3)


def _values_aug(v):
    v_t = _heads_t(v)
    bsz, nh, dh, seq = v_t.shape
    return jnp.concatenate([v_t, jnp.ones((bsz, nh, 1, seq), v.dtype),
                            jnp.zeros((bsz, nh, V_ROWS - dh - 1, seq), v.dtype)], axis=2)


def _key_norm_max(k):
    bsz, seq, w = k.shape
    kf = k.astype(F32).reshape(bsz, seq, N_HEADS, w // N_HEADS)
    return jnp.max(jnp.sqrt(jnp.sum(kf * kf, axis=-1)), axis=1).reshape(-1)


def _unheads_t(o_t):
    bsz, h, dh, seq = o_t.shape
    return o_t.transpose(0, 3, 1, 2).reshape(bsz, seq, h * dh)


def _diff_kernel(t5_ref, bmax_ref, kmax_ref, lam_ref, q_ref, k_ref, v_ref, g_ref, o_ref, bias_ref, m_ref, acc_ref,
                 *, tile, head0, heads, group_heads, out_scale):
    b = pl.program_id(0)
    hg = pl.program_id(1)
    i = pl.program_id(2)

    @pl.when(i == 0)
    def _():
        for h in range(heads):
            _fill_t5_tile(bias_ref, 2 * h, t5_ref, head0 + hg * heads + h, 0, tile)
            _fill_t5_tile(bias_ref, 2 * h + 1, t5_ref, head0 + hg * heads + h, tile, tile)

    far_bias = [LOG2E * t5_ref[(T5_BUCKETS - 1) * N_BIAS_HEADS + head0 + hg * heads + h] for h in range(heads)]
    bounds = [_logit_bound(q_ref[h, c], kmax_ref[b * group_heads + hg * heads + h], bmax_ref[head0 + hg * heads + h])
              for h in range(heads) for c in range(2)]

    def run(online):
        def step(j, kind):
            k0 = pl.multiple_of(j * tile, tile)
            logits = [jnp.dot(k_ref[h, pl.ds(k0, tile), :], q_ref[h, c], preferred_element_type=F32)
                      for h in range(heads) for c in range(2)]
            for h in range(heads):
                vt = v_ref[h, :, pl.ds(k0, tile)]
                for c in range(2):
                    idx = 2 * h + c
                    s, shift = logits[idx], bounds[idx]
                    if kind != "far":
                        s = s + bias_ref[2 * h + (1 if kind == "near" else 0)]
                    elif online:
                        s = s + far_bias[h]
                    else:
                        shift = shift - far_bias[h]
                    if kind == "diag":
                        s = jnp.where(_causal_mask(tile), s, NEG)
                    _softmax_accumulate(s, vt, acc_ref, idx, shift=shift, m_ref=m_ref if online else None)

        step(i, "diag")

        @pl.when(i >= 1)
        def _():
            step(i - 1, "near")

        def far(j, carry):
            step(j, "far")
            return carry

        lax.fori_loop(0, jnp.maximum(i - 1, 0), far, 0)

    _two_pass_attention(run, acc_ref, m_ref, 2 * heads)

    for h in range(heads):
        o = _normalized(acc_ref, 2 * h) - lam_ref[0] * _normalized(acc_ref, 2 * h + 1)
        y = o * lax.rsqrt(jnp.mean(o * o, axis=0, keepdims=True) + EPS)
        o_ref[h] = ((y * g_ref[...]) * out_scale).astype(o_ref.dtype)


def _diff_attention(q_t, k, v_aug, lam, subln_gain, t5_flat, bias_max, k_norm_max, out_scale):
    bsz, nh, _, dh, seq = q_t.shape
    tile = min(ATTN_TILE, seq)
    hps = DIFF_HEADS_PER_STEP
    assert tile >= T5_FAR_DIST and seq % tile == 0 and nh % hps == 0
    kern = functools.partial(_diff_kernel, tile=tile, head0=0, heads=hps, group_heads=nh, out_scale=out_scale)
    return pl.pallas_call(
        kern,
        grid=(bsz, nh // hps, seq // tile),
        in_specs=[_smem(), _smem(), _smem(), _smem(),
                  pl.BlockSpec((None, hps, 2, dh, tile), lambda b, h, i: (b, h, 0, 0, i)),
                  _resident((None, hps, seq, dh), lambda b, h, i: (b, h, 0, 0)),
                  _resident((None, hps, V_ROWS, seq), lambda b, h, i: (b, h, 0, 0)),
                  pl.BlockSpec((dh, 1), lambda b, h, i: (0, 0))],
        out_specs=pl.BlockSpec((None, hps, dh, tile), lambda b, h, i: (b, h, 0, i)),
        out_shape=jax.ShapeDtypeStruct((bsz, nh, dh, seq), BF16),
        scratch_shapes=[pltpu.VMEM((2 * hps, tile, tile), F32),
                        pltpu.VMEM((2 * hps, 1, tile), F32),
                        pltpu.VMEM((2 * hps, V_ROWS, tile), F32)],
        compiler_params=_cparams(3),
        name="diff_attention",
    )(t5_flat, bias_max, k_norm_max, lam.reshape(1), q_t, k, v_aug, subln_gain.reshape(dh, 1))


def _kmean_kernel(k_ref, o_ref, *, blocks):
    k = k_ref[...].astype(F32)
    o_ref[...] = jnp.mean(k.reshape(blocks, MOBA_BLOCK, k.shape[-1]), axis=1)


def _moba_kmean(k):
    bsz, nh, seq, dh = k.shape
    nb = seq // MOBA_BLOCK
    blocks = min(8, nb)
    return pl.pallas_call(
        functools.partial(_kmean_kernel, blocks=blocks),
        grid=(bsz, nh, nb // blocks),
        in_specs=[pl.BlockSpec((None, None, blocks * MOBA_BLOCK, dh), lambda b, h, i: (b, h, i, 0))],
        out_specs=pl.BlockSpec((None, None, blocks, dh), lambda b, h, i: (b, h, i, 0)),
        out_shape=jax.ShapeDtypeStruct((bsz, nh, nb, dh), F32),
        compiler_params=_cparams(3),
    )(k)


def _moba_kernel(t5_ref, bmax_ref, kmax_ref, q_ref, k_ref, v_ref, km_ref, o_ref, bias_ref, sel_ref, m_ref, acc_ref,
                 *, tile, head0, heads, n_blocks):
    b = pl.program_id(0)
    i = pl.program_id(1)

    @pl.when(i == 0)
    def _():
        for h in range(heads):
            _fill_t5_tile(bias_ref, 2 * h, t5_ref, head0 + h, 0, tile)
            _fill_t5_tile(bias_ref, 2 * h + 1, t5_ref, head0 + h, tile, tile)

    far_bias = [LOG2E * t5_ref[(T5_BUCKETS - 1) * N_BIAS_HEADS + head0 + h] for h in range(heads)]
    bounds = [_logit_bound(q_ref[h], kmax_ref[b * heads + h], bmax_ref[head0 + h]) for h in range(heads)]

    nidx = lax.broadcasted_iota(I32, (n_blocks, tile), 0)
    for h in range(heads):
        gate = jnp.dot(km_ref[h], q_ref[h], preferred_element_type=F32)
        g = jnp.where(nidx < i, gate, NEG)
        sel = jnp.zeros(gate.shape, F32)
        for _ in range(min(MOBA_TOPK, n_blocks)):
            mx = jnp.max(g, axis=0, keepdims=True)
            first = jnp.min(jnp.where(g == mx, nidx, n_blocks), axis=0, keepdims=True)
            pick = nidx == first
            sel = jnp.where(pick, 1.0, sel)
            g = jnp.where(pick, -jnp.inf, g)
        sel_ref[h] = jnp.where(nidx < i, sel, 0.0)

    def run(online):
        def step(j, kind):
            k0 = pl.multiple_of(j * tile, tile)
            logits = [jnp.dot(k_ref[h, pl.ds(k0, tile), :], q_ref[h], preferred_element_type=F32)
                      for h in range(heads)]
            for h in range(heads):
                s, shift = logits[h], bounds[h]
                if kind != "far":
                    s = s + bias_ref[2 * h + (1 if kind == "near" else 0)]
                elif online:
                    s = s + far_bias[h]
                else:
                    shift = shift - far_bias[h]
                mask = _causal_mask(tile) if kind == "diag" else sel_ref[h, pl.ds(j, 1), :] > 0.5
                s = jnp.where(mask, s, NEG)
                _softmax_accumulate(s, v_ref[h, :, pl.ds(k0, tile)], acc_ref, h, shift=shift,
                                    m_ref=m_ref if online else None)

        step(i, "diag")

        @pl.when(i >= 1)
        def _():
            step(i - 1, "near")

        def far(j, carry):
            step(j, "far")
            return carry

        _unrolled_loop(jnp.maximum(i - 1, 0), 2, far, 0)

    _two_pass_attention(run, acc_ref, m_ref, heads)
    for h in range(heads):
        o_ref[h] = _normalized(acc_ref, h).astype(o_ref.dtype)


def _moba_attention(q_t, k, v_aug, kmean, t5_flat, bias_max, k_norm_max):
    bsz, nh, dh, seq = q_t.shape
    tile = MOBA_BLOCK
    assert SPARSE_TILE == MOBA_BLOCK and seq % tile == 0 and tile >= T5_FAR_DIST
    nb = seq // tile
    kern = functools.partial(_moba_kernel, tile=tile, head0=N_HEADS, heads=nh, n_blocks=nb)
    return pl.pallas_call(
        kern,
        grid=(bsz, nb),
        in_specs=[_smem(), _smem(), _smem(),
                  pl.BlockSpec((None, nh, dh, tile), lambda b, i: (b, 0, 0, i)),
                  _resident((None, nh, seq, dh), lambda b, i: (b, 0, 0, 0)),
                  _resident((None, nh, V_ROWS, seq), lambda b, i: (b, 0, 0, 0)),
                  _resident((None, nh, nb, dh), lambda b, i: (b, 0, 0, 0))],
        out_specs=pl.BlockSpec((None, nh, dh, tile), lambda b, i: (b, 0, 0, i)),
        out_shape=jax.ShapeDtypeStruct((bsz, nh, dh, seq), BF16),
        scratch_shapes=[pltpu.VMEM((2 * nh, tile, tile), F32), pltpu.VMEM((nh, nb, tile), F32),
                        pltpu.VMEM((nh, 1, tile), F32),
                        pltpu.VMEM((nh, V_ROWS, tile), F32)],
        compiler_params=_cparams(2),
        name="moba_attention",
    )(t5_flat, bias_max, k_norm_max, q_t, k, v_aug, kmean)


def _split3(x):
    def trunc(v):
        bits = lax.bitcast_convert_type(v, I32)
        return lax.bitcast_convert_type(bits & jnp.int32(-65536), F32)
    hi = trunc(x)
    r1 = x - hi
    mid = trunc(r1)
    lo = trunc(r1 - mid)
    return hi, mid, lo


def _forget_cumsum_kernel(f_ref, hi_ref, mid_ref, lo_ref):
    x = f_ref[...]
    rows = x.shape[0]
    ls = jnp.minimum(x, 0.0) - jnp.log(1.0 + jnp.exp(-jnp.abs(x)))
    upper = (lax.broadcasted_iota(I32, (128, 128), 0) <= lax.broadcasted_iota(I32, (128, 128), 1)).astype(F32)
    within = jnp.dot(ls, upper, preferred_element_type=F32, precision=lax.Precision.HIGHEST)
    strict = (lax.broadcasted_iota(I32, (rows, rows), 1) < lax.broadcasted_iota(I32, (rows, rows), 0)).astype(F32)
    before = jnp.dot(strict, within, preferred_element_type=F32, precision=lax.Precision.HIGHEST)
    cf = (within + before[:, 127:128]) * LOG2E
    hi, mid, lo = _split3(cf)
    hi_ref[...] = hi
    mid_ref[...] = mid
    lo_ref[...] = lo


def _forget_cumsum(f_logit):
    bsz, seq, nh = f_logit.shape
    rows = seq // 128
    f = f_logit.transpose(0, 2, 1).reshape(bsz, nh, rows, 128)
    spec = pl.BlockSpec((None, None, rows, 128), lambda b, h: (b, h, 0, 0))
    parts = pl.pallas_call(
        _forget_cumsum_kernel,
        grid=(bsz, nh),
        in_specs=[spec],
        out_specs=[spec] * 3,
        out_shape=[jax.ShapeDtypeStruct((bsz, nh, rows, 128), F32)] * 3,
        compiler_params=_cparams(2),
        name="forget_cumsum",
    )(f)
    return [p.reshape(bsz, nh, seq) for p in parts]


def _forget_kernel(kmax_ref, q_ref, k_ref, v_ref, gate_ref, o_ref, m_ref, acc_ref, *, tile, heads, group_heads):
    b = pl.program_id(0)
    hg = pl.program_id(1)
    i = pl.program_id(2)
    bounds = [_logit_bound(q_ref[h, :HEAD_DIM, :], kmax_ref[b * group_heads + hg * heads + h], 0.0)
              for h in range(heads)]

    def run(online):
        def step(j, blocks, mask):
            k0 = pl.multiple_of(j * tile, tile)
            logits = [jnp.dot(k_ref[h, pl.ds(k0, blocks * tile), :], q_ref[h], preferred_element_type=F32)
                      for h in range(heads)]
            for h in range(heads):
                s = logits[h] if mask is None else jnp.where(mask, logits[h], NEG)
                _softmax_accumulate(s, v_ref[h, :, pl.ds(k0, blocks * tile)], acc_ref, h, shift=bounds[h],
                                    m_ref=m_ref if online else None)

        step(i, 1, _causal_mask(tile))
        n_big = i // FORGET_BLOCKS_PER_STEP

        def past_big(j, carry):
            step(j * FORGET_BLOCKS_PER_STEP, FORGET_BLOCKS_PER_STEP, None)
            return carry

        def past_single(j, carry):
            step(j, 1, None)
            return carry

        lax.fori_loop(0, n_big, past_big, 0)
        lax.fori_loop(n_big * FORGET_BLOCKS_PER_STEP, i, past_single, 0)

    _two_pass_attention(run, acc_ref, m_ref, heads)
    for h in range(heads):
        g = gate_ref[h]
        o_ref[h] = (_normalized(acc_ref, h) * (1.0 / (1.0 + jnp.exp(-g)))).astype(o_ref.dtype)


def _forgetting_attention(q_aug_t, k_aug, v_aug, gate_t, k_norm_max):
    bsz, nh, kdim, seq = q_aug_t.shape
    dh = gate_t.shape[2]
    tile = min(ATTN_TILE, seq)
    hps = FORGET_HEADS_PER_STEP
    assert seq % tile == 0 and nh % hps == 0
    return pl.pallas_call(
        functools.partial(_forget_kernel, tile=tile, heads=hps, group_heads=nh),
        grid=(bsz, nh // hps, seq // tile),
        in_specs=[_smem(),
                  pl.BlockSpec((None, hps, kdim, tile), lambda b, h, i: (b, h, 0, i)),
                  _resident((None, hps, seq, kdim), lambda b, h, i: (b, h, 0, 0)),
                  _resident((None, hps, V_ROWS, seq), lambda b, h, i: (b, h, 0, 0)),
                  pl.BlockSpec((None, hps, dh, tile), lambda b, h, i: (b, h, 0, i))],
        out_specs=pl.BlockSpec((None, hps, dh, tile), lambda b, h, i: (b, h, 0, i)),
        out_shape=jax.ShapeDtypeStruct((bsz, nh, dh, seq), BF16),
        scratch_shapes=[pltpu.VMEM((hps, 1, tile), F32),
                        pltpu.VMEM((hps, V_ROWS, tile), F32)],
        compiler_params=_cparams(3),
        name="forget_attention",
    )(k_norm_max, q_aug_t, k_aug, v_aug, gate_t)


def _key_to_float(u):
    ks = u ^ jnp.int32(-2147483648)
    bits = jnp.where(ks < 0, ks ^ jnp.int32(2147483647), ks)
    return lax.bitcast_convert_type(bits, F32)


def _truncate_to_bf16(x):
    bits = lax.bitcast_convert_type(x, I32) & jnp.int32(-65536)
    return lax.bitcast_convert_type(bits, F32).astype(BF16)


def _dsa_kernel(t5_ref, bmax_ref, kmax_ref, qi_ref, w_ref, ki_ref, q_ref, k_ref, v_ref, o_ref,
                sc_ref, hi_ref, bias_ref, tri_ref, m_ref, acc_ref, *, tile, head0, topk):
    b = pl.program_id(0)
    i = pl.program_id(1)

    @pl.when((b == 0) & (i == 0))
    def _():
        for h in range(N_HEADS):
            _fill_t5_tile(bias_ref, 2 * h, t5_ref, head0 + h, 0, tile)
            _fill_t5_tile(bias_ref, 2 * h + 1, t5_ref, head0 + h, tile, tile)
        tri_ref[...] = jnp.where(lax.broadcasted_iota(I32, (tile, tile), 1) <= lax.broadcasted_iota(I32, (tile, tile), 0),
                                 1.0, 0.0).astype(BF16)

    causal = _causal_mask(tile)

    def index_scores(j):
        k0 = pl.multiple_of(j * tile, tile)
        kt = ki_ref[pl.ds(k0, tile), :]
        raws = [jnp.dot(kt, qi_ref[h * IDX_DIM:(h + 1) * IDX_DIM, :], preferred_element_type=F32)
                for h in range(IDX_HEADS)]
        sc = jnp.zeros((tile, tile), F32)
        for h in range(IDX_HEADS):
            sc = sc + jnp.maximum(raws[h], 0.0) * w_ref[h:h + 1, :]
        return k0, sc

    def store_scores(k0, sc):
        sc_ref[pl.ds(k0, tile), :] = sc
        hi_ref[pl.ds(k0, tile), :] = _truncate_to_bf16(sc)

    def score_body(j, carry):
        store_scores(*index_scores(j))
        return carry

    _unrolled_loop(i, 2, score_body, 0)
    k0, sc = index_scores(i)
    store_scores(k0, jnp.where(causal, sc, NEG))

    def count(pred):
        def body(j, cnt):
            k0 = pl.multiple_of(j * tile, tile)
            ind = jnp.where(pred(sc_ref[pl.ds(k0, tile), :]), 1.0, 0.0)
            return cnt + jnp.sum(ind.reshape(tile // 8, 8, tile), axis=0)
        cnt = _unrolled_loop(i + 1, 2, body, jnp.zeros((8, tile), F32))
        return jnp.sum(cnt, axis=0, keepdims=True)

    def count_truncated(thr):
        def body(j, cnt):
            k0 = pl.multiple_of(j * tile, tile)
            ind = jnp.where(hi_ref[pl.ds(k0, tile), :] >= thr, jnp.ones((), BF16), jnp.zeros((), BF16))
            part = ind[0:BF16_ROWS]
            for r in range(1, tile // BF16_ROWS):
                part = part + ind[r * BF16_ROWS:(r + 1) * BF16_ROWS]
            return cnt + part.astype(F32)
        cnt = _unrolled_loop(i + 1, 4, body, jnp.zeros((BF16_ROWS, tile), F32))
        return jnp.sum(cnt, axis=0, keepdims=True)

    def high_bit(it, ans):
        cand = ans | jnp.left_shift(jnp.int32(1), 31 - it)
        return jnp.where(count_truncated(_truncate_to_bf16(_key_to_float(cand))) >= topk, cand, ans)

    ans = lax.fori_loop(0, 16, high_bit, jnp.zeros((1, tile), I32))

    above = count(lambda s: s > _key_to_float(ans))
    settled = above < topk
    need = jnp.where(settled, topk - above, 0.0)

    def low_bit(state):
        it, ans, open_f, need = state
        cand = ans | jnp.left_shift(jnp.int32(1), 31 - it)
        thr = _key_to_float(cand)
        cnt = count(lambda s: s >= thr)
        unsettled = open_f > 0.5
        ans = jnp.where(unsettled & (cnt >= topk), cand, ans)
        exact = unsettled & (cnt == topk)
        return it + 1, ans, jnp.where(exact, 0.0, open_f), jnp.where(exact, TAKE_ALL_TIES, need)

    _, ans, open_f, need = lax.while_loop(lambda st: (st[0] < 32) & (jnp.max(st[2]) > 0.5), low_bit,
                                          (jnp.int32(16), ans, jnp.where(settled, 0.0, 1.0), need))
    tau = _key_to_float(ans)
    need = jnp.where(open_f > 0.5, topk - count(lambda s: s > tau), need)

    qs = [q_ref[h] for h in range(N_HEADS)]

    far_bias = [LOG2E * t5_ref[(T5_BUCKETS - 1) * N_BIAS_HEADS + head0 + h] for h in range(N_HEADS)]
    bounds = [_logit_bound(qs[h], kmax_ref[b * N_HEADS + h], bmax_ref[head0 + h]) for h in range(N_HEADS)]

    def run(online):
        def attend(j, ties_seen, kind):
            k0 = pl.multiple_of(j * tile, tile)
            sc = sc_ref[pl.ds(k0, tile), :]
            eq = sc == tau
            eqf = jnp.where(eq, 1.0, 0.0)
            rank = ties_seen + jnp.dot(tri_ref[...], eqf.astype(BF16), preferred_element_type=F32)
            wgt = jnp.where(eq, jnp.where(rank <= need, 1.0, 0.0), jnp.where(sc > tau, 1.0, 0.0))
            if kind == "diag":
                wgt = jnp.where(causal, wgt, 0.0)
            keep = wgt > 0.5
            kt = k_ref[pl.ds(k0, tile), :]
            logits = [jnp.dot(kt, qs[h], preferred_element_type=F32) for h in range(N_HEADS)]
            for h in range(N_HEADS):
                s, shift = logits[h], bounds[h]
                if kind != "far":
                    s = s + bias_ref[2 * h + (1 if kind == "near" else 0)]
                elif online:
                    s = s + far_bias[h]
                else:
                    shift = shift - far_bias[h]
                s = jnp.where(keep, s, NEG)
                _softmax_accumulate(s, v_ref[h, :, pl.ds(k0, tile)], acc_ref, h, shift=shift,
                                    m_ref=m_ref if online else None, keep=keep)
            return ties_seen + jnp.sum(eqf, axis=0, keepdims=True)

        seen = _unrolled_loop(jnp.maximum(i - 1, 0), 2, lambda j, r: attend(j, r, "far"), jnp.zeros((1, tile), F32))
        seen = lax.cond(i >= 1, lambda r: attend(i - 1, r, "near"), lambda r: r, seen)
        attend(i, seen, "diag")

    _two_pass_attention(run, acc_ref, m_ref, N_HEADS)
    for h in range(N_HEADS):
        o_ref[h * HEAD_DIM:(h + 1) * HEAD_DIM, :] = _normalized(acc_ref, h).astype(o_ref.dtype)


def _dsa_attention(qi_t, w_t, k_idx, q_t, k, v_aug, t5_flat, bias_max, k_norm_max):
    bsz, _, width, seq = q_t.shape
    tile = min(SPARSE_TILE, seq)
    topk = min(DSA_TOPK_MAX, seq // 4)
    assert seq % tile == 0 and tile >= topk and tile >= T5_FAR_DIST
    kern = functools.partial(_dsa_kernel, tile=tile, head0=2 * N_HEADS, topk=topk)
    return pl.pallas_call(
        kern,
        grid=(bsz, seq // tile),
        in_specs=[_smem(), _smem(), _smem(),
                  pl.BlockSpec((None, qi_t.shape[1], tile), lambda b, i: (b, 0, i)),
                  pl.BlockSpec((None, 8, tile), lambda b, i: (b, 0, i)),
                  _resident((None, seq, k_idx.shape[2]), lambda b, i: (b, 0, 0)),
                  pl.BlockSpec((None, N_HEADS, width, tile), lambda b, i: (b, 0, 0, i)),
                  _resident((None, seq, width), lambda b, i: (b, 0, 0)),
                  _resident((None, N_HEADS, V_ROWS, seq), lambda b, i: (b, 0, 0, 0))],
        out_specs=pl.BlockSpec((None, width, tile), lambda b, i: (b, 0, i)),
        out_shape=jax.ShapeDtypeStruct((bsz, width, seq), BF16),
        scratch_shapes=[pltpu.VMEM((seq, tile), F32), pltpu.VMEM((seq, tile), BF16),
                        pltpu.VMEM((2 * N_HEADS, tile, tile), F32),
                        pltpu.VMEM((tile, tile), BF16),
                        pltpu.VMEM((N_HEADS, 1, tile), F32),
                        pltpu.VMEM((N_HEADS, V_ROWS, tile), F32)],
        compiler_params=_cparams(2),
        name="dsa_attention",
    )(t5_flat, bias_max, k_norm_max, qi_t, w_t, k_idx, q_t, k, v_aug)


def _outproj_kernel(x_ref, o_ref, w_ref, g_ref, y_ref):
    y_ref[...] = x_ref[...] + g_ref[...] * jnp.dot(o_ref[...], w_ref[...], preferred_element_type=F32)


def _output_projection(x, o, w_out, gate):
    bsz, seq, d = x.shape
    tm = ROW_TILE
    row = pl.BlockSpec((None, tm, d), lambda b, i: (b, i, 0))
    return pl.pallas_call(
        _outproj_kernel,
        grid=(bsz, seq // tm),
        in_specs=[row, pl.BlockSpec((None, tm, o.shape[2]), lambda b, i: (b, i, 0)),
                  _resident(w_out.shape, lambda b, i: (0, 0)),
                  pl.BlockSpec((None, 1, d), lambda b, i: (b, 0, 0))],
        out_specs=row,
        out_shape=jax.ShapeDtypeStruct(x.shape, F32),
        compiler_params=_cparams(2),
        name="output_projection",
    )(x, o, w_out.astype(BF16), gate)


HALO = 8


def _ffn_kernel(x_ref, xp_ref, gain_ref, sc_ref, sh_ref, g_ref, wg_ref, wv_ref, cw_ref, cb_ref, wd_ref, fg_ref,
                y_ref, h_ref, acc_ref, *, tm, n_chunks, final_norm):
    i = pl.program_id(1)
    c = pl.program_id(2)

    @pl.when(c == 0)
    def _():
        def modulated(x):
            return ((_rms(x) * gain_ref[...]) * (1.0 + sc_ref[...]) + sh_ref[...]).astype(BF16)
        h_ref[HALO:, :] = modulated(x_ref[...])
        halo = modulated(xp_ref[...])
        h_ref[:HALO, :] = jnp.where(i > 0, halo, jnp.zeros_like(halo))
        acc_ref[...] = jnp.zeros(acc_ref.shape, F32)

    h = h_ref[...]

    def conv(w_ref, half):
        u = jnp.dot(h, w_ref[...], preferred_element_type=F32)
        cw = cw_ref[half]
        return (cw[0:1] * u[HALO - 2:HALO - 2 + tm] + cw[1:2] * u[HALO - 1:HALO - 1 + tm]
                + cw[2:3] * u[HALO:HALO + tm]) + cb_ref[half]

    gate = conv(wg_ref, 0)
    val = conv(wv_ref, 1)
    a = (gate * (1.0 / (1.0 + jnp.exp(-gate)))) * val
    acc_ref[...] += jnp.dot(a.astype(BF16), wd_ref[...], preferred_element_type=F32)

    @pl.when(c == n_chunks - 1)
    def _():
        y = x_ref[...] + g_ref[...] * acc_ref[...]
        if final_norm:
            y = _rms(y) * fg_ref[...]
        y_ref[...] = y


def _ffn(x, gain, sc, sh, gate, w_up, conv_w, conv_b, w_down, final_gain, final_norm):
    bsz, seq, d = x.shape
    tm = min(FFN_ROW_TILE, seq)
    fc = FFN_CHUNK
    n_chunks = D_FF // fc
    w_up_b = w_up.astype(BF16)
    cw = conv_w.reshape(3, 2, D_FF).transpose(1, 0, 2)
    cb = conv_b.reshape(2, 1, D_FF)
    vec = pl.BlockSpec((None, 1, d), lambda b, i, c: (b, 0, 0))
    one = pl.BlockSpec((1, d), lambda b, i, c: (0, 0))
    row = pl.BlockSpec((None, tm, d), lambda b, i, c: (b, i, 0))
    return pl.pallas_call(
        functools.partial(_ffn_kernel, tm=tm, n_chunks=n_chunks, final_norm=final_norm),
        grid=(bsz, seq // tm, n_chunks),
        in_specs=[row,
                  pl.BlockSpec((None, HALO, d), lambda b, i, c: (b, jnp.maximum(i * (tm // HALO) - 1, 0), 0)),
                  one, vec, vec, vec,
                  pl.BlockSpec((d, fc), lambda b, i, c: (0, c)),
                  pl.BlockSpec((d, fc), lambda b, i, c: (0, n_chunks + c)),
                  pl.BlockSpec((2, 3, fc), lambda b, i, c: (0, 0, c)),
                  pl.BlockSpec((2, 1, fc), lambda b, i, c: (0, 0, c)),
                  pl.BlockSpec((fc, d), lambda b, i, c: (c, 0)),
                  one],
        out_specs=row,
        out_shape=jax.ShapeDtypeStruct(x.shape, F32),
        scratch_shapes=[pltpu.VMEM((HALO + tm, d), BF16), pltpu.VMEM((tm, d), F32)],
        compiler_params=_cparams(3),
        name="conv_glu_ffn",
    )(x, x, gain.reshape(1, d), sc, sh, gate, w_up_b, w_up_b, cw, cb, w_down.astype(BF16),
      final_gain.reshape(1, d))


def kernel(x, c, w_ada, b_ada, norm1_gain, w_in, f_bias, diff_lambda, diff_subln_gain, w_dq_up, w_didx_q,
           w_dkv_up, w_out, t5_table, norm2_gain, w_ffn_up, ffn_conv_w, ffn_conv_b, w_ffn_down, final_gain):
    bsz, seq, d = x.shape
    depth = w_ada.shape[0]
    t5_flat = t5_table.reshape(-1)
    bias_max = jnp.max(t5_table, axis=0) * LOG2E
    mod = _modulation(c, w_ada, b_ada)
    w = GROUP_W

    for l in range(depth):
        sh1, sc1, g1, sh2, sc2, g2 = [m[:, None, :] for m in jnp.split(mod[l], 6, axis=-1)]
        pa, pb, pc, gc, dq, dkv, misc = _input_projection(x, norm1_gain[l], sc1, sh1, w_in[l],
                                                          w_dq_up[l], w_didx_q[l], w_dkv_up[l])
        k_idx = misc[..., :IDX_DIM].astype(BF16)
        fc = misc[..., IDX_DIM:IDX_DIM + N_HEADS]
        w_idx = misc[..., IDX_DIM + N_HEADS:IDX_DIM + N_HEADS + IDX_HEADS]

        lambda_init = 0.8 - 0.6 * math.exp(-0.3 * l)
        lq1, lk1, lq2, lk2 = diff_lambda[l]
        lam = jnp.exp(jnp.sum(lq1 * lk1)) - jnp.exp(jnp.sum(lq2 * lk2)) + lambda_init
        qa_t = _heads_t(pa[..., :w]).reshape(bsz, N_HEADS, 2, DIFF_HALF, seq)
        zeros = jnp.zeros_like(qa_t)
        qa_t = jnp.stack([jnp.concatenate([qa_t[:, :, 0], zeros[:, :, 0]], axis=2),
                          jnp.concatenate([zeros[:, :, 1], qa_t[:, :, 1]], axis=2)], axis=2)
        o_a = _diff_attention(qa_t, _heads(pa[..., w:2 * w]), _values_aug(pa[..., 2 * w:]), lam,
                              diff_subln_gain[l], t5_flat, bias_max, _key_norm_max(pa[..., w:2 * w]),
                              1.0 - lambda_init)

        kb = _heads(pb[..., w:2 * w])
        o_b = _moba_attention(_heads_t(pb[..., :w]), kb, _values_aug(pb[..., 2 * w:]),
                              _moba_kmean(kb).astype(BF16), t5_flat, bias_max, _key_norm_max(pb[..., w:2 * w]))

        hi, mid, lo = _forget_cumsum(fc + f_bias[l])
        ones = jnp.ones_like(hi)
        cparts = jnp.stack([hi, mid, lo], axis=2)
        onep = jnp.stack([ones] * 3, axis=2)
        padw = 2 * HEAD_DIM - HEAD_DIM - 6
        qc_aug = jnp.concatenate([_heads_t(pc[..., :w]), onep.astype(BF16), cparts.astype(BF16),
                                  jnp.zeros((bsz, N_HEADS, padw, seq), BF16)], axis=2)
        kc_aug = jnp.concatenate([_heads(pc[..., w:2 * w]), (-cparts).transpose(0, 1, 3, 2).astype(BF16),
                                  onep.transpose(0, 1, 3, 2).astype(BF16),
                                  jnp.zeros((bsz, N_HEADS, seq, padw), BF16)], axis=3)
        o_c = _forgetting_attention(qc_aug, kc_aug, _values_aug(pc[..., 2 * w:]), _heads_t(gc),
                                    _key_norm_max(pc[..., w:2 * w]))

        qd_t = dq[..., :w].transpose(0, 2, 1)
        head_of_row = jnp.arange(w)[None, :, None] // HEAD_DIM
        qd_t = jnp.stack([jnp.where(head_of_row == h, qd_t, jnp.zeros((), BF16)) for h in range(N_HEADS)],
                         axis=1)
        qi_t = dq[..., w:].transpose(0, 2, 1)
        w_t = (w_idx * (IDX_HEADS ** -0.5 * IDX_DIM ** -0.5)).transpose(0, 2, 1)
        w_t = jnp.concatenate([w_t, jnp.zeros((bsz, 8 - IDX_HEADS, seq), F32)], axis=1)
        o_d = _dsa_attention(qi_t, w_t, k_idx, qd_t, dkv[..., :w], _values_aug(dkv[..., w:]), t5_flat, bias_max,
                             _key_norm_max(dkv[..., :w]))

        o = jnp.concatenate([_unheads_t(o_a), _unheads_t(o_b), _unheads_t(o_c), o_d.transpose(0, 2, 1)], axis=-1)
        x = _output_projection(x, o, w_out[l], g1)
        x = _ffn(x, norm2_gain[l], sc2, sh2, g2, w_ffn_up[l], ffn_conv_w[l], ffn_conv_b[l], w_ffn_down[l],
                 final_gain, final_norm=(l == depth - 1))
    return x
```

```python
import functools
import math

import numpy as np
import jax
import jax.numpy as jnp
from jax import lax
from jax.experimental import pallas as pl
from jax.experimental.pallas import tpu as pltpu

F32 = jnp.float32
BF16 = jnp.bfloat16
I32 = jnp.int32

HEAD_DIM = 64
N_HEADS = 4
GROUP_W = N_HEADS * HEAD_DIM
DIFF_HALF = HEAD_DIM // 2
MOBA_BLOCK = 256
MOBA_TOPK = 3
DSA_RANK = 128
IDX_HEADS = 4
IDX_DIM = 64
DSA_TOPK_MAX = 256
T5_BUCKETS = 32
T5_MAX_DIST = 128
N_BIAS_HEADS = 12
D_FF = 2816
EPS = 1e-6
NEG = -1e30
LOG2E = math.log2(math.e)
V_ROWS = HEAD_DIM + 16
BOUND_SLACK = 1.0 + 2.0 ** -6
MIN_DENOMINATOR = 2.0 ** -60
BF16_ROWS = 16
TAKE_ALL_TIES = 1e9

V7X_VMEM_LIMIT_BYTES = 56 * 1024 * 1024

ATTN_TILE = 512
SPARSE_TILE = 256
ROW_TILE = 512
FFN_ROW_TILE = 1024
FFN_CHUNK = 256
FORGET_HEADS_PER_STEP = 4
DIFF_HEADS_PER_STEP = 2
FORGET_BLOCKS_PER_STEP = 2


def _t5_thresholds():
    d = np.arange(0, 4 * T5_MAX_DIST)
    max_exact = T5_BUCKETS // 2
    ratio = np.maximum(d, 1).astype(np.float32) / max_exact
    large = max_exact + (np.log(ratio) / math.log(T5_MAX_DIST / max_exact) * (T5_BUCKETS - max_exact)).astype(np.int32)
    bucket = np.where(d < max_exact, d, np.minimum(large, T5_BUCKETS - 1))
    return [int(np.argmax(bucket >= b)) for b in range(T5_BUCKETS)]


T5_THRESH = _t5_thresholds()
T5_FAR_DIST = T5_THRESH[-1]


def _cparams(n_axes):
    return pltpu.CompilerParams(dimension_semantics=("arbitrary",) * n_axes,
                                vmem_limit_bytes=V7X_VMEM_LIMIT_BYTES)


def _resident(block_shape, index_map):
    return pl.BlockSpec(block_shape, index_map, pipeline_mode=pl.Buffered(1))


def _smem():
    return pl.BlockSpec(memory_space=pltpu.SMEM)


def _fill_t5_tile(bias_ref, slot, t5_ref, head, delta, tile):
    rows = 64

    def body(r, carry):
        r0 = pl.multiple_of(r * rows, rows)
        kr = lax.broadcasted_iota(I32, (rows, tile), 0) + r0
        qc = lax.broadcasted_iota(I32, (rows, tile), 1)
        d = delta + qc - kr
        val = jnp.full((rows, tile), LOG2E * t5_ref[head], F32)
        for b in range(1, T5_BUCKETS):
            val = jnp.where(d >= T5_THRESH[b], LOG2E * t5_ref[b * N_BIAS_HEADS + head], val)
        bias_ref[slot, pl.ds(r0, rows), :] = val
        return carry

    lax.fori_loop(0, tile // rows, body, 0)


def _unrolled_loop(n, unroll, body, carry):
    def group(g, c):
        for u in range(unroll):
            c = body(g * unroll + u, c)
        return c

    n_groups = n // unroll
    carry = lax.fori_loop(0, n_groups, group, carry)
    return lax.fori_loop(n_groups * unroll, n, body, carry)


def _causal_mask(tile):
    kr = lax.broadcasted_iota(I32, (tile, tile), 0)
    qc = lax.broadcasted_iota(I32, (tile, tile), 1)
    return kr <= qc


def _softmax_accumulate(s, v_aug, acc_ref, idx, *, shift=None, m_ref=None, keep=None):
    if m_ref is None:
        p = jnp.exp2(s - shift)
        acc_ref[idx] += jnp.dot(v_aug, p.astype(BF16), preferred_element_type=F32)
        return
    m_old = m_ref[idx]
    m_new = jnp.maximum(m_old, jnp.max(s, axis=0, keepdims=True))
    p = jnp.exp2(s - m_new)
    if keep is not None:
        p = jnp.where(keep, p, 0.0)
    acc_ref[idx] = jnp.exp2(m_old - m_new) * acc_ref[idx] + jnp.dot(v_aug, p.astype(BF16), preferred_element_type=F32)
    m_ref[idx] = m_new


def _logit_bound(q, k_norm_max, extra):
    qf = q.astype(F32)
    q_norm = jnp.sqrt(jnp.sum(qf * qf, axis=0, keepdims=True))
    return q_norm * (k_norm_max * BOUND_SLACK) + (extra + 1.0)


def _denominators_ok(acc_ref, n_chains):
    low = acc_ref[0, HEAD_DIM:HEAD_DIM + 1, :]
    for idx in range(1, n_chains):
        low = jnp.minimum(low, acc_ref[idx, HEAD_DIM:HEAD_DIM + 1, :])
    return jnp.min(low) > MIN_DENOMINATOR


def _two_pass_attention(run, acc_ref, m_ref, n_chains):
    acc_ref[...] = jnp.zeros(acc_ref.shape, F32)
    run(False)

    @pl.when(jnp.logical_not(_denominators_ok(acc_ref, n_chains)))
    def _():
        acc_ref[...] = jnp.zeros(acc_ref.shape, F32)
        m_ref[...] = jnp.full(m_ref.shape, NEG, F32)
        run(True)


def _normalized(acc_ref, idx):
    return acc_ref[idx, :HEAD_DIM, :] / acc_ref[idx, HEAD_DIM:HEAD_DIM + 1, :]


def _mod_kernel(c_ref, w_ref, b_ref, o_ref):
    c = c_ref[...]
    cond = c * (1.0 / (1.0 + jnp.exp(-c)))
    o_ref[...] = jnp.dot(cond.astype(BF16), w_ref[...].astype(BF16), preferred_element_type=F32) + b_ref[...]


def _modulation(c, w_ada, b_ada):
    depth, d, n = w_ada.shape
    bsz = c.shape[0]
    rows = 8
    c_pad = jnp.zeros((rows, d), F32).at[:bsz].set(c)
    tn = 1024
    out = pl.pallas_call(
        _mod_kernel,
        grid=(depth, n // tn),
        in_specs=[pl.BlockSpec((rows, d), lambda l, j: (0, 0)),
                  pl.BlockSpec((None, d, tn), lambda l, j: (l, 0, j)),
                  pl.BlockSpec((None, 1, tn), lambda l, j: (l, 0, j))],
        out_specs=pl.BlockSpec((None, rows, tn), lambda l, j: (l, 0, j)),
        out_shape=jax.ShapeDtypeStruct((depth, rows, n), F32),
        compiler_params=_cparams(2),
        name="adaln_modulation",
    )(c_pad, w_ada, b_ada.reshape(depth, 1, n))
    return out[:, :bsz]


def _rms(x):
    return x * lax.rsqrt(jnp.mean(x * x, axis=-1, keepdims=True) + EPS)


def _inproj_kernel(x_ref, gain_ref, sc_ref, sh_ref, w_ref, wq_ref, wkv_ref,
                   pa_ref, pb_ref, pc_ref, gc_ref, dq_ref, dkv_ref, misc_ref):
    x = x_ref[...]
    h = (_rms(x) * gain_ref[...]) * (1.0 + sc_ref[...]) + sh_ref[...]
    hb = h.astype(BF16)
    w3 = 3 * GROUP_W

    def proj(c0, c1):
        return jnp.dot(hb, w_ref[:, c0:c1], preferred_element_type=F32)

    pa_ref[...] = proj(0, w3).astype(BF16)
    pb_ref[...] = proj(w3, 2 * w3).astype(BF16)
    pc_ref[...] = proj(2 * w3, 3 * w3).astype(BF16)
    c0 = 3 * w3
    gc_ref[...] = proj(c0, c0 + GROUP_W)
    c0 += GROUP_W
    q_lat = _rms(proj(c0, c0 + DSA_RANK))
    kv_lat = _rms(proj(c0 + DSA_RANK, c0 + 2 * DSA_RANK))
    dq_ref[...] = jnp.dot(q_lat.astype(BF16), wq_ref[...], preferred_element_type=F32).astype(BF16)
    dkv_ref[...] = jnp.dot(kv_lat.astype(BF16), wkv_ref[...], preferred_element_type=F32).astype(BF16)
    c0 += 2 * DSA_RANK
    misc_ref[...] = proj(c0, c0 + 128)


def _input_projection(x, gain, sc, sh, w_in, w_dq_up, w_didx_q, w_dkv_up):
    bsz, seq, d = x.shape
    w3 = 3 * GROUP_W
    o_c = 2 * w3
    o_fc = o_c + w3
    o_gc = o_fc + N_HEADS
    o_ql = o_gc + GROUP_W
    o_kv = o_ql + DSA_RANK
    o_ki = o_kv + DSA_RANK
    o_wi = o_ki + IDX_DIM
    pad = jnp.zeros((d, 128 - IDX_DIM - N_HEADS - IDX_HEADS), w_in.dtype)
    col = jnp.arange(o_fc)
    is_q = (col % w3) < GROUP_W
    q_scale = jnp.where(is_q, jnp.where(col < w3, DIFF_HALF ** -0.5, HEAD_DIM ** -0.5) * LOG2E, 1.0).astype(F32)
    w = jnp.concatenate([w_in[:, :o_fc] * q_scale[None, :], w_in[:, o_gc:o_ql], w_in[:, o_ql:o_ki],
                         w_in[:, o_ki:o_wi], w_in[:, o_fc:o_gc], w_in[:, o_wi:], pad], axis=1).astype(BF16)
    n_cols = w.shape[1]
    wq = jnp.concatenate([w_dq_up * (HEAD_DIM ** -0.5 * LOG2E), w_didx_q], axis=1).astype(BF16)
    wkv = w_dkv_up.astype(BF16)
    tm = ROW_TILE
    row = lambda width: pl.BlockSpec((None, tm, width), lambda b, i: (b, i, 0))
    vec = pl.BlockSpec((None, 1, d), lambda b, i: (b, 0, 0))
    outs = pl.pallas_call(
        _inproj_kernel,
        grid=(bsz, seq // tm),
        in_specs=[row(d), pl.BlockSpec((1, d), lambda b, i: (0, 0)), vec, vec,
                  _resident((d, n_cols), lambda b, i: (0, 0)),
                  _resident(wq.shape, lambda b, i: (0, 0)),
                  _resident(wkv.shape, lambda b, i: (0, 0))],
        out_specs=[row(w3), row(w3), row(w3), row(GROUP_W), row(512), row(512), row(128)],
        out_shape=[jax.ShapeDtypeStruct((bsz, seq, w3), BF16)] * 3
                  + [jax.ShapeDtypeStruct((bsz, seq, GROUP_W), F32)]
                  + [jax.ShapeDtypeStruct((bsz, seq, 512), BF16)] * 2
                  + [jax.ShapeDtypeStruct((bsz, seq, 128), F32)],
        compiler_params=_cparams(2),
        name="input_projection",
    )(x, gain.reshape(1, d), sc, sh, w, wq, wkv)
    return outs


def _heads_t(t):
    bsz, seq, w = t.shape
    return t.reshape(bsz, seq, N_HEADS, w // N_HEADS).transpose(0, 2, 3, 1)


def _heads(t):
    bsz, seq, w = t.shape
    return t.reshape(bsz, seq, N_HEADS, w // N_HEADS).transpose(0, 2, 1, 3)


def _values_aug(v):
    v_t = _heads_t(v)
    bsz, nh, dh, seq = v_t.shape
    return jnp.concatenate([v_t, jnp.ones((bsz, nh, 1, seq), v.dtype),
                            jnp.zeros((bsz, nh, V_ROWS - dh - 1, seq), v.dtype)], axis=2)


def _key_norm_max(k):
    bsz, seq, w = k.shape
    kf = k.astype(F32).reshape(bsz, seq, N_HEADS, w // N_HEADS)
    return jnp.max(jnp.sqrt(jnp.sum(kf * kf, axis=-1)), axis=1).reshape(-1)


def _unheads_t(o_t):
    bsz, h, dh, seq = o_t.shape
    return o_t.transpose(0, 3, 1, 2).reshape(bsz, seq, h * dh)


def _diff_kernel(t5_ref, bmax_ref, kmax_ref, lam_ref, q_ref, k_ref, v_ref, g_ref, o_ref, bias_ref, m_ref, acc_ref,
                 *, tile, head0, heads, group_heads, out_scale):
    b = pl.program_id(0)
    hg = pl.program_id(1)
    i = pl.program_id(2)

    @pl.when(i == 0)
    def _():
        for h in range(heads):
            _fill_t5_tile(bias_ref, 2 * h, t5_ref, head0 + hg * heads + h, 0, tile)
            _fill_t5_tile(bias_ref, 2 * h + 1, t5_ref, head0 + hg * heads + h, tile, tile)

    far_bias = [LOG2E * t5_ref[(T5_BUCKETS - 1) * N_BIAS_HEADS + head0 + hg * heads + h] for h in range(heads)]
    bounds = [_logit_bound(q_ref[h, c], kmax_ref[b * group_heads + hg * heads + h], bmax_ref[head0 + hg * heads + h])
              for h in range(heads) for c in range(2)]

    def run(online):
        def step(j, kind):
            k0 = pl.multiple_of(j * tile, tile)
            logits = [jnp.dot(k_ref[h, pl.ds(k0, tile), :], q_ref[h, c], preferred_element_type=F32)
                      for h in range(heads) for c in range(2)]
            for h in range(heads):
                vt = v_ref[h, :, pl.ds(k0, tile)]
                for c in range(2):
                    idx = 2 * h + c
                    s, shift = logits[idx], bounds[idx]
                    if kind != "far":
                        s = s + bias_ref[2 * h + (1 if kind == "near" else 0)]
                    elif online:
                        s = s + far_bias[h]
                    else:
                        shift = shift - far_bias[h]
                    if kind == "diag":
                        s = jnp.where(_causal_mask(tile), s, NEG)
                    _softmax_accumulate(s, vt, acc_ref, idx, shift=shift, m_ref=m_ref if online else None)

        step(i, "diag")

        @pl.when(i >= 1)
        def _():
            step(i - 1, "near")

        def far(j, carry):
            step(j, "far")
            return carry

        _unrolled_loop(jnp.maximum(i - 1, 0), 2, far, 0)

    _two_pass_attention(run, acc_ref, m_ref, 2 * heads)

    for h in range(heads):
        o = _normalized(acc_ref, 2 * h) - lam_ref[0] * _normalized(acc_ref, 2 * h + 1)
        y = o * lax.rsqrt(jnp.mean(o * o, axis=0, keepdims=True) + EPS)
        o_ref[h] = ((y * g_ref[...]) * out_scale).astype(o_ref.dtype)


def _diff_attention(q_t, k, v_aug, lam, subln_gain, t5_flat, bias_max, k_norm_max, out_scale):
    bsz, nh, _, dh, seq = q_t.shape
    tile = min(ATTN_TILE, seq)
    hps = DIFF_HEADS_PER_STEP
    assert tile >= T5_FAR_DIST and seq % tile == 0 and nh % hps == 0
    kern = functools.partial(_diff_kernel, tile=tile, head0=0, heads=hps, group_heads=nh, out_scale=out_scale)
    return pl.pallas_call(
        kern,
        grid=(bsz, nh // hps, seq // tile),
        in_specs=[_smem(), _smem(), _smem(), _smem(),
                  pl.BlockSpec((None, hps, 2, dh, tile), lambda b, h, i: (b, h, 0, 0, i)),
                  _resident((None, hps, seq, dh), lambda b, h, i: (b, h, 0, 0)),
                  _resident((None, hps, V_ROWS, seq), lambda b, h, i: (b, h, 0, 0)),
                  pl.BlockSpec((dh, 1), lambda b, h, i: (0, 0))],
        out_specs=pl.BlockSpec((None, hps, dh, tile), lambda b, h, i: (b, h, 0, i)),
        out_shape=jax.ShapeDtypeStruct((bsz, nh, dh, seq), BF16),
        scratch_shapes=[pltpu.VMEM((2 * hps, tile, tile), F32),
                        pltpu.VMEM((2 * hps, 1, tile), F32),
                        pltpu.VMEM((2 * hps, V_ROWS, tile), F32)],
        compiler_params=_cparams(3),
        name="diff_attention",
    )(t5_flat, bias_max, k_norm_max, lam.reshape(1), q_t, k, v_aug, subln_gain.reshape(dh, 1))


def _kmean_kernel(k_ref, o_ref, *, blocks):
    k = k_ref[...].astype(F32)
    o_ref[...] = jnp.mean(k.reshape(blocks, MOBA_BLOCK, k.shape[-1]), axis=1)


def _moba_kmean(k):
    bsz, nh, seq, dh = k.shape
    nb = seq // MOBA_BLOCK
    blocks = min(8, nb)
    return pl.pallas_call(
        functools.partial(_kmean_kernel, blocks=blocks),
        grid=(bsz, nh, nb // blocks),
        in_specs=[pl.BlockSpec((None, None, blocks * MOBA_BLOCK, dh), lambda b, h, i: (b, h, i, 0))],
        out_specs=pl.BlockSpec((None, None, blocks, dh), lambda b, h, i: (b, h, i, 0)),
        out_shape=jax.ShapeDtypeStruct((bsz, nh, nb, dh), F32),
        compiler_params=_cparams(3),
    )(k)


def _moba_kernel(t5_ref, bmax_ref, kmax_ref, q_ref, k_ref, v_ref, km_ref, o_ref, bias_ref, sel_ref, m_ref, acc_ref,
                 *, tile, head0, heads, n_blocks):
    b = pl.program_id(0)
    i = pl.program_id(1)

    @pl.when(i == 0)
    def _():
        for h in range(heads):
            _fill_t5_tile(bias_ref, 2 * h, t5_ref, head0 + h, 0, tile)
            _fill_t5_tile(bias_ref, 2 * h + 1, t5_ref, head0 + h, tile, tile)

    far_bias = [LOG2E * t5_ref[(T5_BUCKETS - 1) * N_BIAS_HEADS + head0 + h] for h in range(heads)]
    bounds = [_logit_bound(q_ref[h], kmax_ref[b * heads + h], bmax_ref[head0 + h]) for h in range(heads)]

    nidx = lax.broadcasted_iota(I32, (n_blocks, tile), 0)
    for h in range(heads):
        gate = jnp.dot(km_ref[h], q_ref[h], preferred_element_type=F32)
        g = jnp.where(nidx < i, gate, NEG)
        sel = jnp.zeros(gate.shape, F32)
        for _ in range(min(MOBA_TOPK, n_blocks)):
            mx = jnp.max(g, axis=0, keepdims=True)
            first = jnp.min(jnp.where(g == mx, nidx, n_blocks), axis=0, keepdims=True)
            pick = nidx == first
            sel = jnp.where(pick, 1.0, sel)
            g = jnp.where(pick, -jnp.inf, g)
        sel_ref[h] = jnp.where(nidx < i, sel, 0.0)

    def run(online):
        def step(j, kind):
            k0 = pl.multiple_of(j * tile, tile)
            logits = [jnp.dot(k_ref[h, pl.ds(k0, tile), :], q_ref[h], preferred_element_type=F32)
                      for h in range(heads)]
            for h in range(heads):
                s, shift = logits[h], bounds[h]
                if kind != "far":
                    s = s + bias_ref[2 * h + (1 if kind == "near" else 0)]
                elif online:
                    s = s + far_bias[h]
                else:
                    shift = shift - far_bias[h]
                mask = _causal_mask(tile) if kind == "diag" else sel_ref[h, pl.ds(j, 1), :] > 0.5
                s = jnp.where(mask, s, NEG)
                _softmax_accumulate(s, v_ref[h, :, pl.ds(k0, tile)], acc_ref, h, shift=shift,
                                    m_ref=m_ref if online else None)

        step(i, "diag")

        @pl.when(i >= 1)
        def _():
            step(i - 1, "near")

        def far(j, carry):
            step(j, "far")
            return carry

        _unrolled_loop(jnp.maximum(i - 1, 0), 4, far, 0)

    _two_pass_attention(run, acc_ref, m_ref, heads)
    for h in range(heads):
        o_ref[h] = _normalized(acc_ref, h).astype(o_ref.dtype)


def _moba_attention(q_t, k, v_aug, kmean, t5_flat, bias_max, k_norm_max):
    bsz, nh, dh, seq = q_t.shape
    tile = MOBA_BLOCK
    assert SPARSE_TILE == MOBA_BLOCK and seq % tile == 0 and tile >= T5_FAR_DIST
    nb = seq // tile
    kern = functools.partial(_moba_kernel, tile=tile, head0=N_HEADS, heads=nh, n_blocks=nb)
    return pl.pallas_call(
        kern,
        grid=(bsz, nb),
        in_specs=[_smem(), _smem(), _smem(),
                  pl.BlockSpec((None, nh, dh, tile), lambda b, i: (b, 0, 0, i)),
                  _resident((None, nh, seq, dh), lambda b, i: (b, 0, 0, 0)),
                  _resident((None, nh, V_ROWS, seq), lambda b, i: (b, 0, 0, 0)),
                  _resident((None, nh, nb, dh), lambda b, i: (b, 0, 0, 0))],
        out_specs=pl.BlockSpec((None, nh, dh, tile), lambda b, i: (b, 0, 0, i)),
        out_shape=jax.ShapeDtypeStruct((bsz, nh, dh, seq), BF16),
        scratch_shapes=[pltpu.VMEM((2 * nh, tile, tile), F32), pltpu.VMEM((nh, nb, tile), F32),
                        pltpu.VMEM((nh, 1, tile), F32),
                        pltpu.VMEM((nh, V_ROWS, tile), F32)],
        compiler_params=_cparams(2),
        name="moba_attention",
    )(t5_flat, bias_max, k_norm_max, q_t, k, v_aug, kmean)


def _split3(x):
    def trunc(v):
        bits = lax.bitcast_convert_type(v, I32)
        return lax.bitcast_convert_type(bits & jnp.int32(-65536), F32)
    hi = trunc(x)
    r1 = x - hi
    mid = trunc(r1)
    lo = trunc(r1 - mid)
    return hi, mid, lo


def _forget_cumsum_kernel(f_ref, hi_ref, mid_ref, lo_ref):
    x = f_ref[...]
    rows = x.shape[0]
    ls = jnp.minimum(x, 0.0) - jnp.log(1.0 + jnp.exp(-jnp.abs(x)))
    upper = (lax.broadcasted_iota(I32, (128, 128), 0) <= lax.broadcasted_iota(I32, (128, 128), 1)).astype(F32)
    within = jnp.dot(ls, upper, preferred_element_type=F32, precision=lax.Precision.HIGHEST)
    strict = (lax.broadcasted_iota(I32, (rows, rows), 1) < lax.broadcasted_iota(I32, (rows, rows), 0)).astype(F32)
    before = jnp.dot(strict, within, preferred_element_type=F32, precision=lax.Precision.HIGHEST)
    cf = (within + before[:, 127:128]) * LOG2E
    hi, mid, lo = _split3(cf)
    hi_ref[...] = hi
    mid_ref[...] = mid
    lo_ref[...] = lo


def _forget_cumsum(f_logit):
    bsz, seq, nh = f_logit.shape
    rows = seq // 128
    f = f_logit.transpose(0, 2, 1).reshape(bsz, nh, rows, 128)
    spec = pl.BlockSpec((None, None, rows, 128), lambda b, h: (b, h, 0, 0))
    parts = pl.pallas_call(
        _forget_cumsum_kernel,
        grid=(bsz, nh),
        in_specs=[spec],
        out_specs=[spec] * 3,
        out_shape=[jax.ShapeDtypeStruct((bsz, nh, rows, 128), F32)] * 3,
        compiler_params=_cparams(2),
        name="forget_cumsum",
    )(f)
    return [p.reshape(bsz, nh, seq) for p in parts]


def _forget_kernel(kmax_ref, q_ref, k_ref, v_ref, gate_ref, o_ref, m_ref, acc_ref, *, tile, heads, group_heads):
    b = pl.program_id(0)
    hg = pl.program_id(1)
    i = pl.program_id(2)
    bounds = [_logit_bound(q_ref[h, :HEAD_DIM, :], kmax_ref[b * group_heads + hg * heads + h], 0.0)
              for h in range(heads)]

    def run(online):
        def step(j, blocks, mask):
            k0 = pl.multiple_of(j * tile, tile)
            logits = [jnp.dot(k_ref[h, pl.ds(k0, blocks * tile), :], q_ref[h], preferred_element_type=F32)
                      for h in range(heads)]
            for h in range(heads):
                s = logits[h] if mask is None else jnp.where(mask, logits[h], NEG)
                _softmax_accumulate(s, v_ref[h, :, pl.ds(k0, blocks * tile)], acc_ref, h, shift=bounds[h],
                                    m_ref=m_ref if online else None)

        step(i, 1, _causal_mask(tile))
        n_big = i // FORGET_BLOCKS_PER_STEP

        def past_big(j, carry):
            step(j * FORGET_BLOCKS_PER_STEP, FORGET_BLOCKS_PER_STEP, None)
            return carry

        def past_single(j, carry):
            step(j, 1, None)
            return carry

        lax.fori_loop(0, n_big, past_big, 0)
        lax.fori_loop(n_big * FORGET_BLOCKS_PER_STEP, i, past_single, 0)

    _two_pass_attention(run, acc_ref, m_ref, heads)
    for h in range(heads):
        g = gate_ref[h]
        o_ref[h] = (_normalized(acc_ref, h) * (1.0 / (1.0 + jnp.exp(-g)))).astype(o_ref.dtype)


def _forgetting_attention(q_aug_t, k_aug, v_aug, gate_t, k_norm_max):
    bsz, nh, kdim, seq = q_aug_t.shape
    dh = gate_t.shape[2]
    tile = min(ATTN_TILE, seq)
    hps = FORGET_HEADS_PER_STEP
    assert seq % tile == 0 and nh % hps == 0
    return pl.pallas_call(
        functools.partial(_forget_kernel, tile=tile, heads=hps, group_heads=nh),
        grid=(bsz, nh // hps, seq // tile),
        in_specs=[_smem(),
                  pl.BlockSpec((None, hps, kdim, tile), lambda b, h, i: (b, h, 0, i)),
                  _resident((None, hps, seq, kdim), lambda b, h, i: (b, h, 0, 0)),
                  _resident((None, hps, V_ROWS, seq), lambda b, h, i: (b, h, 0, 0)),
                  pl.BlockSpec((None, hps, dh, tile), lambda b, h, i: (b, h, 0, i))],
        out_specs=pl.BlockSpec((None, hps, dh, tile), lambda b, h, i: (b, h, 0, i)),
        out_shape=jax.ShapeDtypeStruct((bsz, nh, dh, seq), BF16),
        scratch_shapes=[pltpu.VMEM((hps, 1, tile), F32),
                        pltpu.VMEM((hps, V_ROWS, tile), F32)],
        compiler_params=_cparams(3),
        name="forget_attention",
    )(k_norm_max, q_aug_t, k_aug, v_aug, gate_t)


def _key_to_float(u):
    ks = u ^ jnp.int32(-2147483648)
    bits = jnp.where(ks < 0, ks ^ jnp.int32(2147483647), ks)
    return lax.bitcast_convert_type(bits, F32)


def _truncate_to_bf16(x):
    bits = lax.bitcast_convert_type(x, I32) & jnp.int32(-65536)
    return lax.bitcast_convert_type(bits, F32).astype(BF16)


def _dsa_kernel(t5_ref, bmax_ref, kmax_ref, qi_ref, w_ref, ki_ref, q_ref, k_ref, v_ref, o_ref,
                sc_ref, hi_ref, bias_ref, tri_ref, m_ref, acc_ref, *, tile, head0, topk):
    b = pl.program_id(0)
    i = pl.program_id(1)

    @pl.when((b == 0) & (i == 0))
    def _():
        for h in range(N_HEADS):
            _fill_t5_tile(bias_ref, 2 * h, t5_ref, head0 + h, 0, tile)
            _fill_t5_tile(bias_ref, 2 * h + 1, t5_ref, head0 + h, tile, tile)
        tri_ref[...] = jnp.where(lax.broadcasted_iota(I32, (tile, tile), 1) <= lax.broadcasted_iota(I32, (tile, tile), 0),
                                 1.0, 0.0).astype(BF16)

    causal = _causal_mask(tile)

    def index_scores(j):
        k0 = pl.multiple_of(j * tile, tile)
        kt = ki_ref[pl.ds(k0, tile), :]
        raws = [jnp.dot(kt, qi_ref[h * IDX_DIM:(h + 1) * IDX_DIM, :], preferred_element_type=F32)
                for h in range(IDX_HEADS)]
        sc = jnp.zeros((tile, tile), F32)
        for h in range(IDX_HEADS):
            sc = sc + jnp.maximum(raws[h], 0.0) * w_ref[h:h + 1, :]
        return k0, sc

    def store_scores(k0, sc):
        sc_ref[pl.ds(k0, tile), :] = sc
        hi_ref[pl.ds(k0, tile), :] = _truncate_to_bf16(sc)

    def score_body(j, carry):
        store_scores(*index_scores(j))
        return carry

    _unrolled_loop(i, 4, score_body, 0)
    k0, sc = index_scores(i)
    store_scores(k0, jnp.where(causal, sc, NEG))

    def count(pred):
        def body(j, cnt):
            k0 = pl.multiple_of(j * tile, tile)
            ind = jnp.where(pred(sc_ref[pl.ds(k0, tile), :]), 1.0, 0.0)
            return cnt + jnp.sum(ind.reshape(tile // 8, 8, tile), axis=0)
        cnt = _unrolled_loop(i + 1, 4, body, jnp.zeros((8, tile), F32))
        return jnp.sum(cnt, axis=0, keepdims=True)

    def count_truncated(thr):
        def body(j, cnt):
            k0 = pl.multiple_of(j * tile, tile)
            ind = jnp.where(hi_ref[pl.ds(k0, tile), :] >= thr, jnp.ones((), BF16), jnp.zeros((), BF16))
            part = ind[0:BF16_ROWS]
            for r in range(1, tile // BF16_ROWS):
                part = part + ind[r * BF16_ROWS:(r + 1) * BF16_ROWS]
            return cnt + part.astype(F32)
        cnt = _unrolled_loop(i + 1, 4, body, jnp.zeros((BF16_ROWS, tile), F32))
        return jnp.sum(cnt, axis=0, keepdims=True)

    def high_bit(it, ans):
        cand = ans | jnp.left_shift(jnp.int32(1), 31 - it)
        return jnp.where(count_truncated(_truncate_to_bf16(_key_to_float(cand))) >= topk, cand, ans)

    ans = lax.fori_loop(0, 16, high_bit, jnp.zeros((1, tile), I32))

    above = count(lambda s: s > _key_to_float(ans))
    settled = above < topk
    need = jnp.where(settled, topk - above, 0.0)

    def low_bit(state):
        it, ans, open_f, need = state
        cand = ans | jnp.left_shift(jnp.int32(1), 31 - it)
        thr = _key_to_float(cand)
        cnt = count(lambda s: s >= thr)
        unsettled = open_f > 0.5
        ans = jnp.where(unsettled & (cnt >= topk), cand, ans)
        exact = unsettled & (cnt == topk)
        return it + 1, ans, jnp.where(exact, 0.0, open_f), jnp.where(exact, TAKE_ALL_TIES, need)

    _, ans, open_f, need = lax.while_loop(lambda st: (st[0] < 32) & (jnp.max(st[2]) > 0.5), low_bit,
                                          (jnp.int32(16), ans, jnp.where(settled, 0.0, 1.0), need))
    tau = _key_to_float(ans)
    need = jnp.where(open_f > 0.5, topk - count(lambda s: s > tau), need)

    qs = [q_ref[h] for h in range(N_HEADS)]

    far_bias = [LOG2E * t5_ref[(T5_BUCKETS - 1) * N_BIAS_HEADS + head0 + h] for h in range(N_HEADS)]
    bounds = [_logit_bound(qs[h], kmax_ref[b * N_HEADS + h], bmax_ref[head0 + h]) for h in range(N_HEADS)]

    def run(online):
        def attend(j, ties_seen, kind):
            k0 = pl.multiple_of(j * tile, tile)
            sc = sc_ref[pl.ds(k0, tile), :]
            eq = sc == tau
            eqf = jnp.where(eq, 1.0, 0.0)
            rank = ties_seen + jnp.dot(tri_ref[...], eqf.astype(BF16), preferred_element_type=F32)
            wgt = jnp.where(eq, jnp.where(rank <= need, 1.0, 0.0), jnp.where(sc > tau, 1.0, 0.0))
            if kind == "diag":
                wgt = jnp.where(causal, wgt, 0.0)
            keep = wgt > 0.5
            kt = k_ref[pl.ds(k0, tile), :]
            logits = [jnp.dot(kt, qs[h], preferred_element_type=F32) for h in range(N_HEADS)]
            for h in range(N_HEADS):
                s, shift = logits[h], bounds[h]
                if kind != "far":
                    s = s + bias_ref[2 * h + (1 if kind == "near" else 0)]
                elif online:
                    s = s + far_bias[h]
                else:
                    shift = shift - far_bias[h]
                s = jnp.where(keep, s, NEG)
                _softmax_accumulate(s, v_ref[h, :, pl.ds(k0, tile)], acc_ref, h, shift=shift,
                                    m_ref=m_ref if online else None, keep=keep)
            return ties_seen + jnp.sum(eqf, axis=0, keepdims=True)

        seen = _unrolled_loop(jnp.maximum(i - 1, 0), 4, lambda j, r: attend(j, r, "far"), jnp.zeros((1, tile), F32))
        seen = lax.cond(i >= 1, lambda r: attend(i - 1, r, "near"), lambda r: r, seen)
        attend(i, seen, "diag")

    _two_pass_attention(run, acc_ref, m_ref, N_HEADS)
    for h in range(N_HEADS):
        o_ref[h * HEAD_DIM:(h + 1) * HEAD_DIM, :] = _normalized(acc_ref, h).astype(o_ref.dtype)


def _dsa_attention(qi_t, w_t, k_idx, q_t, k, v_aug, t5_flat, bias_max, k_norm_max):
    bsz, _, width, seq = q_t.shape
    tile = min(SPARSE_TILE, seq)
    topk = min(DSA_TOPK_MAX, seq // 4)
    assert seq % tile == 0 and tile >= topk and tile >= T5_FAR_DIST
    kern = functools.partial(_dsa_kernel, tile=tile, head0=2 * N_HEADS, topk=topk)
    return pl.pallas_call(
        kern,
        grid=(bsz, seq // tile),
        in_specs=[_smem(), _smem(), _smem(),
                  pl.BlockSpec((None, qi_t.shape[1], tile), lambda b, i: (b, 0, i)),
                  pl.BlockSpec((None, 8, tile), lambda b, i: (b, 0, i)),
                  _resident((None, seq, k_idx.shape[2]), lambda b, i: (b, 0, 0)),
                  pl.BlockSpec((None, N_HEADS, width, tile), lambda b, i: (b, 0, 0, i)),
                  _resident((None, seq, width), lambda b, i: (b, 0, 0)),
                  _resident((None, N_HEADS, V_ROWS, seq), lambda b, i: (b, 0, 0, 0))],
        out_specs=pl.BlockSpec((None, width, tile), lambda b, i: (b, 0, i)),
        out_shape=jax.ShapeDtypeStruct((bsz, width, seq), BF16),
        scratch_shapes=[pltpu.VMEM((seq, tile), F32), pltpu.VMEM((seq, tile), BF16),
                        pltpu.VMEM((2 * N_HEADS, tile, tile), F32),
                        pltpu.VMEM((tile, tile), BF16),
                        pltpu.VMEM((N_HEADS, 1, tile), F32),
                        pltpu.VMEM((N_HEADS, V_ROWS, tile), F32)],
        compiler_params=_cparams(2),
        name="dsa_attention",
    )(t5_flat, bias_max, k_norm_max, qi_t, w_t, k_idx, q_t, k, v_aug)


def _outproj_kernel(x_ref, o_ref, w_ref, g_ref, y_ref):
    y_ref[...] = x_ref[...] + g_ref[...] * jnp.dot(o_ref[...], w_ref[...], preferred_element_type=F32)


def _output_projection(x, o, w_out, gate):
    bsz, seq, d = x.shape
    tm = ROW_TILE
    row = pl.BlockSpec((None, tm, d), lambda b, i: (b, i, 0))
    return pl.pallas_call(
        _outproj_kernel,
        grid=(bsz, seq // tm),
        in_specs=[row, pl.BlockSpec((None, tm, o.shape[2]), lambda b, i: (b, i, 0)),
                  _resident(w_out.shape, lambda b, i: (0, 0)),
                  pl.BlockSpec((None, 1, d), lambda b, i: (b, 0, 0))],
        out_specs=row,
        out_shape=jax.ShapeDtypeStruct(x.shape, F32),
        compiler_params=_cparams(2),
        name="output_projection",
    )(x, o, w_out.astype(BF16), gate)


HALO = 8


def _ffn_kernel(x_ref, xp_ref, gain_ref, sc_ref, sh_ref, g_ref, wup_ref, cw_ref, cb_ref, wd_ref, fg_ref,
                y_ref, h_ref, acc_ref, *, tm, chunk, n_chunks, final_norm):
    i = pl.program_id(1)

    def modulated(x):
        return ((_rms(x) * gain_ref[...]) * (1.0 + sc_ref[...]) + sh_ref[...]).astype(BF16)

    h_ref[HALO:, :] = modulated(x_ref[...])
    halo = modulated(xp_ref[...])
    h_ref[:HALO, :] = jnp.where(i > 0, halo, jnp.zeros_like(halo))
    acc_ref[...] = jnp.zeros(acc_ref.shape, F32)
    h = h_ref[...]

    def up(c0, half):
        col = pl.multiple_of(half * D_FF + c0, 128)
        return jnp.dot(h, wup_ref[:, pl.ds(col, chunk)], preferred_element_type=F32)

    def conv(u, c0, half):
        cw = cw_ref[half, :, pl.ds(c0, chunk)]
        return (cw[0:1] * u[HALO - 2:HALO - 2 + tm] + cw[1:2] * u[HALO - 1:HALO - 1 + tm]
                + cw[2:3] * u[HALO:HALO + tm]) + cb_ref[half, :, pl.ds(c0, chunk)]

    def chunks(first, count):
        c0s = [pl.multiple_of((first + n) * chunk, chunk) for n in range(count)]
        ups = [(up(c0, 0), up(c0, 1)) for c0 in c0s]
        acts = []
        for c0, (ug, uv) in zip(c0s, ups):
            gate = conv(ug, c0, 0)
            acts.append(((gate * (1.0 / (1.0 + jnp.exp(-gate)))) * conv(uv, c0, 1)).astype(BF16))
        a = acts[0] if count == 1 else jnp.concatenate(acts, axis=1)
        acc_ref[...] += jnp.dot(a, wd_ref[pl.ds(c0s[0], count * chunk), :], preferred_element_type=F32)

    def pair(p, carry):
        chunks(2 * p, 2)
        return carry

    lax.fori_loop(0, n_chunks // 2, pair, 0)
    if n_chunks % 2:
        chunks(n_chunks - 1, 1)

    y = x_ref[...] + g_ref[...] * acc_ref[...]
    if final_norm:
        y = _rms(y) * fg_ref[...]
    y_ref[...] = y


def _ffn(x, gain, sc, sh, gate, w_up, conv_w, conv_b, w_down, final_gain, final_norm):
    bsz, seq, d = x.shape
    tm = min(FFN_ROW_TILE, seq)
    fc = FFN_CHUNK
    n_chunks = D_FF // fc
    w_up_b = w_up.astype(BF16)
    cw = conv_w.reshape(3, 2, D_FF).transpose(1, 0, 2)
    cb = conv_b.reshape(2, 1, D_FF)
    vec = pl.BlockSpec((None, 1, d), lambda b, i: (b, 0, 0))
    one = pl.BlockSpec((1, d), lambda b, i: (0, 0))
    row = pl.BlockSpec((None, tm, d), lambda b, i: (b, i, 0))
    whole = lambda a: _resident(a.shape, lambda b, i: (0,) * a.ndim)
    w_down_b = w_down.astype(BF16)
    return pl.pallas_call(
        functools.partial(_ffn_kernel, tm=tm, chunk=fc, n_chunks=n_chunks, final_norm=final_norm),
        grid=(bsz, seq // tm),
        in_specs=[row,
                  pl.BlockSpec((None, HALO, d), lambda b, i: (b, jnp.maximum(i * (tm // HALO) - 1, 0), 0)),
                  one, vec, vec, vec, whole(w_up_b), whole(cw), whole(cb), whole(w_down_b), one],
        out_specs=row,
        out_shape=jax.ShapeDtypeStruct(x.shape, F32),
        scratch_shapes=[pltpu.VMEM((HALO + tm, d), BF16), pltpu.VMEM((tm, d), F32)],
        compiler_params=_cparams(2),
        name="conv_glu_ffn",
    )(x, x, gain.reshape(1, d), sc, sh, gate, w_up_b, cw, cb, w_down_b, final_gain.reshape(1, d))


def kernel(x, c, w_ada, b_ada, norm1_gain, w_in, f_bias, diff_lambda, diff_subln_gain, w_dq_up, w_didx_q,
           w_dkv_up, w_out, t5_table, norm2_gain, w_ffn_up, ffn_conv_w, ffn_conv_b, w_ffn_down, final_gain):
    bsz, seq, d = x.shape
    depth = w_ada.shape[0]
    t5_flat = t5_table.reshape(-1)
    bias_max = jnp.max(t5_table, axis=0) * LOG2E
    mod = _modulation(c, w_ada, b_ada)
    w = GROUP_W

    for l in range(depth):
        sh1, sc1, g1, sh2, sc2, g2 = [m[:, None, :] for m in jnp.split(mod[l], 6, axis=-1)]
        pa, pb, pc, gc, dq, dkv, misc = _input_projection(x, norm1_gain[l], sc1, sh1, w_in[l],
                                                          w_dq_up[l], w_didx_q[l], w_dkv_up[l])
        k_idx = misc[..., :IDX_DIM].astype(BF16)
        fc = misc[..., IDX_DIM:IDX_DIM + N_HEADS]
        w_idx = misc[..., IDX_DIM + N_HEADS:IDX_DIM + N_HEADS + IDX_HEADS]

        lambda_init = 0.8 - 0.6 * math.exp(-0.3 * l)
        lq1, lk1, lq2, lk2 = diff_lambda[l]
        lam = jnp.exp(jnp.sum(lq1 * lk1)) - jnp.exp(jnp.sum(lq2 * lk2)) + lambda_init
        qa_t = _heads_t(pa[..., :w]).reshape(bsz, N_HEADS, 2, DIFF_HALF, seq)
        zeros = jnp.zeros_like(qa_t)
        qa_t = jnp.stack([jnp.concatenate([qa_t[:, :, 0], zeros[:, :, 0]], axis=2),
                          jnp.concatenate([zeros[:, :, 1], qa_t[:, :, 1]], axis=2)], axis=2)
        o_a = _diff_attention(qa_t, _heads(pa[..., w:2 * w]), _values_aug(pa[..., 2 * w:]), lam,
                              diff_subln_gain[l], t5_flat, bias_max, _key_norm_max(pa[..., w:2 * w]),
                              1.0 - lambda_init)

        kb = _heads(pb[..., w:2 * w])
        o_b = _moba_attention(_heads_t(pb[..., :w]), kb, _values_aug(pb[..., 2 * w:]),
                              _moba_kmean(kb).astype(BF16), t5_flat, bias_max, _key_norm_max(pb[..., w:2 * w]))

        hi, mid, lo = _forget_cumsum(fc + f_bias[l])
        ones = jnp.ones_like(hi)
        cparts = jnp.stack([hi, mid, lo], axis=2)
        onep = jnp.stack([ones] * 3, axis=2)
        padw = 2 * HEAD_DIM - HEAD_DIM - 6
        qc_aug = jnp.concatenate([_heads_t(pc[..., :w]), onep.astype(BF16), cparts.astype(BF16),
                                  jnp.zeros((bsz, N_HEADS, padw, seq), BF16)], axis=2)
        kc_aug = jnp.concatenate([_heads(pc[..., w:2 * w]), (-cparts).transpose(0, 1, 3, 2).astype(BF16),
                                  onep.transpose(0, 1, 3, 2).astype(BF16),
                                  jnp.zeros((bsz, N_HEADS, seq, padw), BF16)], axis=3)
        o_c = _forgetting_attention(qc_aug, kc_aug, _values_aug(pc[..., 2 * w:]), _heads_t(gc),
                                    _key_norm_max(pc[..., w:2 * w]))

        qd_t = dq[..., :w].transpose(0, 2, 1)
        head_of_row = jnp.arange(w)[None, :, None] // HEAD_DIM
        qd_t = jnp.stack([jnp.where(head_of_row == h, qd_t, jnp.zeros((), BF16)) for h in range(N_HEADS)],
                         axis=1)
        qi_t = dq[..., w:].transpose(0, 2, 1)
        w_t = (w_idx * (IDX_HEADS ** -0.5 * IDX_DIM ** -0.5)).transpose(0, 2, 1)
        w_t = jnp.concatenate([w_t, jnp.zeros((bsz, 8 - IDX_HEADS, seq), F32)], axis=1)
        o_d = _dsa_attention(qi_t, w_t, k_idx, qd_t, dkv[..., :w], _values_aug(dkv[..., w:]), t5_flat, bias_max,
                             _key_norm_max(dkv[..., :w]))

        o = jnp.concatenate([_unheads_t(o_a), _unheads_t(o_b), _unheads_t(o_c), o_d.transpose(0, 2, 1)], axis=-1)
        x = _output_projection(x, o, w_out[l], g1)
        x = _ffn(x, norm2_gain[l], sc2, sh2, g2, w_ffn_up[l], ffn_conv_w[l], ffn_conv_b[l], w_ffn_down[l],
                 final_gain, final_norm=(l == depth - 1))
    return x
```

```python
import functools
import math

import numpy as np
import jax
import jax.numpy as jnp
from jax import lax
from jax.experimental import pallas as pl
from jax.experimental.pallas import tpu as pltpu

F32 = jnp.float32
BF16 = jnp.bfloat16
I32 = jnp.int32

HEAD_DIM = 64
N_HEADS = 4
GROUP_W = N_HEADS * HEAD_DIM
DIFF_HALF = HEAD_DIM // 2
MOBA_BLOCK = 256
MOBA_TOPK = 3
DSA_RANK = 128
IDX_HEADS = 4
IDX_DIM = 64
DSA_TOPK_MAX = 256
T5_BUCKETS = 32
T5_MAX_DIST = 128
N_BIAS_HEADS = 12
D_FF = 2816
EPS = 1e-6
NEG = -1e30
LOG2E = math.log2(math.e)
V_ROWS = HEAD_DIM + 16
BOUND_SLACK = 1.0 + 2.0 ** -6
MIN_DENOMINATOR = 2.0 ** -60
BF16_ROWS = 16
TAKE_ALL_TIES = 1e9

V7X_VMEM_LIMIT_BYTES = 56 * 1024 * 1024

ATTN_TILE = 512
SPARSE_TILE = 256
ROW_TILE = 512
FFN_ROW_TILE = 1024
FFN_CHUNK = 256
FORGET_HEADS_PER_STEP = 4
DIFF_HEADS_PER_STEP = 2
FORGET_BLOCKS_PER_STEP = 2


def _t5_thresholds():
    d = np.arange(0, 4 * T5_MAX_DIST)
    max_exact = T5_BUCKETS // 2
    ratio = np.maximum(d, 1).astype(np.float32) / max_exact
    large = max_exact + (np.log(ratio) / math.log(T5_MAX_DIST / max_exact) * (T5_BUCKETS - max_exact)).astype(np.int32)
    bucket = np.where(d < max_exact, d, np.minimum(large, T5_BUCKETS - 1))
    return [int(np.argmax(bucket >= b)) for b in range(T5_BUCKETS)]


T5_THRESH = _t5_thresholds()
T5_FAR_DIST = T5_THRESH[-1]


def _cparams(n_axes):
    return pltpu.CompilerParams(dimension_semantics=("arbitrary",) * n_axes,
                                vmem_limit_bytes=V7X_VMEM_LIMIT_BYTES)


def _resident(block_shape, index_map):
    return pl.BlockSpec(block_shape, index_map, pipeline_mode=pl.Buffered(1))


def _smem():
    return pl.BlockSpec(memory_space=pltpu.SMEM)


def _fill_t5_tile(bias_ref, slot, t5_ref, head, delta, tile):
    rows = 64

    def body(r, carry):
        r0 = pl.multiple_of(r * rows, rows)
        kr = lax.broadcasted_iota(I32, (rows, tile), 0) + r0
        qc = lax.broadcasted_iota(I32, (rows, tile), 1)
        d = delta + qc - kr
        val = jnp.full((rows, tile), LOG2E * t5_ref[head], F32)
        for b in range(1, T5_BUCKETS):
            val = jnp.where(d >= T5_THRESH[b], LOG2E * t5_ref[b * N_BIAS_HEADS + head], val)
        bias_ref[slot, pl.ds(r0, rows), :] = val
        return carry

    lax.fori_loop(0, tile // rows, body, 0)


def _unrolled_loop(n, unroll, body, carry):
    def group(g, c):
        for u in range(unroll):
            c = body(g * unroll + u, c)
        return c

    n_groups = n // unroll
    carry = lax.fori_loop(0, n_groups, group, carry)
    return lax.fori_loop(n_groups * unroll, n, body, carry)


def _causal_mask(tile):
    kr = lax.broadcasted_iota(I32, (tile, tile), 0)
    qc = lax.broadcasted_iota(I32, (tile, tile), 1)
    return kr <= qc


def _softmax_accumulate(s, v_aug, acc_ref, idx, *, shift=None, m_ref=None, keep=None):
    if m_ref is None:
        p = jnp.exp2(s - shift)
        acc_ref[idx] += jnp.dot(v_aug, p.astype(BF16), preferred_element_type=F32)
        return
    m_old = m_ref[idx]
    m_new = jnp.maximum(m_old, jnp.max(s, axis=0, keepdims=True))
    p = jnp.exp2(s - m_new)
    if keep is not None:
        p = jnp.where(keep, p, 0.0)
    acc_ref[idx] = jnp.exp2(m_old - m_new) * acc_ref[idx] + jnp.dot(v_aug, p.astype(BF16), preferred_element_type=F32)
    m_ref[idx] = m_new


def _logit_bound(q, k_norm_max, extra):
    qf = q.astype(F32)
    q_norm = jnp.sqrt(jnp.sum(qf * qf, axis=0, keepdims=True))
    return q_norm * (k_norm_max * BOUND_SLACK) + (extra + 1.0)


def _denominators_ok(acc_ref, n_chains):
    low = acc_ref[0, HEAD_DIM:HEAD_DIM + 1, :]
    for idx in range(1, n_chains):
        low = jnp.minimum(low, acc_ref[idx, HEAD_DIM:HEAD_DIM + 1, :])
    return jnp.min(low) > MIN_DENOMINATOR


def _two_pass_attention(run, acc_ref, m_ref, n_chains):
    acc_ref[...] = jnp.zeros(acc_ref.shape, F32)
    run(False)

    @pl.when(jnp.logical_not(_denominators_ok(acc_ref, n_chains)))
    def _():
        acc_ref[...] = jnp.zeros(acc_ref.shape, F32)
        m_ref[...] = jnp.full(m_ref.shape, NEG, F32)
        run(True)


def _normalized(acc_ref, idx):
    return acc_ref[idx, :HEAD_DIM, :] / acc_ref[idx, HEAD_DIM:HEAD_DIM + 1, :]


def _mod_kernel(c_ref, w_ref, b_ref, o_ref):
    c = c_ref[...]
    cond = c * (1.0 / (1.0 + jnp.exp(-c)))
    o_ref[...] = jnp.dot(cond.astype(BF16), w_ref[...].astype(BF16), preferred_element_type=F32) + b_ref[...]


def _modulation(c, w_ada, b_ada):
    depth, d, n = w_ada.shape
    bsz = c.shape[0]
    rows = 8
    c_pad = jnp.zeros((rows, d), F32).at[:bsz].set(c)
    tn = 1024
    out = pl.pallas_call(
        _mod_kernel,
        grid=(depth, n // tn),
        in_specs=[pl.BlockSpec((rows, d), lambda l, j: (0, 0)),
                  pl.BlockSpec((None, d, tn), lambda l, j: (l, 0, j)),
                  pl.BlockSpec((None, 1, tn), lambda l, j: (l, 0, j))],
        out_specs=pl.BlockSpec((None, rows, tn), lambda l, j: (l, 0, j)),
        out_shape=jax.ShapeDtypeStruct((depth, rows, n), F32),
        compiler_params=_cparams(2),
        name="adaln_modulation",
    )(c_pad, w_ada, b_ada.reshape(depth, 1, n))
    return out[:, :bsz]


def _rms(x):
    return x * lax.rsqrt(jnp.mean(x * x, axis=-1, keepdims=True) + EPS)


V_GROUP = N_HEADS * V_ROWS
T_QA, T_VA = 0, 2 * GROUP_W
T_QB = T_VA + V_GROUP
T_VB = T_QB + GROUP_W
T_QC = T_VB + V_GROUP
T_VC = T_QC + GROUP_W
T_GC = T_VC + V_GROUP
T_QLAT = T_GC + GROUP_W
T_KVLAT = T_QLAT + DSA_RANK
T_MISC = T_KVLAT + DSA_RANK
T_ROWS = T_MISC + 16
R_KA, R_KB, R_KC, R_KIDX, R_KVLAT, R_COLS = 0, GROUP_W, 2 * GROUP_W, 3 * GROUP_W, 3 * GROUP_W + 128, 4 * GROUP_W


def _inproj_kernel(x_ref, gain_ref, sc_ref, sh_ref, wt_ref, wr_ref, wqd_ref, wqi_ref, wvd_ref, wkd_ref, ones_ref,
                   seg_ref, qa_ref, va_ref, qb_ref, vb_ref, qc_ref, vc_ref, gc_ref, qd_ref, qi_ref, vd_ref, misc_ref,
                   ka_ref, kb_ref, kc_ref, ki_ref, kd_ref, kn_ref, ht_ref):
    x = x_ref[...]
    h = (_rms(x) * gain_ref[...]) * (1.0 + sc_ref[...]) + sh_ref[...]
    h_r = h.astype(BF16)
    ht_ref[...] = h.T.astype(BF16)
    h_t = ht_ref[...]
    ones_rows = ones_ref[...]

    def rms_t(z):
        return z * lax.rsqrt(jnp.mean(z * z, axis=0, keepdims=True) + EPS)

    def proj_t(r0, r1):
        return jnp.dot(wt_ref[r0:r1, :], h_t, preferred_element_type=F32)

    qa_ref[...] = proj_t(T_QA, T_VA).astype(BF16)
    va_ref[...] = (proj_t(T_VA, T_QB) + ones_rows).astype(BF16)
    qb_ref[...] = proj_t(T_QB, T_VB).astype(BF16)
    vb_ref[...] = (proj_t(T_VB, T_QC) + ones_rows).astype(BF16)
    qc_ref[...] = proj_t(T_QC, T_VC).astype(BF16)
    vc_ref[...] = (proj_t(T_VC, T_GC) + ones_rows).astype(BF16)
    gc_ref[...] = proj_t(T_GC, T_QLAT)
    q_lat = rms_t(proj_t(T_QLAT, T_KVLAT)).astype(BF16)
    kv_lat_t = rms_t(proj_t(T_KVLAT, T_MISC)).astype(BF16)
    qd_ref[...] = jnp.dot(wqd_ref[...], q_lat, preferred_element_type=F32).astype(BF16)
    qi_ref[...] = jnp.dot(wqi_ref[...], q_lat, preferred_element_type=F32).astype(BF16)
    vd_ref[...] = (jnp.dot(wvd_ref[...], kv_lat_t, preferred_element_type=F32) + ones_rows).astype(BF16)
    misc_ref[...] = proj_t(T_MISC, T_ROWS)

    p_r = jnp.dot(h_r, wr_ref[...], preferred_element_type=F32)
    kv_lat_r = _rms(p_r[:, R_KVLAT:R_COLS]).astype(BF16)
    keys = [p_r[:, R_KA:R_KB].astype(BF16), p_r[:, R_KB:R_KC].astype(BF16), p_r[:, R_KC:R_KIDX].astype(BF16),
            jnp.dot(kv_lat_r, wkd_ref[...], preferred_element_type=F32).astype(BF16)]
    for k, out_ref in zip(keys[:3], (ka_ref, kb_ref, kc_ref)):
        for hd in range(N_HEADS):
            out_ref[hd] = k[:, hd * HEAD_DIM:(hd + 1) * HEAD_DIM]
    kd_ref[...] = keys[3]
    ki_ref[...] = p_r[:, R_KIDX:R_KIDX + IDX_DIM].astype(BF16)

    norms = []
    for k in keys:
        kf = k.astype(F32)
        sq = jnp.dot((kf * kf).astype(BF16), seg_ref[...], preferred_element_type=F32)
        norms.append(jnp.max(sq, axis=0, keepdims=True))
    kn_ref[...] = jnp.concatenate(norms + [jnp.zeros((8 - len(norms), 128), F32)], axis=0)


def _head_rows(w_cols, rows_per_head):
    d = w_cols.shape[0]
    w = w_cols.T.reshape(N_HEADS, HEAD_DIM, d)
    return jnp.concatenate([w, jnp.zeros((N_HEADS, rows_per_head - HEAD_DIM, d), w.dtype)], axis=1)


def _input_projection(x, gain, sc, sh, w_in, w_dq_up, w_didx_q, w_dkv_up):
    bsz, seq, d = x.shape
    w = GROUP_W
    o_fc = 9 * w
    o_gc = o_fc + N_HEADS
    o_ql = o_gc + w
    o_kv = o_ql + DSA_RANK
    o_ki = o_kv + DSA_RANK
    o_wi = o_ki + IDX_DIM
    q_scale_a = DIFF_HALF ** -0.5 * LOG2E
    q_scale = HEAD_DIM ** -0.5 * LOG2E
    wqa = (w_in[:, :w] * q_scale_a).T.reshape(N_HEADS, 2, DIFF_HALF, d)
    zero = jnp.zeros((N_HEADS, DIFF_HALF, d), w_in.dtype)
    wqa = jnp.stack([jnp.concatenate([wqa[:, 0], zero], axis=1), jnp.concatenate([zero, wqa[:, 1]], axis=1)], axis=1)
    values = lambda c0: _head_rows(w_in[:, c0:c0 + w], V_ROWS).reshape(V_GROUP, d)
    small = jnp.zeros((16, d), w_in.dtype)
    small = small.at[:IDX_HEADS].set((w_in[:, o_wi:o_wi + IDX_HEADS] * (IDX_HEADS ** -0.5 * IDX_DIM ** -0.5)).T)
    small = small.at[8:8 + N_HEADS].set(w_in[:, o_fc:o_gc].T)
    wt = jnp.concatenate([wqa.reshape(2 * w, d), values(2 * w),
                          (w_in[:, 3 * w:4 * w] * q_scale).T, values(5 * w),
                          (w_in[:, 6 * w:7 * w] * q_scale).T, values(8 * w),
                          w_in[:, o_gc:o_ql].T, w_in[:, o_ql:o_ki].T, small], axis=0).astype(BF16)
    assert wt.shape[0] == T_ROWS
    wr = jnp.concatenate([w_in[:, w:2 * w], w_in[:, 4 * w:5 * w], w_in[:, 7 * w:8 * w], w_in[:, o_ki:o_wi],
                          jnp.zeros((d, 128 - IDX_DIM), w_in.dtype), w_in[:, o_kv:o_ki]], axis=1).astype(BF16)
    assert wr.shape[1] == R_COLS
    wqd = (w_dq_up * q_scale).T.reshape(N_HEADS, HEAD_DIM, DSA_RANK)
    wqd = jnp.stack([jnp.zeros((N_HEADS, HEAD_DIM, DSA_RANK), wqd.dtype).at[hd].set(wqd[hd]).reshape(w, DSA_RANK)
                     for hd in range(N_HEADS)], axis=0).reshape(N_HEADS * w, DSA_RANK).astype(BF16)
    wqi = w_didx_q.T.astype(BF16)
    wvd = _head_rows(w_dkv_up[:, w:], V_ROWS).reshape(V_GROUP, DSA_RANK).astype(BF16)
    wkd = w_dkv_up[:, :w].astype(BF16)
    ones_rows = jnp.zeros((N_HEADS, V_ROWS, 1), F32).at[:, HEAD_DIM].set(1.0).reshape(V_GROUP, 1)
    seg = (jnp.arange(w)[:, None] // HEAD_DIM == jnp.arange(128)[None, :]).astype(BF16)

    tm = ROW_TILE
    nt = seq // tm
    feat = lambda rows: pl.BlockSpec((None, rows, tm), lambda b, i: (b, 0, i))
    keyh = pl.BlockSpec((None, N_HEADS, tm, HEAD_DIM), lambda b, i: (b, 0, i, 0))
    tok = lambda width: pl.BlockSpec((None, tm, width), lambda b, i: (b, i, 0))
    vec = pl.BlockSpec((None, 1, d), lambda b, i: (b, 0, 0))
    whole = lambda a: _resident(a.shape, lambda b, i: (0,) * a.ndim)
    f_bf = lambda rows: jax.ShapeDtypeStruct((bsz, rows, seq), BF16)
    k_bf = jax.ShapeDtypeStruct((bsz, N_HEADS, seq, HEAD_DIM), BF16)
    outs = pl.pallas_call(
        _inproj_kernel,
        grid=(bsz, nt),
        in_specs=[tok(d), pl.BlockSpec((1, d), lambda b, i: (0, 0)), vec, vec,
                  whole(wt), whole(wr), whole(wqd), whole(wqi), whole(wvd), whole(wkd), whole(ones_rows), whole(seg)],
        out_specs=[feat(2 * w), feat(V_GROUP), feat(w), feat(V_GROUP), feat(w), feat(V_GROUP), feat(w),
                   feat(N_HEADS * w), feat(w), feat(V_GROUP), feat(16),
                   keyh, keyh, keyh, tok(IDX_DIM), tok(w),
                   pl.BlockSpec((None, None, 8, 128), lambda b, i: (b, i, 0, 0))],
        out_shape=[f_bf(2 * w), f_bf(V_GROUP), f_bf(w), f_bf(V_GROUP), f_bf(w), f_bf(V_GROUP),
                   jax.ShapeDtypeStruct((bsz, w, seq), F32),
                   f_bf(N_HEADS * w), f_bf(w), f_bf(V_GROUP), jax.ShapeDtypeStruct((bsz, 16, seq), F32),
                   k_bf, k_bf, k_bf, jax.ShapeDtypeStruct((bsz, seq, IDX_DIM), BF16),
                   jax.ShapeDtypeStruct((bsz, seq, w), BF16),
                   jax.ShapeDtypeStruct((bsz, nt, 8, 128), F32)],
        scratch_shapes=[pltpu.VMEM((d, tm), BF16)],
        compiler_params=_cparams(2),
        name="input_projection",
    )(x, gain.reshape(1, d), sc, sh, wt, wr, wqd, wqi, wvd, wkd, ones_rows, seg)
    k_norm_max = jnp.sqrt(jnp.max(outs[-1], axis=1))[:, :4, :N_HEADS]
    return outs[:-1], k_norm_max


def _diff_kernel(t5_ref, bmax_ref, kmax_ref, lam_ref, q_ref, k_ref, v_ref, g_ref, o_ref, bias_ref, m_ref, acc_ref,
                 *, tile, head0, heads, group_heads, out_scale):
    b = pl.program_id(0)
    hg = pl.program_id(1)
    i = pl.program_id(2)

    @pl.when(i == 0)
    def _():
        for h in range(heads):
            _fill_t5_tile(bias_ref, 2 * h, t5_ref, head0 + hg * heads + h, 0, tile)
            _fill_t5_tile(bias_ref, 2 * h + 1, t5_ref, head0 + hg * heads + h, tile, tile)

    far_bias = [LOG2E * t5_ref[(T5_BUCKETS - 1) * N_BIAS_HEADS + head0 + hg * heads + h] for h in range(heads)]
    bounds = [_logit_bound(q_ref[h, c], kmax_ref[b * group_heads + hg * heads + h], bmax_ref[head0 + hg * heads + h])
              for h in range(heads) for c in range(2)]

    def run(online):
        def step(j, kind):
            k0 = pl.multiple_of(j * tile, tile)
            logits = [jnp.dot(k_ref[h, pl.ds(k0, tile), :], q_ref[h, c], preferred_element_type=F32)
                      for h in range(heads) for c in range(2)]
            for h in range(heads):
                vt = v_ref[h, :, pl.ds(k0, tile)]
                for c in range(2):
                    idx = 2 * h + c
                    s, shift = logits[idx], bounds[idx]
                    if kind != "far":
                        s = s + bias_ref[2 * h + (1 if kind == "near" else 0)]
                    elif online:
                        s = s + far_bias[h]
                    else:
                        shift = shift - far_bias[h]
                    if kind == "diag":
                        s = jnp.where(_causal_mask(tile), s, NEG)
                    _softmax_accumulate(s, vt, acc_ref, idx, shift=shift, m_ref=m_ref if online else None)

        step(i, "diag")

        @pl.when(i >= 1)
        def _():
            step(i - 1, "near")

        def far(j, carry):
            step(j, "far")
            return carry

        _unrolled_loop(jnp.maximum(i - 1, 0), 2, far, 0)

    _two_pass_attention(run, acc_ref, m_ref, 2 * heads)

    for h in range(heads):
        o = _normalized(acc_ref, 2 * h) - lam_ref[0] * _normalized(acc_ref, 2 * h + 1)
        y = o * lax.rsqrt(jnp.mean(o * o, axis=0, keepdims=True) + EPS)
        o_ref[h] = ((y * g_ref[...]) * out_scale).astype(o_ref.dtype)


def _diff_attention(q_t, k, v_aug, lam, subln_gain, t5_flat, bias_max, k_norm_max, out_scale):
    bsz, nh, _, dh, seq = q_t.shape
    tile = min(ATTN_TILE, seq)
    hps = DIFF_HEADS_PER_STEP
    assert tile >= T5_FAR_DIST and seq % tile == 0 and nh % hps == 0
    kern = functools.partial(_diff_kernel, tile=tile, head0=0, heads=hps, group_heads=nh, out_scale=out_scale)
    return pl.pallas_call(
        kern,
        grid=(bsz, nh // hps, seq // tile),
        in_specs=[_smem(), _smem(), _smem(), _smem(),
                  pl.BlockSpec((None, hps, 2, dh, tile), lambda b, h, i: (b, h, 0, 0, i)),
                  _resident((None, hps, seq, dh), lambda b, h, i: (b, h, 0, 0)),
                  _resident((None, hps, V_ROWS, seq), lambda b, h, i: (b, h, 0, 0)),
                  pl.BlockSpec((dh, 1), lambda b, h, i: (0, 0))],
        out_specs=pl.BlockSpec((None, hps, dh, tile), lambda b, h, i: (b, h, 0, i)),
        out_shape=jax.ShapeDtypeStruct((bsz, nh, dh, seq), BF16),
        scratch_shapes=[pltpu.VMEM((2 * hps, tile, tile), F32),
                        pltpu.VMEM((2 * hps, 1, tile), F32),
                        pltpu.VMEM((2 * hps, V_ROWS, tile), F32)],
        compiler_params=_cparams(3),
        name="diff_attention",
    )(t5_flat, bias_max, k_norm_max, lam.reshape(1), q_t, k, v_aug, subln_gain.reshape(dh, 1))


def _kmean_kernel(k_ref, o_ref, *, blocks):
    k = k_ref[...].astype(F32)
    o_ref[...] = jnp.mean(k.reshape(blocks, MOBA_BLOCK, k.shape[-1]), axis=1)


def _moba_kmean(k):
    bsz, nh, seq, dh = k.shape
    nb = seq // MOBA_BLOCK
    blocks = min(8, nb)
    return pl.pallas_call(
        functools.partial(_kmean_kernel, blocks=blocks),
        grid=(bsz, nh, nb // blocks),
        in_specs=[pl.BlockSpec((None, None, blocks * MOBA_BLOCK, dh), lambda b, h, i: (b, h, i, 0))],
        out_specs=pl.BlockSpec((None, None, blocks, dh), lambda b, h, i: (b, h, i, 0)),
        out_shape=jax.ShapeDtypeStruct((bsz, nh, nb, dh), F32),
        compiler_params=_cparams(3),
    )(k)


def _moba_kernel(t5_ref, bmax_ref, kmax_ref, q_ref, k_ref, v_ref, km_ref, o_ref, bias_ref, sel_ref, m_ref, acc_ref,
                 *, tile, head0, heads, n_blocks):
    b = pl.program_id(0)
    i = pl.program_id(1)

    @pl.when(i == 0)
    def _():
        for h in range(heads):
            _fill_t5_tile(bias_ref, 2 * h, t5_ref, head0 + h, 0, tile)
            _fill_t5_tile(bias_ref, 2 * h + 1, t5_ref, head0 + h, tile, tile)

    far_bias = [LOG2E * t5_ref[(T5_BUCKETS - 1) * N_BIAS_HEADS + head0 + h] for h in range(heads)]
    bounds = [_logit_bound(q_ref[h], kmax_ref[b * heads + h], bmax_ref[head0 + h]) for h in range(heads)]

    nidx = lax.broadcasted_iota(I32, (n_blocks, tile), 0)
    for h in range(heads):
        gate = jnp.dot(km_ref[h], q_ref[h], preferred_element_type=F32)
        g = jnp.where(nidx < i, gate, NEG)
        sel = jnp.zeros(gate.shape, F32)
        for _ in range(min(MOBA_TOPK, n_blocks)):
            mx = jnp.max(g, axis=0, keepdims=True)
            first = jnp.min(jnp.where(g == mx, nidx, n_blocks), axis=0, keepdims=True)
            pick = nidx == first
            sel = jnp.where(pick, 1.0, sel)
            g = jnp.where(pick, -jnp.inf, g)
        sel_ref[h] = jnp.where(nidx < i, sel, 0.0)

    def run(online):
        def step(j, kind):
            k0 = pl.multiple_of(j * tile, tile)
            logits = [jnp.dot(k_ref[h, pl.ds(k0, tile), :], q_ref[h], preferred_element_type=F32)
                      for h in range(heads)]
            for h in range(heads):
                s, shift = logits[h], bounds[h]
                if kind != "far":
                    s = s + bias_ref[2 * h + (1 if kind == "near" else 0)]
                elif online:
                    s = s + far_bias[h]
                else:
                    shift = shift - far_bias[h]
                mask = _causal_mask(tile) if kind == "diag" else sel_ref[h, pl.ds(j, 1), :] > 0.5
                s = jnp.where(mask, s, NEG)
                _softmax_accumulate(s, v_ref[h, :, pl.ds(k0, tile)], acc_ref, h, shift=shift,
                                    m_ref=m_ref if online else None)

        step(i, "diag")

        @pl.when(i >= 1)
        def _():
            step(i - 1, "near")

        def far(j, carry):
            step(j, "far")
            return carry

        _unrolled_loop(jnp.maximum(i - 1, 0), 4, far, 0)

    _two_pass_attention(run, acc_ref, m_ref, heads)
    for h in range(heads):
        o_ref[h] = _normalized(acc_ref, h).astype(o_ref.dtype)


def _moba_attention(q_t, k, v_aug, kmean, t5_flat, bias_max, k_norm_max):
    bsz, nh, dh, seq = q_t.shape
    tile = MOBA_BLOCK
    assert SPARSE_TILE == MOBA_BLOCK and seq % tile == 0 and tile >= T5_FAR_DIST
    nb = seq // tile
    kern = functools.partial(_moba_kernel, tile=tile, head0=N_HEADS, heads=nh, n_blocks=nb)
    return pl.pallas_call(
        kern,
        grid=(bsz, nb),
        in_specs=[_smem(), _smem(), _smem(),
                  pl.BlockSpec((None, nh, dh, tile), lambda b, i: (b, 0, 0, i)),
                  _resident((None, nh, seq, dh), lambda b, i: (b, 0, 0, 0)),
                  _resident((None, nh, V_ROWS, seq), lambda b, i: (b, 0, 0, 0)),
                  _resident((None, nh, nb, dh), lambda b, i: (b, 0, 0, 0))],
        out_specs=pl.BlockSpec((None, nh, dh, tile), lambda b, i: (b, 0, 0, i)),
        out_shape=jax.ShapeDtypeStruct((bsz, nh, dh, seq), BF16),
        scratch_shapes=[pltpu.VMEM((2 * nh, tile, tile), F32), pltpu.VMEM((nh, nb, tile), F32),
                        pltpu.VMEM((nh, 1, tile), F32),
                        pltpu.VMEM((nh, V_ROWS, tile), F32)],
        compiler_params=_cparams(2),
        name="moba_attention",
    )(t5_flat, bias_max, k_norm_max, q_t, k, v_aug, kmean)


def _split3(x):
    def trunc(v):
        bits = lax.bitcast_convert_type(v, I32)
        return lax.bitcast_convert_type(bits & jnp.int32(-65536), F32)
    hi = trunc(x)
    r1 = x - hi
    mid = trunc(r1)
    lo = trunc(r1 - mid)
    return hi, mid, lo


def _forget_cumsum_kernel(f_ref, hi_ref, mid_ref, lo_ref):
    x = f_ref[...]
    rows = x.shape[0]
    ls = jnp.minimum(x, 0.0) - jnp.log(1.0 + jnp.exp(-jnp.abs(x)))
    upper = (lax.broadcasted_iota(I32, (128, 128), 0) <= lax.broadcasted_iota(I32, (128, 128), 1)).astype(F32)
    within = jnp.dot(ls, upper, preferred_element_type=F32, precision=lax.Precision.HIGHEST)
    strict = (lax.broadcasted_iota(I32, (rows, rows), 1) < lax.broadcasted_iota(I32, (rows, rows), 0)).astype(F32)
    before = jnp.dot(strict, within, preferred_element_type=F32, precision=lax.Precision.HIGHEST)
    cf = (within + before[:, 127:128]) * LOG2E
    hi, mid, lo = _split3(cf)
    hi_ref[...] = hi
    mid_ref[...] = mid
    lo_ref[...] = lo


def _forget_cumsum(f_logit):
    bsz, nh, seq = f_logit.shape
    rows = seq // 128
    f = f_logit.reshape(bsz, nh, rows, 128)
    spec = pl.BlockSpec((None, None, rows, 128), lambda b, h: (b, h, 0, 0))
    parts = pl.pallas_call(
        _forget_cumsum_kernel,
        grid=(bsz, nh),
        in_specs=[spec],
        out_specs=[spec] * 3,
        out_shape=[jax.ShapeDtypeStruct((bsz, nh, rows, 128), F32)] * 3,
        compiler_params=_cparams(2),
        name="forget_cumsum",
    )(f)
    return [p.reshape(bsz, nh, seq) for p in parts]


def _forget_kernel(kmax_ref, q_ref, k_ref, v_ref, gate_ref, o_ref, m_ref, acc_ref, *, tile, heads, group_heads):
    b = pl.program_id(0)
    hg = pl.program_id(1)
    i = pl.program_id(2)
    bounds = [_logit_bound(q_ref[h, :HEAD_DIM, :], kmax_ref[b * group_heads + hg * heads + h], 0.0)
              for h in range(heads)]

    def run(online):
        def step(j, blocks, mask):
            k0 = pl.multiple_of(j * tile, tile)
            logits = [jnp.dot(k_ref[h, pl.ds(k0, blocks * tile), :], q_ref[h], preferred_element_type=F32)
                      for h in range(heads)]
            for h in range(heads):
                s = logits[h] if mask is None else jnp.where(mask, logits[h], NEG)
                _softmax_accumulate(s, v_ref[h, :, pl.ds(k0, blocks * tile)], acc_ref, h, shift=bounds[h],
                                    m_ref=m_ref if online else None)

        step(i, 1, _causal_mask(tile))
        n_big = i // FORGET_BLOCKS_PER_STEP

        def past_big(j, carry):
            step(j * FORGET_BLOCKS_PER_STEP, FORGET_BLOCKS_PER_STEP, None)
            return carry

        def past_single(j, carry):
            step(j, 1, None)
            return carry

        lax.fori_loop(0, n_big, past_big, 0)
        lax.fori_loop(n_big * FORGET_BLOCKS_PER_STEP, i, past_single, 0)

    _two_pass_attention(run, acc_ref, m_ref, heads)
    for h in range(heads):
        g = gate_ref[h]
        o_ref[h] = (_normalized(acc_ref, h) * (1.0 / (1.0 + jnp.exp(-g)))).astype(o_ref.dtype)


def _forgetting_attention(q_aug_t, k_aug, v_aug, gate_t, k_norm_max):
    bsz, nh, kdim, seq = q_aug_t.shape
    dh = gate_t.shape[2]
    tile = min(ATTN_TILE, seq)
    hps = FORGET_HEADS_PER_STEP
    assert seq % tile == 0 and nh % hps == 0
    return pl.pallas_call(
        functools.partial(_forget_kernel, tile=tile, heads=hps, group_heads=nh),
        grid=(bsz, nh // hps, seq // tile),
        in_specs=[_smem(),
                  pl.BlockSpec((None, hps, kdim, tile), lambda b, h, i: (b, h, 0, i)),
                  _resident((None, hps, seq, kdim), lambda b, h, i: (b, h, 0, 0)),
                  _resident((None, hps, V_ROWS, seq), lambda b, h, i: (b, h, 0, 0)),
                  pl.BlockSpec((None, hps, dh, tile), lambda b, h, i: (b, h, 0, i))],
        out_specs=pl.BlockSpec((None, hps, dh, tile), lambda b, h, i: (b, h, 0, i)),
        out_shape=jax.ShapeDtypeStruct((bsz, nh, dh, seq), BF16),
        scratch_shapes=[pltpu.VMEM((hps, 1, tile), F32),
                        pltpu.VMEM((hps, V_ROWS, tile), F32)],
        compiler_params=_cparams(3),
        name="forget_attention",
    )(k_norm_max, q_aug_t, k_aug, v_aug, gate_t)


def _key_to_float(u):
    ks = u ^ jnp.int32(-2147483648)
    bits = jnp.where(ks < 0, ks ^ jnp.int32(2147483647), ks)
    return lax.bitcast_convert_type(bits, F32)


def _truncate_to_bf16(x):
    bits = lax.bitcast_convert_type(x, I32) & jnp.int32(-65536)
    return lax.bitcast_convert_type(bits, F32).astype(BF16)


def _dsa_kernel(t5_ref, bmax_ref, kmax_ref, qi_ref, w_ref, ki_ref, q_ref, k_ref, v_ref, o_ref,
                sc_ref, hi_ref, bias_ref, tri_ref, m_ref, acc_ref, *, tile, head0, topk):
    b = pl.program_id(0)
    i = pl.program_id(1)

    @pl.when((b == 0) & (i == 0))
    def _():
        for h in range(N_HEADS):
            _fill_t5_tile(bias_ref, 2 * h, t5_ref, head0 + h, 0, tile)
            _fill_t5_tile(bias_ref, 2 * h + 1, t5_ref, head0 + h, tile, tile)
        tri_ref[...] = jnp.where(lax.broadcasted_iota(I32, (tile, tile), 1) <= lax.broadcasted_iota(I32, (tile, tile), 0),
                                 1.0, 0.0).astype(BF16)

    causal = _causal_mask(tile)

    def index_scores(j):
        k0 = pl.multiple_of(j * tile, tile)
        kt = ki_ref[pl.ds(k0, tile), :]
        raws = [jnp.dot(kt, qi_ref[h * IDX_DIM:(h + 1) * IDX_DIM, :], preferred_element_type=F32)
                for h in range(IDX_HEADS)]
        sc = jnp.zeros((tile, tile), F32)
        for h in range(IDX_HEADS):
            sc = sc + jnp.maximum(raws[h], 0.0) * w_ref[h:h + 1, :]
        return k0, sc

    def store_scores(k0, sc):
        sc_ref[pl.ds(k0, tile), :] = sc
        hi_ref[pl.ds(k0, tile), :] = _truncate_to_bf16(sc)

    def score_body(j, carry):
        store_scores(*index_scores(j))
        return carry

    _unrolled_loop(i, 4, score_body, 0)
    k0, sc = index_scores(i)
    store_scores(k0, jnp.where(causal, sc, NEG))

    def count(pred):
        def body(j, cnt):
            k0 = pl.multiple_of(j * tile, tile)
            ind = jnp.where(pred(sc_ref[pl.ds(k0, tile), :]), 1.0, 0.0)
            return cnt + jnp.sum(ind.reshape(tile // 8, 8, tile), axis=0)
        cnt = _unrolled_loop(i + 1, 4, body, jnp.zeros((8, tile), F32))
        return jnp.sum(cnt, axis=0, keepdims=True)

    def count_truncated(thr):
        def body(j, cnt):
            k0 = pl.multiple_of(j * tile, tile)
            ind = jnp.where(hi_ref[pl.ds(k0, tile), :] >= thr, jnp.ones((), BF16), jnp.zeros((), BF16))
            part = ind[0:BF16_ROWS]
            for r in range(1, tile // BF16_ROWS):
                part = part + ind[r * BF16_ROWS:(r + 1) * BF16_ROWS]
            return cnt + part.astype(F32)
        cnt = _unrolled_loop(i + 1, 4, body, jnp.zeros((BF16_ROWS, tile), F32))
        return jnp.sum(cnt, axis=0, keepdims=True)

    def high_bit(it, ans):
        cand = ans | jnp.left_shift(jnp.int32(1), 31 - it)
        return jnp.where(count_truncated(_truncate_to_bf16(_key_to_float(cand))) >= topk, cand, ans)

    ans = lax.fori_loop(0, 16, high_bit, jnp.zeros((1, tile), I32))

    above = count(lambda s: s > _key_to_float(ans))
    settled = above < topk
    need = jnp.where(settled, topk - above, 0.0)

    def low_bit(state):
        it, ans, open_f, need = state
        cand = ans | jnp.left_shift(jnp.int32(1), 31 - it)
        thr = _key_to_float(cand)
        cnt = count(lambda s: s >= thr)
        unsettled = open_f > 0.5
        ans = jnp.where(unsettled & (cnt >= topk), cand, ans)
        exact = unsettled & (cnt == topk)
        return it + 1, ans, jnp.where(exact, 0.0, open_f), jnp.where(exact, TAKE_ALL_TIES, need)

    _, ans, open_f, need = lax.while_loop(lambda st: (st[0] < 32) & (jnp.max(st[2]) > 0.5), low_bit,
                                          (jnp.int32(16), ans, jnp.where(settled, 0.0, 1.0), need))
    tau = _key_to_float(ans)
    need = jnp.where(open_f > 0.5, topk - count(lambda s: s > tau), need)

    qs = [q_ref[h] for h in range(N_HEADS)]

    far_bias = [LOG2E * t5_ref[(T5_BUCKETS - 1) * N_BIAS_HEADS + head0 + h] for h in range(N_HEADS)]
    bounds = [_logit_bound(qs[h], kmax_ref[b * N_HEADS + h], bmax_ref[head0 + h]) for h in range(N_HEADS)]

    def run(online):
        def attend(j, ties_seen, kind):
            k0 = pl.multiple_of(j * tile, tile)
            sc = sc_ref[pl.ds(k0, tile), :]
            eq = sc == tau
            eqf = jnp.where(eq, 1.0, 0.0)
            rank = ties_seen + jnp.dot(tri_ref[...], eqf.astype(BF16), preferred_element_type=F32)
            wgt = jnp.where(eq, jnp.where(rank <= need, 1.0, 0.0), jnp.where(sc > tau, 1.0, 0.0))
            if kind == "diag":
                wgt = jnp.where(causal, wgt, 0.0)
            keep = wgt > 0.5
            kt = k_ref[pl.ds(k0, tile), :]
            logits = [jnp.dot(kt, qs[h], preferred_element_type=F32) for h in range(N_HEADS)]
            for h in range(N_HEADS):
                s, shift = logits[h], bounds[h]
                if kind != "far":
                    s = s + bias_ref[2 * h + (1 if kind == "near" else 0)]
                elif online:
                    s = s + far_bias[h]
                else:
                    shift = shift - far_bias[h]
                s = jnp.where(keep, s, NEG)
                _softmax_accumulate(s, v_ref[h, :, pl.ds(k0, tile)], acc_ref, h, shift=shift,
                                    m_ref=m_ref if online else None, keep=keep)
            return ties_seen + jnp.sum(eqf, axis=0, keepdims=True)

        seen = _unrolled_loop(jnp.maximum(i - 1, 0), 4, lambda j, r: attend(j, r, "far"), jnp.zeros((1, tile), F32))
        seen = lax.cond(i >= 1, lambda r: attend(i - 1, r, "near"), lambda r: r, seen)
        attend(i, seen, "diag")

    _two_pass_attention(run, acc_ref, m_ref, N_HEADS)
    for h in range(N_HEADS):
        o_ref[h * HEAD_DIM:(h + 1) * HEAD_DIM, :] = _normalized(acc_ref, h).astype(o_ref.dtype)


def _dsa_attention(qi_t, w_t, k_idx, q_t, k, v_aug, t5_flat, bias_max, k_norm_max):
    bsz, _, width, seq = q_t.shape
    tile = min(SPARSE_TILE, seq)
    topk = min(DSA_TOPK_MAX, seq // 4)
    assert seq % tile == 0 and tile >= topk and tile >= T5_FAR_DIST
    kern = functools.partial(_dsa_kernel, tile=tile, head0=2 * N_HEADS, topk=topk)
    return pl.pallas_call(
        kern,
        grid=(bsz, seq // tile),
        in_specs=[_smem(), _smem(), _smem(),
                  pl.BlockSpec((None, qi_t.shape[1], tile), lambda b, i: (b, 0, i)),
                  pl.BlockSpec((None, 8, tile), lambda b, i: (b, 0, i)),
                  _resident((None, seq, k_idx.shape[2]), lambda b, i: (b, 0, 0)),
                  pl.BlockSpec((None, N_HEADS, width, tile), lambda b, i: (b, 0, 0, i)),
                  _resident((None, seq, width), lambda b, i: (b, 0, 0)),
                  _resident((None, N_HEADS, V_ROWS, seq), lambda b, i: (b, 0, 0, 0))],
        out_specs=pl.BlockSpec((None, width, tile), lambda b, i: (b, 0, i)),
        out_shape=jax.ShapeDtypeStruct((bsz, width, seq), BF16),
        scratch_shapes=[pltpu.VMEM((seq, tile), F32), pltpu.VMEM((seq, tile), BF16),
                        pltpu.VMEM((2 * N_HEADS, tile, tile), F32),
                        pltpu.VMEM((tile, tile), BF16),
                        pltpu.VMEM((N_HEADS, 1, tile), F32),
                        pltpu.VMEM((N_HEADS, V_ROWS, tile), F32)],
        compiler_params=_cparams(2),
        name="dsa_attention",
    )(t5_flat, bias_max, k_norm_max, qi_t, w_t, k_idx, q_t, k, v_aug)


def _outproj_kernel(x_ref, oa_ref, ob_ref, oc_ref, od_ref, w_ref, g_ref, y_ref):
    o_t = jnp.concatenate([oa_ref[...], ob_ref[...], oc_ref[...], od_ref[...]], axis=0)
    y = lax.dot_general(o_t, w_ref[...], (((0,), (0,)), ((), ())), preferred_element_type=F32)
    y_ref[...] = x_ref[...] + g_ref[...] * y


def _output_projection(x, o_groups, w_out, gate):
    bsz, seq, d = x.shape
    tm = ROW_TILE
    row = pl.BlockSpec((None, tm, d), lambda b, i: (b, i, 0))
    grp = pl.BlockSpec((None, GROUP_W, tm), lambda b, i: (b, 0, i))
    return pl.pallas_call(
        _outproj_kernel,
        grid=(bsz, seq // tm),
        in_specs=[row, grp, grp, grp, grp,
                  _resident(w_out.shape, lambda b, i: (0, 0)),
                  pl.BlockSpec((None, 1, d), lambda b, i: (b, 0, 0))],
        out_specs=row,
        out_shape=jax.ShapeDtypeStruct(x.shape, F32),
        compiler_params=_cparams(2),
        name="output_projection",
    )(x, *o_groups, w_out.astype(BF16), gate)


HALO = 8


def _ffn_kernel(x_ref, xp_ref, gain_ref, sc_ref, sh_ref, g_ref, wup_ref, cw_ref, cb_ref, wd_ref, fg_ref,
                y_ref, h_ref, acc_ref, *, tm, chunk, n_chunks, final_norm):
    i = pl.program_id(1)

    def modulated(x):
        return ((_rms(x) * gain_ref[...]) * (1.0 + sc_ref[...]) + sh_ref[...]).astype(BF16)

    h_ref[HALO:, :] = modulated(x_ref[...])
    halo = modulated(xp_ref[...])
    h_ref[:HALO, :] = jnp.where(i > 0, halo, jnp.zeros_like(halo))
    acc_ref[...] = jnp.zeros(acc_ref.shape, F32)
    h = h_ref[...]

    def up(c0, half):
        col = pl.multiple_of(half * D_FF + c0, 128)
        return jnp.dot(h, wup_ref[:, pl.ds(col, chunk)], preferred_element_type=F32)

    def conv(u, c0, half):
        cw = cw_ref[half, :, pl.ds(c0, chunk)]
        return (cw[0:1] * u[HALO - 2:HALO - 2 + tm] + cw[1:2] * u[HALO - 1:HALO - 1 + tm]
                + cw[2:3] * u[HALO:HALO + tm]) + cb_ref[half, :, pl.ds(c0, chunk)]

    def chunks(first, count):
        c0s = [pl.multiple_of((first + n) * chunk, chunk) for n in range(count)]
        ups = [(up(c0, 0), up(c0, 1)) for c0 in c0s]
        acts = []
        for c0, (ug, uv) in zip(c0s, ups):
            gate = conv(ug, c0, 0)
            acts.append(((gate * (1.0 / (1.0 + jnp.exp(-gate)))) * conv(uv, c0, 1)).astype(BF16))
        a = acts[0] if count == 1 else jnp.concatenate(acts, axis=1)
        acc_ref[...] += jnp.dot(a, wd_ref[pl.ds(c0s[0], count * chunk), :], preferred_element_type=F32)

    def pair(p, carry):
        chunks(2 * p, 2)
        return carry

    lax.fori_loop(0, n_chunks // 2, pair, 0)
    if n_chunks % 2:
        chunks(n_chunks - 1, 1)

    y = x_ref[...] + g_ref[...] * acc_ref[...]
    if final_norm:
        y = _rms(y) * fg_ref[...]
    y_ref[...] = y


def _ffn(x, gain, sc, sh, gate, w_up, conv_w, conv_b, w_down, final_gain, final_norm):
    bsz, seq, d = x.shape
    tm = min(FFN_ROW_TILE, seq)
    fc = FFN_CHUNK
    n_chunks = D_FF // fc
    w_up_b = w_up.astype(BF16)
    cw = conv_w.reshape(3, 2, D_FF).transpose(1, 0, 2)
    cb = conv_b.reshape(2, 1, D_FF)
    vec = pl.BlockSpec((None, 1, d), lambda b, i: (b, 0, 0))
    one = pl.BlockSpec((1, d), lambda b, i: (0, 0))
    row = pl.BlockSpec((None, tm, d), lambda b, i: (b, i, 0))
    whole = lambda a: _resident(a.shape, lambda b, i: (0,) * a.ndim)
    w_down_b = w_down.astype(BF16)
    return pl.pallas_call(
        functools.partial(_ffn_kernel, tm=tm, chunk=fc, n_chunks=n_chunks, final_norm=final_norm),
        grid=(bsz, seq // tm),
        in_specs=[row,
                  pl.BlockSpec((None, HALO, d), lambda b, i: (b, jnp.maximum(i * (tm // HALO) - 1, 0), 0)),
                  one, vec, vec, vec, whole(w_up_b), whole(cw), whole(cb), whole(w_down_b), one],
        out_specs=row,
        out_shape=jax.ShapeDtypeStruct(x.shape, F32),
        scratch_shapes=[pltpu.VMEM((HALO + tm, d), BF16), pltpu.VMEM((tm, d), F32)],
        compiler_params=_cparams(2),
        name="conv_glu_ffn",
    )(x, x, gain.reshape(1, d), sc, sh, gate, w_up_b, cw, cb, w_down_b, final_gain.reshape(1, d))


def kernel(x, c, w_ada, b_ada, norm1_gain, w_in, f_bias, diff_lambda, diff_subln_gain, w_dq_up, w_didx_q,
           w_dkv_up, w_out, t5_table, norm2_gain, w_ffn_up, ffn_conv_w, ffn_conv_b, w_ffn_down, final_gain):
    bsz, seq, d = x.shape
    depth = w_ada.shape[0]
    t5_flat = t5_table.reshape(-1)
    bias_max = jnp.max(t5_table, axis=0) * LOG2E
    mod = _modulation(c, w_ada, b_ada)
    w = GROUP_W

    for l in range(depth):
        sh1, sc1, g1, sh2, sc2, g2 = [m[:, None, :] for m in jnp.split(mod[l], 6, axis=-1)]
        (qa, va, qb, vb, qc, vc, gc, qd, qi_t, vd, misc, ka, kb, kc, k_idx, kd), k_norm_max = _input_projection(
            x, norm1_gain[l], sc1, sh1, w_in[l], w_dq_up[l], w_didx_q[l], w_dkv_up[l])
        heads_t = lambda t, rows: t.reshape(bsz, N_HEADS, rows, seq)
        kmax = [k_norm_max[:, g].reshape(-1) for g in range(4)]

        lambda_init = 0.8 - 0.6 * math.exp(-0.3 * l)
        lq1, lk1, lq2, lk2 = diff_lambda[l]
        lam = jnp.exp(jnp.sum(lq1 * lk1)) - jnp.exp(jnp.sum(lq2 * lk2)) + lambda_init
        o_a = _diff_attention(qa.reshape(bsz, N_HEADS, 2, HEAD_DIM, seq), ka, heads_t(va, V_ROWS), lam,
                              diff_subln_gain[l], t5_flat, bias_max, kmax[0], 1.0 - lambda_init)

        o_b = _moba_attention(heads_t(qb, HEAD_DIM), kb, heads_t(vb, V_ROWS), _moba_kmean(kb).astype(BF16),
                              t5_flat, bias_max, kmax[1])

        hi, mid, lo = _forget_cumsum(misc[:, 8:8 + N_HEADS] + f_bias[l][None, :, None])
        ones = jnp.ones_like(hi)
        cparts = jnp.stack([hi, mid, lo], axis=2)
        onep = jnp.stack([ones] * 3, axis=2)
        padw = 2 * HEAD_DIM - HEAD_DIM - 6
        qc_aug = jnp.concatenate([heads_t(qc, HEAD_DIM), onep.astype(BF16), cparts.astype(BF16),
                                  jnp.zeros((bsz, N_HEADS, padw, seq), BF16)], axis=2)
        kc_aug = jnp.concatenate([kc, (-cparts).transpose(0, 1, 3, 2).astype(BF16),
                                  onep.transpose(0, 1, 3, 2).astype(BF16),
                                  jnp.zeros((bsz, N_HEADS, seq, padw), BF16)], axis=3)
        o_c = _forgetting_attention(qc_aug, kc_aug, heads_t(vc, V_ROWS), heads_t(gc, HEAD_DIM), kmax[2])

        o_d = _dsa_attention(qi_t, misc[:, :8], k_idx, heads_t(qd, w), kd, heads_t(vd, V_ROWS), t5_flat, bias_max,
                             kmax[3])

        x = _output_projection(x, [o.reshape(bsz, w, seq) for o in (o_a, o_b, o_c)] + [o_d], w_out[l], g1)
        x = _ffn(x, norm2_gain[l], sc2, sh2, g2, w_ffn_up[l], ffn_conv_w[l], ffn_conv_b[l], w_ffn_down[l],
                 final_gain, final_norm=(l == depth - 1))
    return x
```

```python
import functools
import math

import numpy as np
import jax
import jax.numpy as jnp
from jax import lax
from jax.experimental import pallas as pl
from jax.experimental.pallas import tpu as pltpu

F32 = jnp.float32
BF16 = jnp.bfloat16
I32 = jnp.int32

HEAD_DIM = 64
N_HEADS = 4
GROUP_W = N_HEADS * HEAD_DIM
DIFF_HALF = HEAD_DIM // 2
MOBA_BLOCK = 256
MOBA_TOPK = 3
DSA_RANK = 128
IDX_HEADS = 4
IDX_DIM = 64
DSA_TOPK_MAX = 256
T5_BUCKETS = 32
T5_MAX_DIST = 128
N_BIAS_HEADS = 12
D_FF = 2816
EPS = 1e-6
NEG = -1e30
LOG2E = math.log2(math.e)
V_ROWS = HEAD_DIM + 16
BOUND_SLACK = 1.0 + 2.0 ** -6
MIN_DENOMINATOR = 2.0 ** -60
BF16_ROWS = 16
TAKE_ALL_TIES = 1e9

V7X_VMEM_LIMIT_BYTES = 56 * 1024 * 1024

ATTN_TILE = 512
SPARSE_TILE = 256
ROW_TILE = 512
FFN_ROW_TILE = 1024
FFN_CHUNK = 256
FORGET_HEADS_PER_STEP = 4
DIFF_HEADS_PER_STEP = 2
FORGET_BLOCKS_PER_STEP = 2


def _t5_thresholds():
    d = np.arange(0, 4 * T5_MAX_DIST)
    max_exact = T5_BUCKETS // 2
    ratio = np.maximum(d, 1).astype(np.float32) / max_exact
    large = max_exact + (np.log(ratio) / math.log(T5_MAX_DIST / max_exact) * (T5_BUCKETS - max_exact)).astype(np.int32)
    bucket = np.where(d < max_exact, d, np.minimum(large, T5_BUCKETS - 1))
    return [int(np.argmax(bucket >= b)) for b in range(T5_BUCKETS)]


T5_THRESH = _t5_thresholds()
T5_FAR_DIST = T5_THRESH[-1]


def _cparams(n_axes):
    return pltpu.CompilerParams(dimension_semantics=("arbitrary",) * n_axes,
                                vmem_limit_bytes=V7X_VMEM_LIMIT_BYTES)


def _resident(block_shape, index_map):
    return pl.BlockSpec(block_shape, index_map, pipeline_mode=pl.Buffered(1))


def _smem():
    return pl.BlockSpec(memory_space=pltpu.SMEM)


def _fill_t5_tile(bias_ref, slot, t5_ref, head, delta, tile):
    rows = 64

    def body(r, carry):
        r0 = pl.multiple_of(r * rows, rows)
        kr = lax.broadcasted_iota(I32, (rows, tile), 0) + r0
        qc = lax.broadcasted_iota(I32, (rows, tile), 1)
        d = delta + qc - kr
        val = jnp.full((rows, tile), LOG2E * t5_ref[head], F32)
        for b in range(1, T5_BUCKETS):
            val = jnp.where(d >= T5_THRESH[b], LOG2E * t5_ref[b * N_BIAS_HEADS + head], val)
        bias_ref[slot, pl.ds(r0, rows), :] = val
        return carry

    lax.fori_loop(0, tile // rows, body, 0)


def _unrolled_loop(n, unroll, body, carry):
    def group(g, c):
        for u in range(unroll):
            c = body(g * unroll + u, c)
        return c

    n_groups = n // unroll
    carry = lax.fori_loop(0, n_groups, group, carry)
    return lax.fori_loop(n_groups * unroll, n, body, carry)


def _causal_mask(tile):
    kr = lax.broadcasted_iota(I32, (tile, tile), 0)
    qc = lax.broadcasted_iota(I32, (tile, tile), 1)
    return kr <= qc


def _softmax_accumulate(s, v_aug, acc_ref, idx, *, shift=None, m_ref=None, keep=None, cols=slice(None)):
    if m_ref is None:
        p = jnp.exp2(s - shift)
        acc_ref[idx, :, cols] += jnp.dot(v_aug, p.astype(BF16), preferred_element_type=F32)
        return
    m_old = m_ref[idx, :, cols]
    m_new = jnp.maximum(m_old, jnp.max(s, axis=0, keepdims=True))
    p = jnp.exp2(s - m_new)
    if keep is not None:
        p = jnp.where(keep, p, 0.0)
    acc_ref[idx, :, cols] = (jnp.exp2(m_old - m_new) * acc_ref[idx, :, cols]
                             + jnp.dot(v_aug, p.astype(BF16), preferred_element_type=F32))
    m_ref[idx, :, cols] = m_new


def _logit_bound(q, k_norm_max, extra):
    qf = q.astype(F32)
    q_norm = jnp.sqrt(jnp.sum(qf * qf, axis=0, keepdims=True))
    return q_norm * (k_norm_max * BOUND_SLACK) + (extra + 1.0)


def _denominators_ok(acc_ref, n_chains):
    low = acc_ref[0, HEAD_DIM:HEAD_DIM + 1, :]
    for idx in range(1, n_chains):
        low = jnp.minimum(low, acc_ref[idx, HEAD_DIM:HEAD_DIM + 1, :])
    return jnp.min(low) > MIN_DENOMINATOR


def _two_pass_attention(run, acc_ref, m_ref, n_chains):
    acc_ref[...] = jnp.zeros(acc_ref.shape, F32)
    run(False)

    @pl.when(jnp.logical_not(_denominators_ok(acc_ref, n_chains)))
    def _():
        acc_ref[...] = jnp.zeros(acc_ref.shape, F32)
        m_ref[...] = jnp.full(m_ref.shape, NEG, F32)
        run(True)


def _normalized(acc_ref, idx):
    return acc_ref[idx, :HEAD_DIM, :] / acc_ref[idx, HEAD_DIM:HEAD_DIM + 1, :]


def _mod_kernel(c_ref, w_ref, b_ref, o_ref):
    c = c_ref[...]
    cond = c * (1.0 / (1.0 + jnp.exp(-c)))
    o_ref[...] = jnp.dot(cond.astype(BF16), w_ref[...].astype(BF16), preferred_element_type=F32) + b_ref[...]


def _modulation(c, w_ada, b_ada):
    depth, d, n = w_ada.shape
    bsz = c.shape[0]
    rows = 8
    c_pad = jnp.zeros((rows, d), F32).at[:bsz].set(c)
    tn = 1024
    out = pl.pallas_call(
        _mod_kernel,
        grid=(depth, n // tn),
        in_specs=[pl.BlockSpec((rows, d), lambda l, j: (0, 0)),
                  pl.BlockSpec((None, d, tn), lambda l, j: (l, 0, j)),
                  pl.BlockSpec((None, 1, tn), lambda l, j: (l, 0, j))],
        out_specs=pl.BlockSpec((None, rows, tn), lambda l, j: (l, 0, j)),
        out_shape=jax.ShapeDtypeStruct((depth, rows, n), F32),
        compiler_params=_cparams(2),
        name="adaln_modulation",
    )(c_pad, w_ada, b_ada.reshape(depth, 1, n))
    return out[:, :bsz]


def _rms(x):
    return x * lax.rsqrt(jnp.mean(x * x, axis=-1, keepdims=True) + EPS)


V_GROUP = N_HEADS * V_ROWS
T_QA, T_VA = 0, 2 * GROUP_W
T_QB = T_VA + V_GROUP
T_VB = T_QB + GROUP_W
T_QC = T_VB + V_GROUP
T_VC = T_QC + GROUP_W
T_GC = T_VC + V_GROUP
T_QLAT = T_GC + GROUP_W
T_KVLAT = T_QLAT + DSA_RANK
T_MISC = T_KVLAT + DSA_RANK
T_ROWS = T_MISC + 16
R_KA, R_KB, R_KC, R_KIDX, R_KVLAT, R_COLS = 0, GROUP_W, 2 * GROUP_W, 3 * GROUP_W, 3 * GROUP_W + 128, 4 * GROUP_W


def _inproj_kernel(x_ref, gain_ref, sc_ref, sh_ref, wt_ref, wr_ref, wqd_ref, wqi_ref, wvd_ref, wkd_ref, ones_ref,
                   seg_ref, qa_ref, va_ref, qb_ref, vb_ref, qc_ref, vc_ref, gc_ref, qd_ref, qi_ref, vd_ref, misc_ref,
                   ka_ref, kb_ref, kc_ref, ki_ref, kd_ref, kn_ref, ht_ref):
    x = x_ref[...]
    h = (_rms(x) * gain_ref[...]) * (1.0 + sc_ref[...]) + sh_ref[...]
    h_r = h.astype(BF16)
    ht_ref[...] = h.T.astype(BF16)
    h_t = ht_ref[...]
    ones_rows = ones_ref[...]

    def rms_t(z):
        return z * lax.rsqrt(jnp.mean(z * z, axis=0, keepdims=True) + EPS)

    def proj_t(r0, r1):
        return jnp.dot(wt_ref[r0:r1, :], h_t, preferred_element_type=F32)

    def put(out_ref, rows):
        out_ref[...] = rows.astype(out_ref.dtype).reshape(out_ref.shape)

    put(qa_ref, proj_t(T_QA, T_VA))
    put(va_ref, proj_t(T_VA, T_QB) + ones_rows)
    put(qb_ref, proj_t(T_QB, T_VB))
    put(vb_ref, proj_t(T_VB, T_QC) + ones_rows)
    put(qc_ref, proj_t(T_QC, T_VC))
    put(vc_ref, proj_t(T_VC, T_GC) + ones_rows)
    put(gc_ref, proj_t(T_GC, T_QLAT))
    q_lat = rms_t(proj_t(T_QLAT, T_KVLAT)).astype(BF16)
    kv_lat_t = rms_t(proj_t(T_KVLAT, T_MISC)).astype(BF16)
    put(qd_ref, jnp.dot(wqd_ref[...], q_lat, preferred_element_type=F32))
    qi_ref[...] = jnp.dot(wqi_ref[...], q_lat, preferred_element_type=F32).astype(BF16)
    put(vd_ref, jnp.dot(wvd_ref[...], kv_lat_t, preferred_element_type=F32) + ones_rows)
    misc_ref[...] = proj_t(T_MISC, T_ROWS)

    p_r = jnp.dot(h_r, wr_ref[...], preferred_element_type=F32)
    kv_lat_r = _rms(p_r[:, R_KVLAT:R_COLS]).astype(BF16)
    keys = [p_r[:, R_KA:R_KB].astype(BF16), p_r[:, R_KB:R_KC].astype(BF16), p_r[:, R_KC:R_KIDX].astype(BF16),
            jnp.dot(kv_lat_r, wkd_ref[...], preferred_element_type=F32).astype(BF16)]
    for k, out_ref in zip(keys[:3], (ka_ref, kb_ref, kc_ref)):
        for hd in range(N_HEADS):
            out_ref[hd] = k[:, hd * HEAD_DIM:(hd + 1) * HEAD_DIM]
    kd_ref[...] = keys[3]
    ki_ref[...] = p_r[:, R_KIDX:R_KIDX + IDX_DIM].astype(BF16)

    norms = []
    for k in keys:
        kf = k.astype(F32)
        sq = jnp.dot((kf * kf).astype(BF16), seg_ref[...], preferred_element_type=F32)
        norms.append(jnp.max(sq, axis=0, keepdims=True))
    kn_ref[...] = jnp.concatenate(norms + [jnp.zeros((8 - len(norms), 128), F32)], axis=0)


def _head_rows(w_cols, rows_per_head):
    d = w_cols.shape[0]
    w = w_cols.T.reshape(N_HEADS, HEAD_DIM, d)
    return jnp.concatenate([w, jnp.zeros((N_HEADS, rows_per_head - HEAD_DIM, d), w.dtype)], axis=1)


def _input_projection(x, gain, sc, sh, w_in, w_dq_up, w_didx_q, w_dkv_up):
    bsz, seq, d = x.shape
    w = GROUP_W
    o_fc = 9 * w
    o_gc = o_fc + N_HEADS
    o_ql = o_gc + w
    o_kv = o_ql + DSA_RANK
    o_ki = o_kv + DSA_RANK
    o_wi = o_ki + IDX_DIM
    q_scale_a = DIFF_HALF ** -0.5 * LOG2E
    q_scale = HEAD_DIM ** -0.5 * LOG2E
    wqa = (w_in[:, :w] * q_scale_a).T.reshape(N_HEADS, 2, DIFF_HALF, d)
    zero = jnp.zeros((N_HEADS, DIFF_HALF, d), w_in.dtype)
    wqa = jnp.stack([jnp.concatenate([wqa[:, 0], zero], axis=1), jnp.concatenate([zero, wqa[:, 1]], axis=1)], axis=1)
    values = lambda c0: _head_rows(w_in[:, c0:c0 + w], V_ROWS).reshape(V_GROUP, d)
    small = jnp.zeros((16, d), w_in.dtype)
    small = small.at[:IDX_HEADS].set((w_in[:, o_wi:o_wi + IDX_HEADS] * (IDX_HEADS ** -0.5 * IDX_DIM ** -0.5)).T)
    small = small.at[8:8 + N_HEADS].set(w_in[:, o_fc:o_gc].T)
    wt = jnp.concatenate([wqa.reshape(2 * w, d), values(2 * w),
                          (w_in[:, 3 * w:4 * w] * q_scale).T, values(5 * w),
                          (w_in[:, 6 * w:7 * w] * q_scale).T, values(8 * w),
                          w_in[:, o_gc:o_ql].T, w_in[:, o_ql:o_ki].T, small], axis=0).astype(BF16)
    assert wt.shape[0] == T_ROWS
    wr = jnp.concatenate([w_in[:, w:2 * w], w_in[:, 4 * w:5 * w], w_in[:, 7 * w:8 * w], w_in[:, o_ki:o_wi],
                          jnp.zeros((d, 128 - IDX_DIM), w_in.dtype), w_in[:, o_kv:o_ki]], axis=1).astype(BF16)
    assert wr.shape[1] == R_COLS
    wqd = (w_dq_up * q_scale).T.reshape(N_HEADS, HEAD_DIM, DSA_RANK)
    wqd = jnp.stack([jnp.zeros((N_HEADS, HEAD_DIM, DSA_RANK), wqd.dtype).at[hd].set(wqd[hd]).reshape(w, DSA_RANK)
                     for hd in range(N_HEADS)], axis=0).reshape(N_HEADS * w, DSA_RANK).astype(BF16)
    wqi = w_didx_q.T.astype(BF16)
    wvd = _head_rows(w_dkv_up[:, w:], V_ROWS).reshape(V_GROUP, DSA_RANK).astype(BF16)
    wkd = w_dkv_up[:, :w].astype(BF16)
    ones_rows = jnp.zeros((N_HEADS, V_ROWS, 1), F32).at[:, HEAD_DIM].set(1.0).reshape(V_GROUP, 1)
    seg = (jnp.arange(w)[:, None] // HEAD_DIM == jnp.arange(128)[None, :]).astype(BF16)

    tm = ROW_TILE
    nt = seq // tm
    feat = lambda rows: pl.BlockSpec((None, rows, tm), lambda b, i: (b, 0, i))
    keyh = pl.BlockSpec((None, N_HEADS, tm, HEAD_DIM), lambda b, i: (b, 0, i, 0))
    tok = lambda width: pl.BlockSpec((None, tm, width), lambda b, i: (b, i, 0))
    vec = pl.BlockSpec((None, 1, d), lambda b, i: (b, 0, 0))
    whole = lambda a: _resident(a.shape, lambda b, i: (0,) * a.ndim)
    head = lambda rows: pl.BlockSpec((None, N_HEADS, rows, tm), lambda b, i: (b, 0, 0, i))
    halves = pl.BlockSpec((None, N_HEADS, 2, HEAD_DIM, tm), lambda b, i: (b, 0, 0, 0, i))
    f_bf = lambda rows: jax.ShapeDtypeStruct((bsz, rows, seq), BF16)
    h_bf = lambda rows: jax.ShapeDtypeStruct((bsz, N_HEADS, rows, seq), BF16)
    k_bf = jax.ShapeDtypeStruct((bsz, N_HEADS, seq, HEAD_DIM), BF16)
    outs = pl.pallas_call(
        _inproj_kernel,
        grid=(bsz, nt),
        in_specs=[tok(d), pl.BlockSpec((1, d), lambda b, i: (0, 0)), vec, vec,
                  whole(wt), whole(wr), whole(wqd), whole(wqi), whole(wvd), whole(wkd), whole(ones_rows), whole(seg)],
        out_specs=[halves, head(V_ROWS), head(HEAD_DIM), head(V_ROWS), head(HEAD_DIM), head(V_ROWS), head(HEAD_DIM),
                   head(w), feat(w), head(V_ROWS), feat(16),
                   keyh, keyh, keyh, tok(IDX_DIM), tok(w),
                   pl.BlockSpec((None, None, 8, 128), lambda b, i: (b, i, 0, 0))],
        out_shape=[jax.ShapeDtypeStruct((bsz, N_HEADS, 2, HEAD_DIM, seq), BF16), h_bf(V_ROWS), h_bf(HEAD_DIM),
                   h_bf(V_ROWS), h_bf(HEAD_DIM), h_bf(V_ROWS),
                   jax.ShapeDtypeStruct((bsz, N_HEADS, HEAD_DIM, seq), F32),
                   h_bf(w), f_bf(w), h_bf(V_ROWS), jax.ShapeDtypeStruct((bsz, 16, seq), F32),
                   k_bf, k_bf, k_bf, jax.ShapeDtypeStruct((bsz, seq, IDX_DIM), BF16),
                   jax.ShapeDtypeStruct((bsz, seq, w), BF16),
                   jax.ShapeDtypeStruct((bsz, nt, 8, 128), F32)],
        scratch_shapes=[pltpu.VMEM((d, tm), BF16)],
        compiler_params=_cparams(2),
        name="input_projection",
    )(x, gain.reshape(1, d), sc, sh, wt, wr, wqd, wqi, wvd, wkd, ones_rows, seg)
    k_norm_max = jnp.sqrt(jnp.max(outs[-1], axis=1))[:, :4, :N_HEADS]
    return outs[:-1], k_norm_max


def _diff_kernel(t5_ref, bmax_ref, kmax_ref, lam_ref, q_ref, k_ref, v_ref, g_ref, o_ref, bias_ref, m_ref, acc_ref,
                 *, tile, head0, heads, group_heads, out_scale):
    b = pl.program_id(0)
    hg = pl.program_id(1)
    i = pl.program_id(2)

    @pl.when(i == 0)
    def _():
        for h in range(heads):
            _fill_t5_tile(bias_ref, 2 * h, t5_ref, head0 + hg * heads + h, 0, tile)
            _fill_t5_tile(bias_ref, 2 * h + 1, t5_ref, head0 + hg * heads + h, tile, tile)

    far_bias = [LOG2E * t5_ref[(T5_BUCKETS - 1) * N_BIAS_HEADS + head0 + hg * heads + h] for h in range(heads)]
    bounds = [_logit_bound(q_ref[h, c], kmax_ref[b * group_heads + hg * heads + h], bmax_ref[head0 + hg * heads + h])
              for h in range(heads) for c in range(2)]

    def run(online):
        def step(j, kind):
            k0 = pl.multiple_of(j * tile, tile)
            logits = [jnp.dot(k_ref[h, pl.ds(k0, tile), :], q_ref[h, c], preferred_element_type=F32)
                      for h in range(heads) for c in range(2)]
            for h in range(heads):
                vt = v_ref[h, :, pl.ds(k0, tile)]
                for c in range(2):
                    idx = 2 * h + c
                    s, shift = logits[idx], bounds[idx]
                    if kind != "far":
                        s = s + bias_ref[2 * h + (1 if kind == "near" else 0)]
                    elif online:
                        s = s + far_bias[h]
                    else:
                        shift = shift - far_bias[h]
                    if kind == "diag":
                        s = jnp.where(_causal_mask(tile), s, NEG)
                    _softmax_accumulate(s, vt, acc_ref, idx, shift=shift, m_ref=m_ref if online else None)

        step(i, "diag")

        @pl.when(i >= 1)
        def _():
            step(i - 1, "near")

        def far(j, carry):
            step(j, "far")
            return carry

        _unrolled_loop(jnp.maximum(i - 1, 0), 2, far, 0)

    _two_pass_attention(run, acc_ref, m_ref, 2 * heads)

    for h in range(heads):
        o = _normalized(acc_ref, 2 * h) - lam_ref[0] * _normalized(acc_ref, 2 * h + 1)
        y = o * lax.rsqrt(jnp.mean(o * o, axis=0, keepdims=True) + EPS)
        o_ref[h] = ((y * g_ref[...]) * out_scale).astype(o_ref.dtype)


def _diff_attention(q_t, k, v_aug, lam, subln_gain, t5_flat, bias_max, k_norm_max, out_scale):
    bsz, nh, _, dh, seq = q_t.shape
    tile = min(ATTN_TILE, seq)
    hps = DIFF_HEADS_PER_STEP
    assert tile >= T5_FAR_DIST and seq % tile == 0 and nh % hps == 0
    kern = functools.partial(_diff_kernel, tile=tile, head0=0, heads=hps, group_heads=nh, out_scale=out_scale)
    return pl.pallas_call(
        kern,
        grid=(bsz, nh // hps, seq // tile),
        in_specs=[_smem(), _smem(), _smem(), _smem(),
                  pl.BlockSpec((None, hps, 2, dh, tile), lambda b, h, i: (b, h, 0, 0, i)),
                  _resident((None, hps, seq, dh), lambda b, h, i: (b, h, 0, 0)),
                  _resident((None, hps, V_ROWS, seq), lambda b, h, i: (b, h, 0, 0)),
                  pl.BlockSpec((dh, 1), lambda b, h, i: (0, 0))],
        out_specs=pl.BlockSpec((None, hps, dh, tile), lambda b, h, i: (b, h, 0, i)),
        out_shape=jax.ShapeDtypeStruct((bsz, nh, dh, seq), BF16),
        scratch_shapes=[pltpu.VMEM((2 * hps, tile, tile), F32),
                        pltpu.VMEM((2 * hps, 1, tile), F32),
                        pltpu.VMEM((2 * hps, V_ROWS, tile), F32)],
        compiler_params=_cparams(3),
        name="diff_attention",
    )(t5_flat, bias_max, k_norm_max, lam.reshape(1), q_t, k, v_aug, subln_gain.reshape(dh, 1))


def _kmean_kernel(k_ref, o_ref, *, blocks):
    k = k_ref[...].astype(F32)
    o_ref[...] = jnp.mean(k.reshape(blocks, MOBA_BLOCK, k.shape[-1]), axis=1)


def _moba_kmean(k):
    bsz, nh, seq, dh = k.shape
    nb = seq // MOBA_BLOCK
    blocks = min(8, nb)
    return pl.pallas_call(
        functools.partial(_kmean_kernel, blocks=blocks),
        grid=(bsz, nh, nb // blocks),
        in_specs=[pl.BlockSpec((None, None, blocks * MOBA_BLOCK, dh), lambda b, h, i: (b, h, i, 0))],
        out_specs=pl.BlockSpec((None, None, blocks, dh), lambda b, h, i: (b, h, i, 0)),
        out_shape=jax.ShapeDtypeStruct((bsz, nh, nb, dh), F32),
        compiler_params=_cparams(3),
    )(k)


def _moba_kernel(t5_ref, bmax_ref, kmax_ref, q_ref, k_ref, v_ref, km_ref, o_ref, bias_ref, sel_ref, m_ref, acc_ref,
                 *, block, head0, heads, n_blocks):
    b = pl.program_id(0)
    t = pl.program_id(1)
    qt = 2 * block
    halves = (slice(0, block), slice(block, qt))

    @pl.when(t == 0)
    def _():
        for h in range(heads):
            _fill_t5_tile(bias_ref, 2 * h, t5_ref, head0 + h, 0, block)
            _fill_t5_tile(bias_ref, 2 * h + 1, t5_ref, head0 + h, block, block)

    far_bias = [LOG2E * t5_ref[(T5_BUCKETS - 1) * N_BIAS_HEADS + head0 + h] for h in range(heads)]
    bounds = [_logit_bound(q_ref[h], kmax_ref[b * heads + h], bmax_ref[head0 + h]) for h in range(heads)]

    nidx = lax.broadcasted_iota(I32, (n_blocks, qt), 0)
    own = 2 * t + jnp.where(lax.broadcasted_iota(I32, (n_blocks, qt), 1) >= block, 1, 0)
    for h in range(heads):
        gate = jnp.dot(km_ref[h], q_ref[h], preferred_element_type=F32)
        g = jnp.where(nidx < own, gate, NEG)
        sel = jnp.zeros(gate.shape, F32)
        for _ in range(min(MOBA_TOPK, n_blocks)):
            mx = jnp.max(g, axis=0, keepdims=True)
            first = jnp.min(jnp.where(g == mx, nidx, n_blocks), axis=0, keepdims=True)
            pick = nidx == first
            sel = jnp.where(pick, 1.0, sel)
            g = jnp.where(pick, -jnp.inf, g)
        sel_ref[h] = jnp.where(nidx < own, sel, 0.0)

    def run(online):
        def step(n, kinds):
            k0 = pl.multiple_of(n * block, block)
            active = [hf for hf in range(2) if kinds[hf] != "skip"]
            cols = slice(halves[active[0]].start, halves[active[-1]].stop)
            logits = [jnp.dot(k_ref[h, pl.ds(k0, block), :], q_ref[h, :, cols], preferred_element_type=F32)
                      for h in range(heads)]
            merged = len(active) == 2 and kinds[0] == kinds[1] == "far"
            segments = [(cols, "far")] if merged else [(halves[hf], kinds[hf]) for hf in active]
            for h in range(heads):
                vt = v_ref[h, :, pl.ds(k0, block)]
                for seg, kind in segments:
                    s = logits[h][:, seg.start - cols.start:seg.stop - cols.start]
                    shift = bounds[h][:, seg]
                    if kind != "far":
                        s = s + bias_ref[2 * h + (1 if kind == "near" else 0)]
                    elif online:
                        s = s + far_bias[h]
                    else:
                        shift = shift - far_bias[h]
                    mask = _causal_mask(block) if kind == "diag" else sel_ref[h, pl.ds(n, 1), seg] > 0.5
                    s = jnp.where(mask, s, NEG)
                    _softmax_accumulate(s, vt, acc_ref, h, shift=shift, m_ref=m_ref if online else None,
                                        keep=mask, cols=seg)

        step(2 * t + 1, ("skip", "diag"))
        step(2 * t, ("diag", "near"))

        @pl.when(t >= 1)
        def _():
            step(2 * t - 1, ("near", "far"))

        def far(n, carry):
            step(n, ("far", "far"))
            return carry

        _unrolled_loop(jnp.maximum(2 * t - 1, 0), 2, far, 0)

    _two_pass_attention(run, acc_ref, m_ref, heads)
    for h in range(heads):
        o_ref[h] = _normalized(acc_ref, h).astype(o_ref.dtype)


def _moba_attention(q_t, k, v_aug, kmean, t5_flat, bias_max, k_norm_max):
    bsz, nh, dh, seq = q_t.shape
    block = MOBA_BLOCK
    qt = 2 * block
    assert seq % qt == 0 and block >= T5_FAR_DIST
    nb = seq // block
    kern = functools.partial(_moba_kernel, block=block, head0=N_HEADS, heads=nh, n_blocks=nb)
    return pl.pallas_call(
        kern,
        grid=(bsz, seq // qt),
        in_specs=[_smem(), _smem(), _smem(),
                  pl.BlockSpec((None, nh, dh, qt), lambda b, i: (b, 0, 0, i)),
                  _resident((None, nh, seq, dh), lambda b, i: (b, 0, 0, 0)),
                  _resident((None, nh, V_ROWS, seq), lambda b, i: (b, 0, 0, 0)),
                  _resident((None, nh, nb, dh), lambda b, i: (b, 0, 0, 0))],
        out_specs=pl.BlockSpec((None, nh, dh, qt), lambda b, i: (b, 0, 0, i)),
        out_shape=jax.ShapeDtypeStruct((bsz, nh, dh, seq), BF16),
        scratch_shapes=[pltpu.VMEM((2 * nh, block, block), F32), pltpu.VMEM((nh, nb, qt), F32),
                        pltpu.VMEM((nh, 1, qt), F32),
                        pltpu.VMEM((nh, V_ROWS, qt), F32)],
        compiler_params=_cparams(2),
        name="moba_attention",
    )(t5_flat, bias_max, k_norm_max, q_t, k, v_aug, kmean)


def _split3(x):
    def trunc(v):
        bits = lax.bitcast_convert_type(v, I32)
        return lax.bitcast_convert_type(bits & jnp.int32(-65536), F32)
    hi = trunc(x)
    r1 = x - hi
    mid = trunc(r1)
    lo = trunc(r1 - mid)
    return hi, mid, lo


def _forget_cumsum_kernel(f_ref, hi_ref, mid_ref, lo_ref):
    x = f_ref[...]
    rows = x.shape[0]
    ls = jnp.minimum(x, 0.0) - jnp.log(1.0 + jnp.exp(-jnp.abs(x)))
    upper = (lax.broadcasted_iota(I32, (128, 128), 0) <= lax.broadcasted_iota(I32, (128, 128), 1)).astype(F32)
    within = jnp.dot(ls, upper, preferred_element_type=F32, precision=lax.Precision.HIGHEST)
    strict = (lax.broadcasted_iota(I32, (rows, rows), 1) < lax.broadcasted_iota(I32, (rows, rows), 0)).astype(F32)
    before = jnp.dot(strict, within, preferred_element_type=F32, precision=lax.Precision.HIGHEST)
    cf = (within + before[:, 127:128]) * LOG2E
    hi, mid, lo = _split3(cf)
    hi_ref[...] = hi
    mid_ref[...] = mid
    lo_ref[...] = lo


def _forget_cumsum(f_logit):
    bsz, nh, seq = f_logit.shape
    rows = seq // 128
    f = f_logit.reshape(bsz, nh, rows, 128)
    spec = pl.BlockSpec((None, None, rows, 128), lambda b, h: (b, h, 0, 0))
    parts = pl.pallas_call(
        _forget_cumsum_kernel,
        grid=(bsz, nh),
        in_specs=[spec],
        out_specs=[spec] * 3,
        out_shape=[jax.ShapeDtypeStruct((bsz, nh, rows, 128), F32)] * 3,
        compiler_params=_cparams(2),
        name="forget_cumsum",
    )(f)
    return [p.reshape(bsz, nh, seq) for p in parts]


def _forget_kernel(kmax_ref, q_ref, k_ref, v_ref, gate_ref, o_ref, m_ref, acc_ref, *, tile, heads, group_heads):
    b = pl.program_id(0)
    hg = pl.program_id(1)
    i = pl.program_id(2)
    bounds = [_logit_bound(q_ref[h, :HEAD_DIM, :], kmax_ref[b * group_heads + hg * heads + h], 0.0)
              for h in range(heads)]

    def run(online):
        def step(j, blocks, mask):
            k0 = pl.multiple_of(j * tile, tile)
            logits = [jnp.dot(k_ref[h, pl.ds(k0, blocks * tile), :], q_ref[h], preferred_element_type=F32)
                      for h in range(heads)]
            for h in range(heads):
                s = logits[h] if mask is None else jnp.where(mask, logits[h], NEG)
                _softmax_accumulate(s, v_ref[h, :, pl.ds(k0, blocks * tile)], acc_ref, h, shift=bounds[h],
                                    m_ref=m_ref if online else None)

        step(i, 1, _causal_mask(tile))
        n_big = i // FORGET_BLOCKS_PER_STEP

        def past_big(j, carry):
            step(j * FORGET_BLOCKS_PER_STEP, FORGET_BLOCKS_PER_STEP, None)
            return carry

        def past_single(j, carry):
            step(j, 1, None)
            return carry

        lax.fori_loop(0, n_big, past_big, 0)
        lax.fori_loop(n_big * FORGET_BLOCKS_PER_STEP, i, past_single, 0)

    _two_pass_attention(run, acc_ref, m_ref, heads)
    for h in range(heads):
        g = gate_ref[h]
        o_ref[h] = (_normalized(acc_ref, h) * (1.0 / (1.0 + jnp.exp(-g)))).astype(o_ref.dtype)


def _forgetting_attention(q_aug_t, k_aug, v_aug, gate_t, k_norm_max):
    bsz, nh, kdim, seq = q_aug_t.shape
    dh = gate_t.shape[2]
    tile = min(ATTN_TILE, seq)
    hps = FORGET_HEADS_PER_STEP
    assert seq % tile == 0 and nh % hps == 0
    return pl.pallas_call(
        functools.partial(_forget_kernel, tile=tile, heads=hps, group_heads=nh),
        grid=(bsz, nh // hps, seq // tile),
        in_specs=[_smem(),
                  pl.BlockSpec((None, hps, kdim, tile), lambda b, h, i: (b, h, 0, i)),
                  _resident((None, hps, seq, kdim), lambda b, h, i: (b, h, 0, 0)),
                  _resident((None, hps, V_ROWS, seq), lambda b, h, i: (b, h, 0, 0)),
                  pl.BlockSpec((None, hps, dh, tile), lambda b, h, i: (b, h, 0, i))],
        out_specs=pl.BlockSpec((None, hps, dh, tile), lambda b, h, i: (b, h, 0, i)),
        out_shape=jax.ShapeDtypeStruct((bsz, nh, dh, seq), BF16),
        scratch_shapes=[pltpu.VMEM((hps, 1, tile), F32),
                        pltpu.VMEM((hps, V_ROWS, tile), F32)],
        compiler_params=_cparams(3),
        name="forget_attention",
    )(k_norm_max, q_aug_t, k_aug, v_aug, gate_t)


def _key_to_float(u):
    ks = u ^ jnp.int32(-2147483648)
    bits = jnp.where(ks < 0, ks ^ jnp.int32(2147483647), ks)
    return lax.bitcast_convert_type(bits, F32)


def _truncate_to_bf16(x):
    bits = lax.bitcast_convert_type(x, I32) & jnp.int32(-65536)
    return lax.bitcast_convert_type(bits, F32).astype(BF16)


def _dsa_kernel(t5_ref, bmax_ref, kmax_ref, qi_ref, w_ref, ki_ref, q_ref, k_ref, v_ref, o_ref,
                sc_ref, hi_ref, bias_ref, tri_ref, m_ref, acc_ref, *, tile, head0, topk):
    b = pl.program_id(0)
    i = pl.program_id(1)

    @pl.when((b == 0) & (i == 0))
    def _():
        for h in range(N_HEADS):
            _fill_t5_tile(bias_ref, 2 * h, t5_ref, head0 + h, 0, tile)
            _fill_t5_tile(bias_ref, 2 * h + 1, t5_ref, head0 + h, tile, tile)
        tri_ref[...] = jnp.where(lax.broadcasted_iota(I32, (tile, tile), 1) <= lax.broadcasted_iota(I32, (tile, tile), 0),
                                 1.0, 0.0).astype(BF16)

    causal = _causal_mask(tile)

    def index_scores(j):
        k0 = pl.multiple_of(j * tile, tile)
        kt = ki_ref[pl.ds(k0, tile), :]
        raws = [jnp.dot(kt, qi_ref[h * IDX_DIM:(h + 1) * IDX_DIM, :], preferred_element_type=F32)
                for h in range(IDX_HEADS)]
        sc = jnp.zeros((tile, tile), F32)
        for h in range(IDX_HEADS):
            sc = sc + jnp.maximum(raws[h], 0.0) * w_ref[h:h + 1, :]
        return k0, sc

    def store_scores(k0, sc):
        sc_ref[pl.ds(k0, tile), :] = sc
        hi_ref[pl.ds(k0, tile), :] = _truncate_to_bf16(sc)

    def score_body(j, carry):
        store_scores(*index_scores(j))
        return carry

    _unrolled_loop(i, 4, score_body, 0)
    k0, sc = index_scores(i)
    store_scores(k0, jnp.where(causal, sc, NEG))

    def count(pred):
        def body(j, cnt):
            k0 = pl.multiple_of(j * tile, tile)
            ind = jnp.where(pred(sc_ref[pl.ds(k0, tile), :]), 1.0, 0.0)
            return cnt + jnp.sum(ind.reshape(tile // 8, 8, tile), axis=0)
        cnt = _unrolled_loop(i + 1, 4, body, jnp.zeros((8, tile), F32))
        return jnp.sum(cnt, axis=0, keepdims=True)

    def count_truncated(thr):
        def body(j, cnt):
            k0 = pl.multiple_of(j * tile, tile)
            ind = jnp.where(hi_ref[pl.ds(k0, tile), :] >= thr, jnp.ones((), BF16), jnp.zeros((), BF16))
            part = ind[0:BF16_ROWS]
            for r in range(1, tile // BF16_ROWS):
                part = part + ind[r * BF16_ROWS:(r + 1) * BF16_ROWS]
            return cnt + part.astype(F32)
        cnt = _unrolled_loop(i + 1, 4, body, jnp.zeros((BF16_ROWS, tile), F32))
        return jnp.sum(cnt, axis=0, keepdims=True)

    def high_bit(it, ans):
        cand = ans | jnp.left_shift(jnp.int32(1), 31 - it)
        return jnp.where(count_truncated(_truncate_to_bf16(_key_to_float(cand))) >= topk, cand, ans)

    ans = lax.fori_loop(0, 16, high_bit, jnp.zeros((1, tile), I32))

    above = count(lambda s: s > _key_to_float(ans))
    settled = above < topk
    need = jnp.where(settled, topk - above, 0.0)

    def low_bit(state):
        it, ans, open_f, need = state
        cand = ans | jnp.left_shift(jnp.int32(1), 31 - it)
        thr = _key_to_float(cand)
        cnt = count(lambda s: s >= thr)
        unsettled = open_f > 0.5
        ans = jnp.where(unsettled & (cnt >= topk), cand, ans)
        exact = unsettled & (cnt == topk)
        return it + 1, ans, jnp.where(exact, 0.0, open_f), jnp.where(exact, TAKE_ALL_TIES, need)

    _, ans, open_f, need = lax.while_loop(lambda st: (st[0] < 32) & (jnp.max(st[2]) > 0.5), low_bit,
                                          (jnp.int32(16), ans, jnp.where(settled, 0.0, 1.0), need))
    tau = _key_to_float(ans)
    need = jnp.where(open_f > 0.5, topk - count(lambda s: s > tau), need)

    qs = [q_ref[h] for h in range(N_HEADS)]

    far_bias = [LOG2E * t5_ref[(T5_BUCKETS - 1) * N_BIAS_HEADS + head0 + h] for h in range(N_HEADS)]
    bounds = [_logit_bound(qs[h], kmax_ref[b * N_HEADS + h], bmax_ref[head0 + h]) for h in range(N_HEADS)]

    def run(online):
        def attend(j, ties_seen, kind):
            k0 = pl.multiple_of(j * tile, tile)
            sc = sc_ref[pl.ds(k0, tile), :]
            eq = sc == tau
            eqf = jnp.where(eq, 1.0, 0.0)
            rank = ties_seen + jnp.dot(tri_ref[...], eqf.astype(BF16), preferred_element_type=F32)
            wgt = jnp.where(eq, jnp.where(rank <= need, 1.0, 0.0), jnp.where(sc > tau, 1.0, 0.0))
            if kind == "diag":
                wgt = jnp.where(causal, wgt, 0.0)
            keep = wgt > 0.5
            kt = k_ref[pl.ds(k0, tile), :]
            logits = [jnp.dot(kt, qs[h], preferred_element_type=F32) for h in range(N_HEADS)]
            for h in range(N_HEADS):
                s, shift = logits[h], bounds[h]
                if kind != "far":
                    s = s + bias_ref[2 * h + (1 if kind == "near" else 0)]
                elif online:
                    s = s + far_bias[h]
                else:
                    shift = shift - far_bias[h]
                s = jnp.where(keep, s, NEG)
                _softmax_accumulate(s, v_ref[h, :, pl.ds(k0, tile)], acc_ref, h, shift=shift,
                                    m_ref=m_ref if online else None, keep=keep)
            return ties_seen + jnp.sum(eqf, axis=0, keepdims=True)

        seen = _unrolled_loop(jnp.maximum(i - 1, 0), 4, lambda j, r: attend(j, r, "far"), jnp.zeros((1, tile), F32))
        seen = lax.cond(i >= 1, lambda r: attend(i - 1, r, "near"), lambda r: r, seen)
        attend(i, seen, "diag")

    _two_pass_attention(run, acc_ref, m_ref, N_HEADS)
    for h in range(N_HEADS):
        o_ref[h * HEAD_DIM:(h + 1) * HEAD_DIM, :] = _normalized(acc_ref, h).astype(o_ref.dtype)


def _dsa_attention(qi_t, w_t, k_idx, q_t, k, v_aug, t5_flat, bias_max, k_norm_max):
    bsz, _, width, seq = q_t.shape
    tile = min(SPARSE_TILE, seq)
    topk = min(DSA_TOPK_MAX, seq // 4)
    assert seq % tile == 0 and tile >= topk and tile >= T5_FAR_DIST
    kern = functools.partial(_dsa_kernel, tile=tile, head0=2 * N_HEADS, topk=topk)
    return pl.pallas_call(
        kern,
        grid=(bsz, seq // tile),
        in_specs=[_smem(), _smem(), _smem(),
                  pl.BlockSpec((None, qi_t.shape[1], tile), lambda b, i: (b, 0, i)),
                  pl.BlockSpec((None, 8, tile), lambda b, i: (b, 0, i)),
                  _resident((None, seq, k_idx.shape[2]), lambda b, i: (b, 0, 0)),
                  pl.BlockSpec((None, N_HEADS, width, tile), lambda b, i: (b, 0, 0, i)),
                  _resident((None, seq, width), lambda b, i: (b, 0, 0)),
                  _resident((None, N_HEADS, V_ROWS, seq), lambda b, i: (b, 0, 0, 0))],
        out_specs=pl.BlockSpec((None, width, tile), lambda b, i: (b, 0, i)),
        out_shape=jax.ShapeDtypeStruct((bsz, width, seq), BF16),
        scratch_shapes=[pltpu.VMEM((seq, tile), F32), pltpu.VMEM((seq, tile), BF16),
                        pltpu.VMEM((2 * N_HEADS, tile, tile), F32),
                        pltpu.VMEM((tile, tile), BF16),
                        pltpu.VMEM((N_HEADS, 1, tile), F32),
                        pltpu.VMEM((N_HEADS, V_ROWS, tile), F32)],
        compiler_params=_cparams(2),
        name="dsa_attention",
    )(t5_flat, bias_max, k_norm_max, qi_t, w_t, k_idx, q_t, k, v_aug)


def _outproj_kernel(x_ref, oa_ref, ob_ref, oc_ref, od_ref, w_ref, g_ref, y_ref):
    o_t = jnp.concatenate([r[...].reshape(GROUP_W, r.shape[-1]) for r in (oa_ref, ob_ref, oc_ref, od_ref)], axis=0)
    y = lax.dot_general(o_t, w_ref[...], (((0,), (0,)), ((), ())), preferred_element_type=F32)
    y_ref[...] = x_ref[...] + g_ref[...] * y


def _output_projection(x, o_groups, w_out, gate):
    bsz, seq, d = x.shape
    tm = ROW_TILE
    row = pl.BlockSpec((None, tm, d), lambda b, i: (b, i, 0))

    def grp(o):
        if o.ndim == 4:
            return pl.BlockSpec((None,) + o.shape[1:3] + (tm,), lambda b, i: (b, 0, 0, i))
        return pl.BlockSpec((None, GROUP_W, tm), lambda b, i: (b, 0, i))

    return pl.pallas_call(
        _outproj_kernel,
        grid=(bsz, seq // tm),
        in_specs=[row] + [grp(o) for o in o_groups] + [
                  _resident(w_out.shape, lambda b, i: (0, 0)),
                  pl.BlockSpec((None, 1, d), lambda b, i: (b, 0, 0))],
        out_specs=row,
        out_shape=jax.ShapeDtypeStruct(x.shape, F32),
        compiler_params=_cparams(2),
        name="output_projection",
    )(x, *o_groups, w_out.astype(BF16), gate)


HALO = 8


def _ffn_kernel(x_ref, xp_ref, gain_ref, sc_ref, sh_ref, g_ref, wup_ref, cw_ref, cb_ref, wd_ref, fg_ref,
                y_ref, h_ref, acc_ref, *, tm, chunk, n_chunks, final_norm):
    i = pl.program_id(1)

    def modulated(x):
        return ((_rms(x) * gain_ref[...]) * (1.0 + sc_ref[...]) + sh_ref[...]).astype(BF16)

    h_ref[HALO:, :] = modulated(x_ref[...])
    halo = modulated(xp_ref[...])
    h_ref[:HALO, :] = jnp.where(i > 0, halo, jnp.zeros_like(halo))
    acc_ref[...] = jnp.zeros(acc_ref.shape, F32)
    h = h_ref[...]

    def up(c0, half):
        col = pl.multiple_of(half * D_FF + c0, 128)
        return jnp.dot(h, wup_ref[:, pl.ds(col, chunk)], preferred_element_type=F32)

    def conv(u, c0, half):
        cw = cw_ref[half, :, pl.ds(c0, chunk)]
        return (cw[0:1] * u[HALO - 2:HALO - 2 + tm] + cw[1:2] * u[HALO - 1:HALO - 1 + tm]
                + cw[2:3] * u[HALO:HALO + tm]) + cb_ref[half, :, pl.ds(c0, chunk)]

    def chunks(first, count):
        c0s = [pl.multiple_of((first + n) * chunk, chunk) for n in range(count)]
        ups = [(up(c0, 0), up(c0, 1)) for c0 in c0s]
        acts = []
        for c0, (ug, uv) in zip(c0s, ups):
            gate = conv(ug, c0, 0)
            acts.append(((gate * (1.0 / (1.0 + jnp.exp(-gate)))) * conv(uv, c0, 1)).astype(BF16))
        a = acts[0] if count == 1 else jnp.concatenate(acts, axis=1)
        acc_ref[...] += jnp.dot(a, wd_ref[pl.ds(c0s[0], count * chunk), :], preferred_element_type=F32)

    def pair(p, carry):
        chunks(2 * p, 2)
        return carry

    lax.fori_loop(0, n_chunks // 2, pair, 0)
    if n_chunks % 2:
        chunks(n_chunks - 1, 1)

    y = x_ref[...] + g_ref[...] * acc_ref[...]
    if final_norm:
        y = _rms(y) * fg_ref[...]
    y_ref[...] = y


def _ffn(x, gain, sc, sh, gate, w_up, conv_w, conv_b, w_down, final_gain, final_norm):
    bsz, seq, d = x.shape
    tm = min(FFN_ROW_TILE, seq)
    fc = FFN_CHUNK
    n_chunks = D_FF // fc
    w_up_b = w_up.astype(BF16)
    cw = conv_w.reshape(3, 2, D_FF).transpose(1, 0, 2)
    cb = conv_b.reshape(2, 1, D_FF)
    vec = pl.BlockSpec((None, 1, d), lambda b, i: (b, 0, 0))
    one = pl.BlockSpec((1, d), lambda b, i: (0, 0))
    row = pl.BlockSpec((None, tm, d), lambda b, i: (b, i, 0))
    whole = lambda a: _resident(a.shape, lambda b, i: (0,) * a.ndim)
    w_down_b = w_down.astype(BF16)
    return pl.pallas_call(
        functools.partial(_ffn_kernel, tm=tm, chunk=fc, n_chunks=n_chunks, final_norm=final_norm),
        grid=(bsz, seq // tm),
        in_specs=[row,
                  pl.BlockSpec((None, HALO, d), lambda b, i: (b, jnp.maximum(i * (tm // HALO) - 1, 0), 0)),
                  one, vec, vec, vec, whole(w_up_b), whole(cw), whole(cb), whole(w_down_b), one],
        out_specs=row,
        out_shape=jax.ShapeDtypeStruct(x.shape, F32),
        scratch_shapes=[pltpu.VMEM((HALO + tm, d), BF16), pltpu.VMEM((tm, d), F32)],
        compiler_params=_cparams(2),
        name="conv_glu_ffn",
    )(x, x, gain.reshape(1, d), sc, sh, gate, w_up_b, cw, cb, w_down_b, final_gain.reshape(1, d))


def kernel(x, c, w_ada, b_ada, norm1_gain, w_in, f_bias, diff_lambda, diff_subln_gain, w_dq_up, w_didx_q,
           w_dkv_up, w_out, t5_table, norm2_gain, w_ffn_up, ffn_conv_w, ffn_conv_b, w_ffn_down, final_gain):
    bsz, seq, d = x.shape
    depth = w_ada.shape[0]
    t5_flat = t5_table.reshape(-1)
    bias_max = jnp.max(t5_table, axis=0) * LOG2E
    mod = _modulation(c, w_ada, b_ada)
    w = GROUP_W

    for l in range(depth):
        sh1, sc1, g1, sh2, sc2, g2 = [m[:, None, :] for m in jnp.split(mod[l], 6, axis=-1)]
        (qa, va, qb, vb, qc, vc, gc, qd, qi_t, vd, misc, ka, kb, kc, k_idx, kd), k_norm_max = _input_projection(
            x, norm1_gain[l], sc1, sh1, w_in[l], w_dq_up[l], w_didx_q[l], w_dkv_up[l])
        kmax = [k_norm_max[:, g].reshape(-1) for g in range(4)]

        lambda_init = 0.8 - 0.6 * math.exp(-0.3 * l)
        lq1, lk1, lq2, lk2 = diff_lambda[l]
        lam = jnp.exp(jnp.sum(lq1 * lk1)) - jnp.exp(jnp.sum(lq2 * lk2)) + lambda_init
        o_a = _diff_attention(qa, ka, va, lam, diff_subln_gain[l], t5_flat, bias_max, kmax[0], 1.0 - lambda_init)

        o_b = _moba_attention(qb, kb, vb, _moba_kmean(kb).astype(BF16), t5_flat, bias_max, kmax[1])

        hi, mid, lo = _forget_cumsum(misc[:, 8:8 + N_HEADS] + f_bias[l][None, :, None])
        ones = jnp.ones_like(hi)
        cparts = jnp.stack([hi, mid, lo], axis=2)
        onep = jnp.stack([ones] * 3, axis=2)
        padw = 2 * HEAD_DIM - HEAD_DIM - 6
        qc_aug = jnp.concatenate([qc, onep.astype(BF16), cparts.astype(BF16),
                                  jnp.zeros((bsz, N_HEADS, padw, seq), BF16)], axis=2)
        kc_aug = jnp.concatenate([kc, (-cparts).transpose(0, 1, 3, 2).astype(BF16),
                                  onep.transpose(0, 1, 3, 2).astype(BF16),
                                  jnp.zeros((bsz, N_HEADS, seq, padw), BF16)], axis=3)
        o_c = _forgetting_attention(qc_aug, kc_aug, vc, gc, kmax[2])

        o_d = _dsa_attention(qi_t, misc[:, :8], k_idx, qd, kd, vd, t5_flat, bias_max, kmax[3])

        x = _output_projection(x, [o_a, o_b, o_c, o_d], w_out[l], g1)
        x = _ffn(x, norm2_gain[l], sc2, sh2, g2, w_ffn_up[l], ffn_conv_w[l], ffn_conv_b[l], w_ffn_down[l],
                 final_gain, final_norm=(l == depth - 1))
    return x
```

```python
import functools
import math

import numpy as np
import jax
import jax.numpy as jnp
from jax import lax
from jax.experimental import pallas as pl
from jax.experimental.pallas import tpu as pltpu

F32 = jnp.float32
BF16 = jnp.bfloat16
I32 = jnp.int32

HEAD_DIM = 64
N_HEADS = 4
GROUP_W = N_HEADS * HEAD_DIM
DIFF_HALF = HEAD_DIM // 2
MOBA_BLOCK = 256
MOBA_TOPK = 3
DSA_RANK = 128
IDX_HEADS = 4
IDX_DIM = 64
DSA_TOPK_MAX = 256
T5_BUCKETS = 32
T5_MAX_DIST = 128
N_BIAS_HEADS = 12
D_FF = 2816
EPS = 1e-6
NEG = -1e30
LOG2E = math.log2(math.e)
V_ROWS = HEAD_DIM + 16
BOUND_SLACK = 1.0 + 2.0 ** -6
MIN_DENOMINATOR = 2.0 ** -60
BF16_ROWS = 16
TAKE_ALL_TIES = 1e9

V7X_VMEM_LIMIT_BYTES = 56 * 1024 * 1024

ATTN_TILE = 512
SPARSE_TILE = 256
ROW_TILE = 512
FFN_ROW_TILE = 1024
FFN_CHUNK = 256
FORGET_HEADS_PER_STEP = 4
DIFF_HEADS_PER_STEP = 2
FORGET_BLOCKS_PER_STEP = 2


def _t5_thresholds():
    d = np.arange(0, 4 * T5_MAX_DIST)
    max_exact = T5_BUCKETS // 2
    ratio = np.maximum(d, 1).astype(np.float32) / max_exact
    large = max_exact + (np.log(ratio) / math.log(T5_MAX_DIST / max_exact) * (T5_BUCKETS - max_exact)).astype(np.int32)
    bucket = np.where(d < max_exact, d, np.minimum(large, T5_BUCKETS - 1))
    return [int(np.argmax(bucket >= b)) for b in range(T5_BUCKETS)]


T5_THRESH = _t5_thresholds()
T5_FAR_DIST = T5_THRESH[-1]


def _cparams(n_axes):
    return pltpu.CompilerParams(dimension_semantics=("arbitrary",) * n_axes,
                                vmem_limit_bytes=V7X_VMEM_LIMIT_BYTES)


def _resident(block_shape, index_map):
    return pl.BlockSpec(block_shape, index_map, pipeline_mode=pl.Buffered(1))


def _smem():
    return pl.BlockSpec(memory_space=pltpu.SMEM)


def _fill_t5_tile(bias_ref, slot, t5_ref, head, delta, tile):
    rows = 64

    def body(r, carry):
        r0 = pl.multiple_of(r * rows, rows)
        kr = lax.broadcasted_iota(I32, (rows, tile), 0) + r0
        qc = lax.broadcasted_iota(I32, (rows, tile), 1)
        d = delta + qc - kr
        val = jnp.full((rows, tile), LOG2E * t5_ref[head], F32)
        for b in range(1, T5_BUCKETS):
            val = jnp.where(d >= T5_THRESH[b], LOG2E * t5_ref[b * N_BIAS_HEADS + head], val)
        bias_ref[slot, pl.ds(r0, rows), :] = val
        return carry

    lax.fori_loop(0, tile // rows, body, 0)


def _unrolled_loop(n, unroll, body, carry):
    def group(g, c):
        for u in range(unroll):
            c = body(g * unroll + u, c)
        return c

    n_groups = n // unroll
    carry = lax.fori_loop(0, n_groups, group, carry)
    return lax.fori_loop(n_groups * unroll, n, body, carry)


def _causal_mask(tile):
    kr = lax.broadcasted_iota(I32, (tile, tile), 0)
    qc = lax.broadcasted_iota(I32, (tile, tile), 1)
    return kr <= qc


def _softmax_accumulate(s, v_aug, acc_ref, idx, *, shift=None, m_ref=None, keep=None, cols=slice(None)):
    if m_ref is None:
        p = jnp.exp2(s - shift)
        acc_ref[idx, :, cols] += jnp.dot(v_aug, p.astype(BF16), preferred_element_type=F32)
        return
    m_old = m_ref[idx, :, cols]
    m_new = jnp.maximum(m_old, jnp.max(s, axis=0, keepdims=True))
    p = jnp.exp2(s - m_new)
    if keep is not None:
        p = jnp.where(keep, p, 0.0)
    acc_ref[idx, :, cols] = (jnp.exp2(m_old - m_new) * acc_ref[idx, :, cols]
                             + jnp.dot(v_aug, p.astype(BF16), preferred_element_type=F32))
    m_ref[idx, :, cols] = m_new


def _logit_bound(q, k_norm_max, extra):
    qf = q.astype(F32)
    q_norm = jnp.sqrt(jnp.sum(qf * qf, axis=0, keepdims=True))
    return q_norm * (k_norm_max * BOUND_SLACK) + (extra + 1.0)


def _denominators_ok(acc_ref, n_chains):
    low = acc_ref[0, HEAD_DIM:HEAD_DIM + 1, :]
    for idx in range(1, n_chains):
        low = jnp.minimum(low, acc_ref[idx, HEAD_DIM:HEAD_DIM + 1, :])
    return jnp.min(low) > MIN_DENOMINATOR


def _two_pass_attention(run, acc_ref, m_ref, n_chains):
    acc_ref[...] = jnp.zeros(acc_ref.shape, F32)
    run(False)

    @pl.when(jnp.logical_not(_denominators_ok(acc_ref, n_chains)))
    def _():
        acc_ref[...] = jnp.zeros(acc_ref.shape, F32)
        m_ref[...] = jnp.full(m_ref.shape, NEG, F32)
        run(True)


def _normalized(acc_ref, idx):
    return acc_ref[idx, :HEAD_DIM, :] / acc_ref[idx, HEAD_DIM:HEAD_DIM + 1, :]


def _mod_kernel(c_ref, w_ref, b_ref, o_ref):
    c = c_ref[...]
    cond = c * (1.0 / (1.0 + jnp.exp(-c)))
    o_ref[...] = jnp.dot(cond.astype(BF16), w_ref[...].astype(BF16), preferred_element_type=F32) + b_ref[...]


def _modulation(c, w_ada, b_ada):
    depth, d, n = w_ada.shape
    bsz = c.shape[0]
    rows = 8
    c_pad = jnp.zeros((rows, d), F32).at[:bsz].set(c)
    tn = 1024
    out = pl.pallas_call(
        _mod_kernel,
        grid=(depth, n // tn),
        in_specs=[pl.BlockSpec((rows, d), lambda l, j: (0, 0)),
                  pl.BlockSpec((None, d, tn), lambda l, j: (l, 0, j)),
                  pl.BlockSpec((None, 1, tn), lambda l, j: (l, 0, j))],
        out_specs=pl.BlockSpec((None, rows, tn), lambda l, j: (l, 0, j)),
        out_shape=jax.ShapeDtypeStruct((depth, rows, n), F32),
        compiler_params=_cparams(2),
        name="adaln_modulation",
    )(c_pad, w_ada, b_ada.reshape(depth, 1, n))
    return out[:, :bsz]


def _rms(x):
    return x * lax.rsqrt(jnp.mean(x * x, axis=-1, keepdims=True) + EPS)


V_GROUP = N_HEADS * V_ROWS
T_QA, T_VA = 0, 2 * GROUP_W
T_QB = T_VA + V_GROUP
T_VB = T_QB + GROUP_W
T_QC = T_VB + V_GROUP
T_VC = T_QC + GROUP_W
T_GC = T_VC + V_GROUP
T_QLAT = T_GC + GROUP_W
T_KVLAT = T_QLAT + DSA_RANK
T_MISC = T_KVLAT + DSA_RANK
T_ROWS = T_MISC + 16
R_KA, R_KB, R_KC, R_KIDX, R_KVLAT, R_COLS = 0, GROUP_W, 2 * GROUP_W, 3 * GROUP_W, 3 * GROUP_W + 128, 4 * GROUP_W


def _inproj_kernel(x_ref, gain_ref, sc_ref, sh_ref, wt_ref, wr_ref, wqd_ref, wqi_ref, wvd_ref, wkd_ref, ones_ref,
                   seg_ref, qa_ref, va_ref, qb_ref, vb_ref, qc_ref, vc_ref, gc_ref, qd_ref, qi_ref, vd_ref, misc_ref,
                   ka_ref, kb_ref, kc_ref, ki_ref, kd_ref, kn_ref, ht_ref):
    x = x_ref[...]
    h = (_rms(x) * gain_ref[...]) * (1.0 + sc_ref[...]) + sh_ref[...]
    h_r = h.astype(BF16)
    ht_ref[...] = h.T.astype(BF16)
    h_t = ht_ref[...]
    ones_rows = ones_ref[...]

    def rms_t(z):
        return z * lax.rsqrt(jnp.mean(z * z, axis=0, keepdims=True) + EPS)

    def proj_t(r0, r1):
        return jnp.dot(wt_ref[r0:r1, :], h_t, preferred_element_type=F32)

    def put(out_ref, rows):
        out_ref[...] = rows.astype(out_ref.dtype).reshape(out_ref.shape)

    put(qa_ref, proj_t(T_QA, T_VA))
    put(va_ref, proj_t(T_VA, T_QB) + ones_rows)
    put(qb_ref, proj_t(T_QB, T_VB))
    put(vb_ref, proj_t(T_VB, T_QC) + ones_rows)
    put(qc_ref, proj_t(T_QC, T_VC))
    put(vc_ref, proj_t(T_VC, T_GC) + ones_rows)
    put(gc_ref, proj_t(T_GC, T_QLAT))
    q_lat = rms_t(proj_t(T_QLAT, T_KVLAT)).astype(BF16)
    kv_lat_t = rms_t(proj_t(T_KVLAT, T_MISC)).astype(BF16)
    put(qd_ref, jnp.dot(wqd_ref[...], q_lat, preferred_element_type=F32))
    qi_ref[...] = jnp.dot(wqi_ref[...], q_lat, preferred_element_type=F32).astype(BF16)
    put(vd_ref, jnp.dot(wvd_ref[...], kv_lat_t, preferred_element_type=F32) + ones_rows)
    misc_ref[...] = proj_t(T_MISC, T_ROWS)

    p_r = jnp.dot(h_r, wr_ref[...], preferred_element_type=F32)
    kv_lat_r = _rms(p_r[:, R_KVLAT:R_COLS]).astype(BF16)
    keys = [p_r[:, R_KA:R_KB].astype(BF16), p_r[:, R_KB:R_KC].astype(BF16), p_r[:, R_KC:R_KIDX].astype(BF16),
            jnp.dot(kv_lat_r, wkd_ref[...], preferred_element_type=F32).astype(BF16)]
    for k, out_ref in zip(keys[:3], (ka_ref, kb_ref, kc_ref)):
        for hd in range(N_HEADS):
            out_ref[hd] = k[:, hd * HEAD_DIM:(hd + 1) * HEAD_DIM]
    kd_ref[...] = keys[3]
    ki_ref[...] = p_r[:, R_KIDX:R_KIDX + IDX_DIM].astype(BF16)

    norms = []
    for k in keys:
        kf = k.astype(F32)
        sq = jnp.dot((kf * kf).astype(BF16), seg_ref[...], preferred_element_type=F32)
        norms.append(jnp.max(sq, axis=0, keepdims=True))
    kn_ref[...] = jnp.concatenate(norms + [jnp.zeros((8 - len(norms), 128), F32)], axis=0)


def _head_rows(w_cols, rows_per_head):
    d = w_cols.shape[0]
    w = w_cols.T.reshape(N_HEADS, HEAD_DIM, d)
    return jnp.concatenate([w, jnp.zeros((N_HEADS, rows_per_head - HEAD_DIM, d), w.dtype)], axis=1)


def _input_projection(x, gain, sc, sh, w_in, w_dq_up, w_didx_q, w_dkv_up):
    bsz, seq, d = x.shape
    w = GROUP_W
    o_fc = 9 * w
    o_gc = o_fc + N_HEADS
    o_ql = o_gc + w
    o_kv = o_ql + DSA_RANK
    o_ki = o_kv + DSA_RANK
    o_wi = o_ki + IDX_DIM
    q_scale_a = DIFF_HALF ** -0.5 * LOG2E
    q_scale = HEAD_DIM ** -0.5 * LOG2E
    wqa = (w_in[:, :w] * q_scale_a).T.reshape(N_HEADS, 2, DIFF_HALF, d)
    zero = jnp.zeros((N_HEADS, DIFF_HALF, d), w_in.dtype)
    wqa = jnp.stack([jnp.concatenate([wqa[:, 0], zero], axis=1), jnp.concatenate([zero, wqa[:, 1]], axis=1)], axis=1)
    values = lambda c0: _head_rows(w_in[:, c0:c0 + w], V_ROWS).reshape(V_GROUP, d)
    small = jnp.zeros((16, d), w_in.dtype)
    small = small.at[:IDX_HEADS].set((w_in[:, o_wi:o_wi + IDX_HEADS] * (IDX_HEADS ** -0.5 * IDX_DIM ** -0.5)).T)
    small = small.at[8:8 + N_HEADS].set(w_in[:, o_fc:o_gc].T)
    wt = jnp.concatenate([wqa.reshape(2 * w, d), values(2 * w),
                          (w_in[:, 3 * w:4 * w] * q_scale).T, values(5 * w),
                          (w_in[:, 6 * w:7 * w] * q_scale).T, values(8 * w),
                          w_in[:, o_gc:o_ql].T, w_in[:, o_ql:o_ki].T, small], axis=0).astype(BF16)
    assert wt.shape[0] == T_ROWS
    wr = jnp.concatenate([w_in[:, w:2 * w], w_in[:, 4 * w:5 * w], w_in[:, 7 * w:8 * w], w_in[:, o_ki:o_wi],
                          jnp.zeros((d, 128 - IDX_DIM), w_in.dtype), w_in[:, o_kv:o_ki]], axis=1).astype(BF16)
    assert wr.shape[1] == R_COLS
    wqd = (w_dq_up * q_scale).T.reshape(N_HEADS, HEAD_DIM, DSA_RANK)
    wqd = jnp.stack([jnp.zeros((N_HEADS, HEAD_DIM, DSA_RANK), wqd.dtype).at[hd].set(wqd[hd]).reshape(w, DSA_RANK)
                     for hd in range(N_HEADS)], axis=0).reshape(N_HEADS * w, DSA_RANK).astype(BF16)
    wqi = w_didx_q.T.astype(BF16)
    wvd = _head_rows(w_dkv_up[:, w:], V_ROWS).reshape(V_GROUP, DSA_RANK).astype(BF16)
    wkd = w_dkv_up[:, :w].astype(BF16)
    ones_rows = jnp.zeros((N_HEADS, V_ROWS, 1), F32).at[:, HEAD_DIM].set(1.0).reshape(V_GROUP, 1)
    seg = (jnp.arange(w)[:, None] // HEAD_DIM == jnp.arange(128)[None, :]).astype(BF16)

    tm = ROW_TILE
    nt = seq // tm
    feat = lambda rows: pl.BlockSpec((None, rows, tm), lambda b, i: (b, 0, i))
    keyh = pl.BlockSpec((None, N_HEADS, tm, HEAD_DIM), lambda b, i: (b, 0, i, 0))
    tok = lambda width: pl.BlockSpec((None, tm, width), lambda b, i: (b, i, 0))
    vec = pl.BlockSpec((None, 1, d), lambda b, i: (b, 0, 0))
    whole = lambda a: _resident(a.shape, lambda b, i: (0,) * a.ndim)
    head = lambda rows: pl.BlockSpec((None, N_HEADS, rows, tm), lambda b, i: (b, 0, 0, i))
    halves = pl.BlockSpec((None, N_HEADS, 2, HEAD_DIM, tm), lambda b, i: (b, 0, 0, 0, i))
    f_bf = lambda rows: jax.ShapeDtypeStruct((bsz, rows, seq), BF16)
    h_bf = lambda rows: jax.ShapeDtypeStruct((bsz, N_HEADS, rows, seq), BF16)
    k_bf = jax.ShapeDtypeStruct((bsz, N_HEADS, seq, HEAD_DIM), BF16)
    outs = pl.pallas_call(
        _inproj_kernel,
        grid=(bsz, nt),
        in_specs=[tok(d), pl.BlockSpec((1, d), lambda b, i: (0, 0)), vec, vec,
                  whole(wt), whole(wr), whole(wqd), whole(wqi), whole(wvd), whole(wkd), whole(ones_rows), whole(seg)],
        out_specs=[halves, head(V_ROWS), head(HEAD_DIM), head(V_ROWS), head(HEAD_DIM), head(V_ROWS), head(HEAD_DIM),
                   head(w), feat(w), head(V_ROWS), feat(16),
                   keyh, keyh, keyh, tok(IDX_DIM), tok(w),
                   pl.BlockSpec((None, None, 8, 128), lambda b, i: (b, i, 0, 0))],
        out_shape=[jax.ShapeDtypeStruct((bsz, N_HEADS, 2, HEAD_DIM, seq), BF16), h_bf(V_ROWS), h_bf(HEAD_DIM),
                   h_bf(V_ROWS), h_bf(HEAD_DIM), h_bf(V_ROWS),
                   jax.ShapeDtypeStruct((bsz, N_HEADS, HEAD_DIM, seq), F32),
                   h_bf(w), f_bf(w), h_bf(V_ROWS), jax.ShapeDtypeStruct((bsz, 16, seq), F32),
                   k_bf, k_bf, k_bf, jax.ShapeDtypeStruct((bsz, seq, IDX_DIM), BF16),
                   jax.ShapeDtypeStruct((bsz, seq, w), BF16),
                   jax.ShapeDtypeStruct((bsz, nt, 8, 128), F32)],
        scratch_shapes=[pltpu.VMEM((d, tm), BF16)],
        compiler_params=_cparams(2),
        name="input_projection",
    )(x, gain.reshape(1, d), sc, sh, wt, wr, wqd, wqi, wvd, wkd, ones_rows, seg)
    k_norm_max = jnp.sqrt(jnp.max(outs[-1], axis=1))[:, :4, :N_HEADS]
    return outs[:-1], k_norm_max


def _diff_kernel(t5_ref, bmax_ref, kmax_ref, lam_ref, q_ref, k_ref, v_ref, g_ref, o_ref, bias_ref, m_ref, acc_ref,
                 *, tile, head0, heads, group_heads, out_scale):
    b = pl.program_id(0)
    hg = pl.program_id(1)
    i = pl.program_id(2)

    @pl.when(i == 0)
    def _():
        for h in range(heads):
            _fill_t5_tile(bias_ref, 2 * h, t5_ref, head0 + hg * heads + h, 0, tile)
            _fill_t5_tile(bias_ref, 2 * h + 1, t5_ref, head0 + hg * heads + h, tile, tile)

    far_bias = [LOG2E * t5_ref[(T5_BUCKETS - 1) * N_BIAS_HEADS + head0 + hg * heads + h] for h in range(heads)]
    bounds = [_logit_bound(q_ref[h, c], kmax_ref[b * group_heads + hg * heads + h], bmax_ref[head0 + hg * heads + h])
              for h in range(heads) for c in range(2)]

    def run(online):
        def step(j, kind):
            k0 = pl.multiple_of(j * tile, tile)
            logits = [jnp.dot(k_ref[h, pl.ds(k0, tile), :], q_ref[h, c], preferred_element_type=F32)
                      for h in range(heads) for c in range(2)]
            for h in range(heads):
                vt = v_ref[h, :, pl.ds(k0, tile)]
                for c in range(2):
                    idx = 2 * h + c
                    s, shift = logits[idx], bounds[idx]
                    if kind != "far":
                        s = s + bias_ref[2 * h + (1 if kind == "near" else 0)]
                    elif online:
                        s = s + far_bias[h]
                    else:
                        shift = shift - far_bias[h]
                    if kind == "diag":
                        s = jnp.where(_causal_mask(tile), s, NEG)
                    _softmax_accumulate(s, vt, acc_ref, idx, shift=shift, m_ref=m_ref if online else None)

        step(i, "diag")

        @pl.when(i >= 1)
        def _():
            step(i - 1, "near")

        def far(j, carry):
            step(j, "far")
            return carry

        _unrolled_loop(jnp.maximum(i - 1, 0), 2, far, 0)

    _two_pass_attention(run, acc_ref, m_ref, 2 * heads)

    for h in range(heads):
        o = _normalized(acc_ref, 2 * h) - lam_ref[0] * _normalized(acc_ref, 2 * h + 1)
        y = o * lax.rsqrt(jnp.mean(o * o, axis=0, keepdims=True) + EPS)
        o_ref[h] = ((y * g_ref[...]) * out_scale).astype(o_ref.dtype)


def _diff_attention(q_t, k, v_aug, lam, subln_gain, t5_flat, bias_max, k_norm_max, out_scale):
    bsz, nh, _, dh, seq = q_t.shape
    tile = min(ATTN_TILE, seq)
    hps = DIFF_HEADS_PER_STEP
    assert tile >= T5_FAR_DIST and seq % tile == 0 and nh % hps == 0
    kern = functools.partial(_diff_kernel, tile=tile, head0=0, heads=hps, group_heads=nh, out_scale=out_scale)
    return pl.pallas_call(
        kern,
        grid=(bsz, nh // hps, seq // tile),
        in_specs=[_smem(), _smem(), _smem(), _smem(),
                  pl.BlockSpec((None, hps, 2, dh, tile), lambda b, h, i: (b, h, 0, 0, i)),
                  _resident((None, hps, seq, dh), lambda b, h, i: (b, h, 0, 0)),
                  _resident((None, hps, V_ROWS, seq), lambda b, h, i: (b, h, 0, 0)),
                  pl.BlockSpec((dh, 1), lambda b, h, i: (0, 0))],
        out_specs=pl.BlockSpec((None, hps, dh, tile), lambda b, h, i: (b, h, 0, i)),
        out_shape=jax.ShapeDtypeStruct((bsz, nh, dh, seq), BF16),
        scratch_shapes=[pltpu.VMEM((2 * hps, tile, tile), F32),
                        pltpu.VMEM((2 * hps, 1, tile), F32),
                        pltpu.VMEM((2 * hps, V_ROWS, tile), F32)],
        compiler_params=_cparams(3),
        name="diff_attention",
    )(t5_flat, bias_max, k_norm_max, lam.reshape(1), q_t, k, v_aug, subln_gain.reshape(dh, 1))


def _kmean_kernel(k_ref, o_ref, *, blocks):
    k = k_ref[...].astype(F32)
    o_ref[...] = jnp.mean(k.reshape(blocks, MOBA_BLOCK, k.shape[-1]), axis=1)


def _moba_kmean(k):
    bsz, nh, seq, dh = k.shape
    nb = seq // MOBA_BLOCK
    blocks = min(8, nb)
    return pl.pallas_call(
        functools.partial(_kmean_kernel, blocks=blocks),
        grid=(bsz, nh, nb // blocks),
        in_specs=[pl.BlockSpec((None, None, blocks * MOBA_BLOCK, dh), lambda b, h, i: (b, h, i, 0))],
        out_specs=pl.BlockSpec((None, None, blocks, dh), lambda b, h, i: (b, h, i, 0)),
        out_shape=jax.ShapeDtypeStruct((bsz, nh, nb, dh), F32),
        compiler_params=_cparams(3),
    )(k)


def _moba_kernel(t5_ref, bmax_ref, kmax_ref, q_ref, k_ref, v_ref, km_ref, o_ref, bias_ref, sel_ref, m_ref, acc_ref,
                 *, block, head0, heads, n_blocks):
    b = pl.program_id(0)
    t = pl.program_id(1)
    qt = 2 * block
    halves = (slice(0, block), slice(block, qt))

    @pl.when(t == 0)
    def _():
        for h in range(heads):
            _fill_t5_tile(bias_ref, 2 * h, t5_ref, head0 + h, 0, block)
            _fill_t5_tile(bias_ref, 2 * h + 1, t5_ref, head0 + h, block, block)

    far_bias = [LOG2E * t5_ref[(T5_BUCKETS - 1) * N_BIAS_HEADS + head0 + h] for h in range(heads)]
    bounds = [_logit_bound(q_ref[h], kmax_ref[b * heads + h], bmax_ref[head0 + h]) for h in range(heads)]

    nidx = lax.broadcasted_iota(I32, (n_blocks, qt), 0)
    own = 2 * t + jnp.where(lax.broadcasted_iota(I32, (n_blocks, qt), 1) >= block, 1, 0)
    for h in range(heads):
        gate = jnp.dot(km_ref[h], q_ref[h], preferred_element_type=F32)
        g = jnp.where(nidx < own, gate, NEG)
        sel = jnp.zeros(gate.shape, F32)
        for _ in range(min(MOBA_TOPK, n_blocks)):
            mx = jnp.max(g, axis=0, keepdims=True)
            first = jnp.min(jnp.where(g == mx, nidx, n_blocks), axis=0, keepdims=True)
            pick = nidx == first
            sel = jnp.where(pick, 1.0, sel)
            g = jnp.where(pick, -jnp.inf, g)
        sel_ref[h] = jnp.where(nidx < own, sel, 0.0)

    def run(online):
        def step(n, kinds):
            k0 = pl.multiple_of(n * block, block)
            active = [hf for hf in range(2) if kinds[hf] != "skip"]
            cols = slice(halves[active[0]].start, halves[active[-1]].stop)
            logits = [jnp.dot(k_ref[h, pl.ds(k0, block), :], q_ref[h, :, cols], preferred_element_type=F32)
                      for h in range(heads)]
            merged = len(active) == 2 and kinds[0] == kinds[1] == "far"
            segments = [(cols, "far")] if merged else [(halves[hf], kinds[hf]) for hf in active]
            for h in range(heads):
                vt = v_ref[h, :, pl.ds(k0, block)]
                for seg, kind in segments:
                    s = logits[h][:, seg.start - cols.start:seg.stop - cols.start]
                    shift = bounds[h][:, seg]
                    if kind != "far":
                        s = s + bias_ref[2 * h + (1 if kind == "near" else 0)]
                    elif online:
                        s = s + far_bias[h]
                    else:
                        shift = shift - far_bias[h]
                    mask = _causal_mask(block) if kind == "diag" else sel_ref[h, pl.ds(n, 1), seg] > 0.5
                    s = jnp.where(mask, s, NEG)
                    _softmax_accumulate(s, vt, acc_ref, h, shift=shift, m_ref=m_ref if online else None,
                                        keep=mask, cols=seg)

        step(2 * t + 1, ("skip", "diag"))
        step(2 * t, ("diag", "near"))

        @pl.when(t >= 1)
        def _():
            step(2 * t - 1, ("near", "far"))

        def far(n, carry):
            step(n, ("far", "far"))
            return carry

        _unrolled_loop(jnp.maximum(2 * t - 1, 0), 2, far, 0)

    _two_pass_attention(run, acc_ref, m_ref, heads)
    for h in range(heads):
        o_ref[h] = _normalized(acc_ref, h).astype(o_ref.dtype)


def _moba_attention(q_t, k, v_aug, kmean, t5_flat, bias_max, k_norm_max):
    bsz, nh, dh, seq = q_t.shape
    block = MOBA_BLOCK
    qt = 2 * block
    assert seq % qt == 0 and block >= T5_FAR_DIST
    nb = seq // block
    kern = functools.partial(_moba_kernel, block=block, head0=N_HEADS, heads=nh, n_blocks=nb)
    return pl.pallas_call(
        kern,
        grid=(bsz, seq // qt),
        in_specs=[_smem(), _smem(), _smem(),
                  pl.BlockSpec((None, nh, dh, qt), lambda b, i: (b, 0, 0, i)),
                  _resident((None, nh, seq, dh), lambda b, i: (b, 0, 0, 0)),
                  _resident((None, nh, V_ROWS, seq), lambda b, i: (b, 0, 0, 0)),
                  _resident((None, nh, nb, dh), lambda b, i: (b, 0, 0, 0))],
        out_specs=pl.BlockSpec((None, nh, dh, qt), lambda b, i: (b, 0, 0, i)),
        out_shape=jax.ShapeDtypeStruct((bsz, nh, dh, seq), BF16),
        scratch_shapes=[pltpu.VMEM((2 * nh, block, block), F32), pltpu.VMEM((nh, nb, qt), F32),
                        pltpu.VMEM((nh, 1, qt), F32),
                        pltpu.VMEM((nh, V_ROWS, qt), F32)],
        compiler_params=_cparams(2),
        name="moba_attention",
    )(t5_flat, bias_max, k_norm_max, q_t, k, v_aug, kmean)


def _split3(x):
    def trunc(v):
        bits = lax.bitcast_convert_type(v, I32)
        return lax.bitcast_convert_type(bits & jnp.int32(-65536), F32)
    hi = trunc(x)
    r1 = x - hi
    mid = trunc(r1)
    lo = trunc(r1 - mid)
    return hi, mid, lo


def _forget_cumsum_kernel(f_ref, hi_ref, mid_ref, lo_ref, *, heads, rows):
    upper = (lax.broadcasted_iota(I32, (128, 128), 0) <= lax.broadcasted_iota(I32, (128, 128), 1)).astype(F32)
    strict = (lax.broadcasted_iota(I32, (rows, rows), 1) < lax.broadcasted_iota(I32, (rows, rows), 0)).astype(F32)
    for h in range(heads):
        x = jnp.concatenate([f_ref[h:h + 1, r * 128:(r + 1) * 128] for r in range(rows)], axis=0)
        ls = jnp.minimum(x, 0.0) - jnp.log(1.0 + jnp.exp(-jnp.abs(x)))
        within = jnp.dot(ls, upper, preferred_element_type=F32, precision=lax.Precision.HIGHEST)
        before = jnp.dot(strict, within, preferred_element_type=F32, precision=lax.Precision.HIGHEST)
        cf = (within + before[:, 127:128]) * LOG2E
        for part, out_ref in zip(_split3(cf), (hi_ref, mid_ref, lo_ref)):
            for r in range(rows):
                out_ref[h:h + 1, r * 128:(r + 1) * 128] = part[r:r + 1, :]


def _forget_cumsum(f_logit):
    bsz, nh, seq = f_logit.shape
    spec = pl.BlockSpec((None, nh, seq), lambda b: (b, 0, 0))
    return pl.pallas_call(
        functools.partial(_forget_cumsum_kernel, heads=nh, rows=seq // 128),
        grid=(bsz,),
        in_specs=[spec],
        out_specs=[spec] * 3,
        out_shape=[jax.ShapeDtypeStruct((bsz, nh, seq), F32)] * 3,
        compiler_params=_cparams(1),
        name="forget_cumsum",
    )(f_logit)


def _forget_kernel(kmax_ref, q_ref, k_ref, v_ref, gate_ref, o_ref, m_ref, acc_ref, *, tile, heads, group_heads):
    b = pl.program_id(0)
    hg = pl.program_id(1)
    i = pl.program_id(2)
    bounds = [_logit_bound(q_ref[h, :HEAD_DIM, :], kmax_ref[b * group_heads + hg * heads + h], 0.0)
              for h in range(heads)]

    def run(online):
        def step(j, blocks, mask):
            k0 = pl.multiple_of(j * tile, tile)
            logits = [jnp.dot(k_ref[h, pl.ds(k0, blocks * tile), :], q_ref[h], preferred_element_type=F32)
                      for h in range(heads)]
            for h in range(heads):
                s = logits[h] if mask is None else jnp.where(mask, logits[h], NEG)
                _softmax_accumulate(s, v_ref[h, :, pl.ds(k0, blocks * tile)], acc_ref, h, shift=bounds[h],
                                    m_ref=m_ref if online else None)

        step(i, 1, _causal_mask(tile))
        n_big = i // FORGET_BLOCKS_PER_STEP

        def past_big(j, carry):
            step(j * FORGET_BLOCKS_PER_STEP, FORGET_BLOCKS_PER_STEP, None)
            return carry

        def past_single(j, carry):
            step(j, 1, None)
            return carry

        lax.fori_loop(0, n_big, past_big, 0)
        lax.fori_loop(n_big * FORGET_BLOCKS_PER_STEP, i, past_single, 0)

    _two_pass_attention(run, acc_ref, m_ref, heads)
    for h in range(heads):
        g = gate_ref[h]
        o_ref[h] = (_normalized(acc_ref, h) * (1.0 / (1.0 + jnp.exp(-g)))).astype(o_ref.dtype)


def _forgetting_attention(q_aug_t, k_aug, v_aug, gate_t, k_norm_max):
    bsz, nh, kdim, seq = q_aug_t.shape
    dh = gate_t.shape[2]
    tile = min(ATTN_TILE, seq)
    hps = FORGET_HEADS_PER_STEP
    assert seq % tile == 0 and nh % hps == 0
    return pl.pallas_call(
        functools.partial(_forget_kernel, tile=tile, heads=hps, group_heads=nh),
        grid=(bsz, nh // hps, seq // tile),
        in_specs=[_smem(),
                  pl.BlockSpec((None, hps, kdim, tile), lambda b, h, i: (b, h, 0, i)),
                  _resident((None, hps, seq, kdim), lambda b, h, i: (b, h, 0, 0)),
                  _resident((None, hps, V_ROWS, seq), lambda b, h, i: (b, h, 0, 0)),
                  pl.BlockSpec((None, hps, dh, tile), lambda b, h, i: (b, h, 0, i))],
        out_specs=pl.BlockSpec((None, hps, dh, tile), lambda b, h, i: (b, h, 0, i)),
        out_shape=jax.ShapeDtypeStruct((bsz, nh, dh, seq), BF16),
        scratch_shapes=[pltpu.VMEM((hps, 1, tile), F32),
                        pltpu.VMEM((hps, V_ROWS, tile), F32)],
        compiler_params=_cparams(3),
        name="forget_attention",
    )(k_norm_max, q_aug_t, k_aug, v_aug, gate_t)


def _key_to_float(u):
    ks = u ^ jnp.int32(-2147483648)
    bits = jnp.where(ks < 0, ks ^ jnp.int32(2147483647), ks)
    return lax.bitcast_convert_type(bits, F32)


def _truncate_to_bf16(x):
    bits = lax.bitcast_convert_type(x, I32) & jnp.int32(-65536)
    return lax.bitcast_convert_type(bits, F32).astype(BF16)


def _dsa_kernel(t5_ref, bmax_ref, kmax_ref, qi_ref, w_ref, ki_ref, q_ref, k_ref, v_ref, o_ref,
                sc_ref, hi_ref, bias_ref, tri_ref, m_ref, acc_ref, *, tile, head0, topk):
    b = pl.program_id(0)
    i = pl.program_id(1)

    @pl.when((b == 0) & (i == 0))
    def _():
        for h in range(N_HEADS):
            _fill_t5_tile(bias_ref, 2 * h, t5_ref, head0 + h, 0, tile)
            _fill_t5_tile(bias_ref, 2 * h + 1, t5_ref, head0 + h, tile, tile)
        tri_ref[...] = jnp.where(lax.broadcasted_iota(I32, (tile, tile), 1) <= lax.broadcasted_iota(I32, (tile, tile), 0),
                                 1.0, 0.0).astype(BF16)

    causal = _causal_mask(tile)

    def index_scores(j):
        k0 = pl.multiple_of(j * tile, tile)
        kt = ki_ref[pl.ds(k0, tile), :]
        raws = [jnp.dot(kt, qi_ref[h * IDX_DIM:(h + 1) * IDX_DIM, :], preferred_element_type=F32)
                for h in range(IDX_HEADS)]
        sc = jnp.zeros((tile, tile), F32)
        for h in range(IDX_HEADS):
            sc = sc + jnp.maximum(raws[h], 0.0) * w_ref[h:h + 1, :]
        return k0, sc

    def store_scores(k0, sc):
        sc_ref[pl.ds(k0, tile), :] = sc
        hi_ref[pl.ds(k0, tile), :] = _truncate_to_bf16(sc)

    def with_tile_max(mx, sc):
        return jnp.maximum(mx, jnp.max(sc.reshape(tile // 8, 8, tile), axis=0))

    def score_body(j, mx):
        k0, sc = index_scores(j)
        store_scores(k0, sc)
        return with_tile_max(mx, sc)

    mx = _unrolled_loop(i, 4, score_body, jnp.full((8, tile), NEG, F32))
    k0, sc = index_scores(i)
    sc = jnp.where(causal, sc, NEG)
    store_scores(k0, sc)
    top_score = jnp.max(with_tile_max(mx, sc), axis=0, keepdims=True)
    top_bits = lax.bitcast_convert_type(jnp.where(top_score == 0.0, 0.0, top_score), I32)
    top_key = jnp.where(top_bits < 0, top_bits ^ jnp.int32(2147483647), top_bits)

    def count(pred):
        def body(j, cnt):
            k0 = pl.multiple_of(j * tile, tile)
            ind = jnp.where(pred(sc_ref[pl.ds(k0, tile), :]), 1.0, 0.0)
            return cnt + jnp.sum(ind.reshape(tile // 8, 8, tile), axis=0)
        cnt = _unrolled_loop(i + 1, 4, body, jnp.zeros((8, tile), F32))
        return jnp.sum(cnt, axis=0, keepdims=True)

    def count_truncated(thr):
        def body(j, cnt):
            k0 = pl.multiple_of(j * tile, tile)
            ind = jnp.where(hi_ref[pl.ds(k0, tile), :] >= thr, jnp.ones((), BF16), jnp.zeros((), BF16))
            part = ind[0:BF16_ROWS]
            for r in range(1, tile // BF16_ROWS):
                part = part + ind[r * BF16_ROWS:(r + 1) * BF16_ROWS]
            return cnt + part.astype(F32)
        cnt = _unrolled_loop(i + 1, 4, body, jnp.zeros((BF16_ROWS, tile), F32))
        return jnp.sum(cnt, axis=0, keepdims=True)

    def high_bit(it, ans):
        cand = ans | jnp.left_shift(jnp.int32(1), 31 - it)
        hopeless = jnp.min(jnp.where((cand ^ jnp.int32(-2147483648)) > top_key, 1, 0)) > 0
        return lax.cond(
            hopeless, lambda: ans,
            lambda: jnp.where(count_truncated(_truncate_to_bf16(_key_to_float(cand))) >= topk, cand, ans))

    ans = lax.fori_loop(0, 16, high_bit, jnp.zeros((1, tile), I32))

    above = count(lambda s: s > _key_to_float(ans))
    settled = above < topk
    need = jnp.where(settled, topk - above, 0.0)

    def low_bit(state):
        it, ans, open_f, need = state
        cand = ans | jnp.left_shift(jnp.int32(1), 31 - it)
        thr = _key_to_float(cand)
        cnt = count(lambda s: s >= thr)
        unsettled = open_f > 0.5
        ans = jnp.where(unsettled & (cnt >= topk), cand, ans)
        exact = unsettled & (cnt == topk)
        return it + 1, ans, jnp.where(exact, 0.0, open_f), jnp.where(exact, TAKE_ALL_TIES, need)

    _, ans, open_f, need = lax.while_loop(lambda st: (st[0] < 32) & (jnp.max(st[2]) > 0.5), low_bit,
                                          (jnp.int32(16), ans, jnp.where(settled, 0.0, 1.0), need))
    tau = _key_to_float(ans)
    need = jnp.where(open_f > 0.5, topk - count(lambda s: s > tau), need)

    qs = [q_ref[h] for h in range(N_HEADS)]

    far_bias = [LOG2E * t5_ref[(T5_BUCKETS - 1) * N_BIAS_HEADS + head0 + h] for h in range(N_HEADS)]
    bounds = [_logit_bound(qs[h], kmax_ref[b * N_HEADS + h], bmax_ref[head0 + h]) for h in range(N_HEADS)]

    def run(online):
        def attend(j, ties_seen, kind):
            k0 = pl.multiple_of(j * tile, tile)
            sc = sc_ref[pl.ds(k0, tile), :]
            eq = sc == tau
            eqf = jnp.where(eq, 1.0, 0.0)
            rank = ties_seen + jnp.dot(tri_ref[...], eqf.astype(BF16), preferred_element_type=F32)
            wgt = jnp.where(eq, jnp.where(rank <= need, 1.0, 0.0), jnp.where(sc > tau, 1.0, 0.0))
            if kind == "diag":
                wgt = jnp.where(causal, wgt, 0.0)
            keep = wgt > 0.5
            kt = k_ref[pl.ds(k0, tile), :]
            logits = [jnp.dot(kt, qs[h], preferred_element_type=F32) for h in range(N_HEADS)]
            for h in range(N_HEADS):
                s, shift = logits[h], bounds[h]
                if kind != "far":
                    s = s + bias_ref[2 * h + (1 if kind == "near" else 0)]
                elif online:
                    s = s + far_bias[h]
                else:
                    shift = shift - far_bias[h]
                s = jnp.where(keep, s, NEG)
                _softmax_accumulate(s, v_ref[h, :, pl.ds(k0, tile)], acc_ref, h, shift=shift,
                                    m_ref=m_ref if online else None, keep=keep)
            return ties_seen + jnp.sum(eqf, axis=0, keepdims=True)

        seen = _unrolled_loop(jnp.maximum(i - 1, 0), 4, lambda j, r: attend(j, r, "far"), jnp.zeros((1, tile), F32))
        seen = lax.cond(i >= 1, lambda r: attend(i - 1, r, "near"), lambda r: r, seen)
        attend(i, seen, "diag")

    _two_pass_attention(run, acc_ref, m_ref, N_HEADS)
    for h in range(N_HEADS):
        o_ref[h * HEAD_DIM:(h + 1) * HEAD_DIM, :] = _normalized(acc_ref, h).astype(o_ref.dtype)


def _dsa_attention(qi_t, w_t, k_idx, q_t, k, v_aug, t5_flat, bias_max, k_norm_max):
    bsz, _, width, seq = q_t.shape
    tile = min(SPARSE_TILE, seq)
    topk = min(DSA_TOPK_MAX, seq // 4)
    assert seq % tile == 0 and tile >= topk and tile >= T5_FAR_DIST
    kern = functools.partial(_dsa_kernel, tile=tile, head0=2 * N_HEADS, topk=topk)
    return pl.pallas_call(
        kern,
        grid=(bsz, seq // tile),
        in_specs=[_smem(), _smem(), _smem(),
                  pl.BlockSpec((None, qi_t.shape[1], tile), lambda b, i: (b, 0, i)),
                  pl.BlockSpec((None, 8, tile), lambda b, i: (b, 0, i)),
                  _resident((None, seq, k_idx.shape[2]), lambda b, i: (b, 0, 0)),
                  pl.BlockSpec((None, N_HEADS, width, tile), lambda b, i: (b, 0, 0, i)),
                  _resident((None, seq, width), lambda b, i: (b, 0, 0)),
                  _resident((None, N_HEADS, V_ROWS, seq), lambda b, i: (b, 0, 0, 0))],
        out_specs=pl.BlockSpec((None, width, tile), lambda b, i: (b, 0, i)),
        out_shape=jax.ShapeDtypeStruct((bsz, width, seq), BF16),
        scratch_shapes=[pltpu.VMEM((seq, tile), F32), pltpu.VMEM((seq, tile), BF16),
                        pltpu.VMEM((2 * N_HEADS, tile, tile), F32),
                        pltpu.VMEM((tile, tile), BF16),
                        pltpu.VMEM((N_HEADS, 1, tile), F32),
                        pltpu.VMEM((N_HEADS, V_ROWS, tile), F32)],
        compiler_params=_cparams(2),
        name="dsa_attention",
    )(t5_flat, bias_max, k_norm_max, qi_t, w_t, k_idx, q_t, k, v_aug)


def _outproj_kernel(x_ref, oa_ref, ob_ref, oc_ref, od_ref, w_ref, g_ref, y_ref):
    o_t = jnp.concatenate([r[...].reshape(GROUP_W, r.shape[-1]) for r in (oa_ref, ob_ref, oc_ref, od_ref)], axis=0)
    y = lax.dot_general(o_t, w_ref[...], (((0,), (0,)), ((), ())), preferred_element_type=F32)
    y_ref[...] = x_ref[...] + g_ref[...] * y


def _output_projection(x, o_groups, w_out, gate):
    bsz, seq, d = x.shape
    tm = ROW_TILE
    row = pl.BlockSpec((None, tm, d), lambda b, i: (b, i, 0))

    def grp(o):
        if o.ndim == 4:
            return pl.BlockSpec((None,) + o.shape[1:3] + (tm,), lambda b, i: (b, 0, 0, i))
        return pl.BlockSpec((None, GROUP_W, tm), lambda b, i: (b, 0, i))

    return pl.pallas_call(
        _outproj_kernel,
        grid=(bsz, seq // tm),
        in_specs=[row] + [grp(o) for o in o_groups] + [
                  _resident(w_out.shape, lambda b, i: (0, 0)),
                  pl.BlockSpec((None, 1, d), lambda b, i: (b, 0, 0))],
        out_specs=row,
        out_shape=jax.ShapeDtypeStruct(x.shape, F32),
        compiler_params=_cparams(2),
        name="output_projection",
    )(x, *o_groups, w_out.astype(BF16), gate)


HALO = 8


def _ffn_kernel(x_ref, xp_ref, gain_ref, sc_ref, sh_ref, g_ref, wup_ref, cw_ref, cb_ref, wd_ref, fg_ref,
                y_ref, h_ref, acc_ref, *, tm, chunk, n_chunks, final_norm):
    i = pl.program_id(1)

    def modulated(x):
        return ((_rms(x) * gain_ref[...]) * (1.0 + sc_ref[...]) + sh_ref[...]).astype(BF16)

    h_ref[HALO:, :] = modulated(x_ref[...])
    halo = modulated(xp_ref[...])
    h_ref[:HALO, :] = jnp.where(i > 0, halo, jnp.zeros_like(halo))
    acc_ref[...] = jnp.zeros(acc_ref.shape, F32)
    h = h_ref[...]

    def up(c0, half):
        col = pl.multiple_of(half * D_FF + c0, 128)
        return jnp.dot(h, wup_ref[:, pl.ds(col, chunk)], preferred_element_type=F32)

    def conv(u, c0, half):
        cw = cw_ref[half, :, pl.ds(c0, chunk)]
        return (cw[0:1] * u[HALO - 2:HALO - 2 + tm] + cw[1:2] * u[HALO - 1:HALO - 1 + tm]
                + cw[2:3] * u[HALO:HALO + tm]) + cb_ref[half, :, pl.ds(c0, chunk)]

    def chunks(first, count):
        c0s = [pl.multiple_of((first + n) * chunk, chunk) for n in range(count)]
        ups = [(up(c0, 0), up(c0, 1)) for c0 in c0s]
        acts = []
        for c0, (ug, uv) in zip(c0s, ups):
            gate = conv(ug, c0, 0)
            acts.append(((gate * (1.0 / (1.0 + jnp.exp(-gate)))) * conv(uv, c0, 1)).astype(BF16))
        a = acts[0] if count == 1 else jnp.concatenate(acts, axis=1)
        acc_ref[...] += jnp.dot(a, wd_ref[pl.ds(c0s[0], count * chunk), :], preferred_element_type=F32)

    def pair(p, carry):
        chunks(2 * p, 2)
        return carry

    lax.fori_loop(0, n_chunks // 2, pair, 0)
    if n_chunks % 2:
        chunks(n_chunks - 1, 1)

    y = x_ref[...] + g_ref[...] * acc_ref[...]
    if final_norm:
        y = _rms(y) * fg_ref[...]
    y_ref[...] = y


def _ffn(x, gain, sc, sh, gate, w_up, conv_w, conv_b, w_down, final_gain, final_norm):
    bsz, seq, d = x.shape
    tm = min(FFN_ROW_TILE, seq)
    fc = FFN_CHUNK
    n_chunks = D_FF // fc
    w_up_b = w_up.astype(BF16)
    cw = conv_w.reshape(3, 2, D_FF).transpose(1, 0, 2)
    cb = conv_b.reshape(2, 1, D_FF)
    vec = pl.BlockSpec((None, 1, d), lambda b, i: (b, 0, 0))
    one = pl.BlockSpec((1, d), lambda b, i: (0, 0))
    row = pl.BlockSpec((None, tm, d), lambda b, i: (b, i, 0))
    whole = lambda a: _resident(a.shape, lambda b, i: (0,) * a.ndim)
    w_down_b = w_down.astype(BF16)
    return pl.pallas_call(
        functools.partial(_ffn_kernel, tm=tm, chunk=fc, n_chunks=n_chunks, final_norm=final_norm),
        grid=(bsz, seq // tm),
        in_specs=[row,
                  pl.BlockSpec((None, HALO, d), lambda b, i: (b, jnp.maximum(i * (tm // HALO) - 1, 0), 0)),
                  one, vec, vec, vec, whole(w_up_b), whole(cw), whole(cb), whole(w_down_b), one],
        out_specs=row,
        out_shape=jax.ShapeDtypeStruct(x.shape, F32),
        scratch_shapes=[pltpu.VMEM((HALO + tm, d), BF16), pltpu.VMEM((tm, d), F32)],
        compiler_params=_cparams(2),
        name="conv_glu_ffn",
    )(x, x, gain.reshape(1, d), sc, sh, gate, w_up_b, cw, cb, w_down_b, final_gain.reshape(1, d))


def kernel(x, c, w_ada, b_ada, norm1_gain, w_in, f_bias, diff_lambda, diff_subln_gain, w_dq_up, w_didx_q,
           w_dkv_up, w_out, t5_table, norm2_gain, w_ffn_up, ffn_conv_w, ffn_conv_b, w_ffn_down, final_gain):
    bsz, seq, d = x.shape
    depth = w_ada.shape[0]
    t5_flat = t5_table.reshape(-1)
    bias_max = jnp.max(t5_table, axis=0) * LOG2E
    mod = _modulation(c, w_ada, b_ada)
    w = GROUP_W

    for l in range(depth):
        sh1, sc1, g1, sh2, sc2, g2 = [m[:, None, :] for m in jnp.split(mod[l], 6, axis=-1)]
        (qa, va, qb, vb, qc, vc, gc, qd, qi_t, vd, misc, ka, kb, kc, k_idx, kd), k_norm_max = _input_projection(
            x, norm1_gain[l], sc1, sh1, w_in[l], w_dq_up[l], w_didx_q[l], w_dkv_up[l])
        kmax = [k_norm_max[:, g].reshape(-1) for g in range(4)]

        lambda_init = 0.8 - 0.6 * math.exp(-0.3 * l)
        lq1, lk1, lq2, lk2 = diff_lambda[l]
        lam = jnp.exp(jnp.sum(lq1 * lk1)) - jnp.exp(jnp.sum(lq2 * lk2)) + lambda_init
        o_a = _diff_attention(qa, ka, va, lam, diff_subln_gain[l], t5_flat, bias_max, kmax[0], 1.0 - lambda_init)

        o_b = _moba_attention(qb, kb, vb, _moba_kmean(kb).astype(BF16), t5_flat, bias_max, kmax[1])

        hi, mid, lo = _forget_cumsum(misc[:, 8:8 + N_HEADS] + f_bias[l][None, :, None])
        ones = jnp.ones_like(hi)
        cparts = jnp.stack([hi, mid, lo], axis=2)
        onep = jnp.stack([ones] * 3, axis=2)
        padw = 2 * HEAD_DIM - HEAD_DIM - 6
        qc_aug = jnp.concatenate([qc, onep.astype(BF16), cparts.astype(BF16),
                                  jnp.zeros((bsz, N_HEADS, padw, seq), BF16)], axis=2)
        kc_aug = jnp.concatenate([kc, (-cparts).transpose(0, 1, 3, 2).astype(BF16),
                                  onep.transpose(0, 1, 3, 2).astype(BF16),
                                  jnp.zeros((bsz, N_HEADS, seq, padw), BF16)], axis=3)
        o_c = _forgetting_attention(qc_aug, kc_aug, vc, gc, kmax[2])

        o_d = _dsa_attention(qi_t, misc[:, :8], k_idx, qd, kd, vd, t5_flat, bias_max, kmax[3])

        x = _output_projection(x, [o_a, o_b, o_c, o_d], w_out[l], g1)
        x = _ffn(x, norm2_gain[l], sc2, sh2, g2, w_ffn_up[l], ffn_conv_w[l], ffn_conv_b[l], w_ffn_down[l],
                 final_gain, final_norm=(l == depth - 1))
    return x
```

```python
import functools
import math

import numpy as np
import jax
import jax.numpy as jnp
from jax import lax
from jax.experimental import pallas as pl
from jax.experimental.pallas import tpu as pltpu

F32 = jnp.float32
BF16 = jnp.bfloat16
I32 = jnp.int32

HEAD_DIM = 64
N_HEADS = 4
GROUP_W = N_HEADS * HEAD_DIM
DIFF_HALF = HEAD_DIM // 2
MOBA_BLOCK = 256
MOBA_TOPK = 3
DSA_RANK = 128
IDX_HEADS = 4
IDX_DIM = 64
DSA_TOPK_MAX = 256
T5_BUCKETS = 32
T5_MAX_DIST = 128
N_BIAS_HEADS = 12
D_FF = 2816
EPS = 1e-6
NEG = -1e30
LOG2E = math.log2(math.e)
V_ROWS = HEAD_DIM + 16
BOUND_SLACK = 1.0 + 2.0 ** -6
MIN_DENOMINATOR = 2.0 ** -60
BF16_ROWS = 16
TAKE_ALL_TIES = 1e9

V7X_VMEM_LIMIT_BYTES = 56 * 1024 * 1024

ATTN_TILE = 512
SPARSE_TILE = 256
ROW_TILE = 512
FFN_ROW_TILE = 1024
FFN_CHUNK = 256
FORGET_HEADS_PER_STEP = 4
DIFF_HEADS_PER_STEP = 2
FORGET_BLOCKS_PER_STEP = 2


def _t5_thresholds():
    d = np.arange(0, 4 * T5_MAX_DIST)
    max_exact = T5_BUCKETS // 2
    ratio = np.maximum(d, 1).astype(np.float32) / max_exact
    large = max_exact + (np.log(ratio) / math.log(T5_MAX_DIST / max_exact) * (T5_BUCKETS - max_exact)).astype(np.int32)
    bucket = np.where(d < max_exact, d, np.minimum(large, T5_BUCKETS - 1))
    return [int(np.argmax(bucket >= b)) for b in range(T5_BUCKETS)]


T5_THRESH = _t5_thresholds()
T5_FAR_DIST = T5_THRESH[-1]


def _cparams(n_axes):
    return pltpu.CompilerParams(dimension_semantics=("arbitrary",) * n_axes,
                                vmem_limit_bytes=V7X_VMEM_LIMIT_BYTES)


def _resident(block_shape, index_map):
    return pl.BlockSpec(block_shape, index_map, pipeline_mode=pl.Buffered(1))


def _smem():
    return pl.BlockSpec(memory_space=pltpu.SMEM)


def _fill_t5_tile(bias_ref, slot, t5_ref, head, delta, tile):
    rows = 64

    def body(r, carry):
        r0 = pl.multiple_of(r * rows, rows)
        kr = lax.broadcasted_iota(I32, (rows, tile), 0) + r0
        qc = lax.broadcasted_iota(I32, (rows, tile), 1)
        d = delta + qc - kr
        val = jnp.full((rows, tile), LOG2E * t5_ref[head], F32)
        for b in range(1, T5_BUCKETS):
            val = jnp.where(d >= T5_THRESH[b], LOG2E * t5_ref[b * N_BIAS_HEADS + head], val)
        bias_ref[slot, pl.ds(r0, rows), :] = val
        return carry

    lax.fori_loop(0, tile // rows, body, 0)


def _unrolled_loop(n, unroll, body, carry):
    def group(g, c):
        for u in range(unroll):
            c = body(g * unroll + u, c)
        return c

    n_groups = n // unroll
    carry = lax.fori_loop(0, n_groups, group, carry)
    return lax.fori_loop(n_groups * unroll, n, body, carry)


def _causal_mask(tile):
    kr = lax.broadcasted_iota(I32, (tile, tile), 0)
    qc = lax.broadcasted_iota(I32, (tile, tile), 1)
    return kr <= qc


def _softmax_accumulate(s, v_aug, acc_ref, idx, *, shift=None, m_ref=None, keep=None, cols=slice(None)):
    if m_ref is None:
        p = jnp.exp2(s - shift)
        acc_ref[idx, :, cols] += jnp.dot(v_aug, p.astype(BF16), preferred_element_type=F32)
        return
    m_old = m_ref[idx, :, cols]
    m_new = jnp.maximum(m_old, jnp.max(s, axis=0, keepdims=True))
    p = jnp.exp2(s - m_new)
    if keep is not None:
        p = jnp.where(keep, p, 0.0)
    acc_ref[idx, :, cols] = (jnp.exp2(m_old - m_new) * acc_ref[idx, :, cols]
                             + jnp.dot(v_aug, p.astype(BF16), preferred_element_type=F32))
    m_ref[idx, :, cols] = m_new


def _logit_bound(q, k_norm_max, extra):
    qf = q.astype(F32)
    q_norm = jnp.sqrt(jnp.sum(qf * qf, axis=0, keepdims=True))
    return q_norm * (k_norm_max * BOUND_SLACK) + (extra + 1.0)


def _denominators_ok(acc_ref, n_chains):
    low = acc_ref[0, HEAD_DIM:HEAD_DIM + 1, :]
    for idx in range(1, n_chains):
        low = jnp.minimum(low, acc_ref[idx, HEAD_DIM:HEAD_DIM + 1, :])
    return jnp.min(low) > MIN_DENOMINATOR


def _two_pass_attention(run, acc_ref, m_ref, n_chains):
    acc_ref[...] = jnp.zeros(acc_ref.shape, F32)
    run(False)

    @pl.when(jnp.logical_not(_denominators_ok(acc_ref, n_chains)))
    def _():
        acc_ref[...] = jnp.zeros(acc_ref.shape, F32)
        m_ref[...] = jnp.full(m_ref.shape, NEG, F32)
        run(True)


def _normalized(acc_ref, idx):
    return acc_ref[idx, :HEAD_DIM, :] / acc_ref[idx, HEAD_DIM:HEAD_DIM + 1, :]


def _mod_kernel(c_ref, w_ref, b_ref, o_ref):
    c = c_ref[...]
    cond = c * (1.0 / (1.0 + jnp.exp(-c)))
    o_ref[...] = jnp.dot(cond.astype(BF16), w_ref[...].astype(BF16), preferred_element_type=F32) + b_ref[...]


def _modulation(c, w_ada, b_ada):
    depth, d, n = w_ada.shape
    bsz = c.shape[0]
    rows = 8
    c_pad = jnp.zeros((rows, d), F32).at[:bsz].set(c)
    tn = 1024
    out = pl.pallas_call(
        _mod_kernel,
        grid=(depth, n // tn),
        in_specs=[pl.BlockSpec((rows, d), lambda l, j: (0, 0)),
                  pl.BlockSpec((None, d, tn), lambda l, j: (l, 0, j)),
                  pl.BlockSpec((None, 1, tn), lambda l, j: (l, 0, j))],
        out_specs=pl.BlockSpec((None, rows, tn), lambda l, j: (l, 0, j)),
        out_shape=jax.ShapeDtypeStruct((depth, rows, n), F32),
        compiler_params=_cparams(2),
        name="adaln_modulation",
    )(c_pad, w_ada, b_ada.reshape(depth, 1, n))
    return out[:, :bsz]


def _rms(x):
    return x * lax.rsqrt(jnp.mean(x * x, axis=-1, keepdims=True) + EPS)


V_GROUP = N_HEADS * V_ROWS
T_QA, T_VA = 0, 2 * GROUP_W
T_QB = T_VA + V_GROUP
T_VB = T_QB + GROUP_W
T_QC = T_VB + V_GROUP
T_VC = T_QC + GROUP_W
T_GC = T_VC + V_GROUP
T_QLAT = T_GC + GROUP_W
T_KVLAT = T_QLAT + DSA_RANK
T_MISC = T_KVLAT + DSA_RANK
T_ROWS = T_MISC + 16
R_KA, R_KB, R_KC, R_KIDX, R_KVLAT, R_COLS = 0, GROUP_W, 2 * GROUP_W, 3 * GROUP_W, 3 * GROUP_W + 128, 4 * GROUP_W


def _inproj_kernel(x_ref, gain_ref, sc_ref, sh_ref, wt_ref, wr_ref, wqd_ref, wqi_ref, wvd_ref, wkd_ref, ones_ref,
                   seg_ref, qa_ref, va_ref, qb_ref, vb_ref, qc_ref, vc_ref, gc_ref, qd_ref, qi_ref, vd_ref, misc_ref,
                   ka_ref, kb_ref, kc_ref, ki_ref, kd_ref, kn_ref, ht_ref):
    x = x_ref[...]
    h = (_rms(x) * gain_ref[...]) * (1.0 + sc_ref[...]) + sh_ref[...]
    h_r = h.astype(BF16)
    ht_ref[...] = h.T.astype(BF16)
    h_t = ht_ref[...]
    ones_rows = ones_ref[...]

    def rms_t(z):
        return z * lax.rsqrt(jnp.mean(z * z, axis=0, keepdims=True) + EPS)

    def proj_t(r0, r1):
        return jnp.dot(wt_ref[r0:r1, :], h_t, preferred_element_type=F32)

    def put(out_ref, rows):
        out_ref[...] = rows.astype(out_ref.dtype).reshape(out_ref.shape)

    put(qa_ref, proj_t(T_QA, T_VA))
    put(va_ref, proj_t(T_VA, T_QB) + ones_rows)
    put(qb_ref, proj_t(T_QB, T_VB))
    put(vb_ref, proj_t(T_VB, T_QC) + ones_rows)
    put(qc_ref, proj_t(T_QC, T_VC))
    put(vc_ref, proj_t(T_VC, T_GC) + ones_rows)
    put(gc_ref, proj_t(T_GC, T_QLAT))
    q_lat = rms_t(proj_t(T_QLAT, T_KVLAT)).astype(BF16)
    kv_lat_t = rms_t(proj_t(T_KVLAT, T_MISC)).astype(BF16)
    put(qd_ref, jnp.dot(wqd_ref[...], q_lat, preferred_element_type=F32))
    qi_ref[...] = jnp.dot(wqi_ref[...], q_lat, preferred_element_type=F32).astype(BF16)
    put(vd_ref, jnp.dot(wvd_ref[...], kv_lat_t, preferred_element_type=F32) + ones_rows)
    misc_ref[...] = proj_t(T_MISC, T_ROWS)

    p_r = jnp.dot(h_r, wr_ref[...], preferred_element_type=F32)
    kv_lat_r = _rms(p_r[:, R_KVLAT:R_COLS]).astype(BF16)
    keys = [p_r[:, R_KA:R_KB].astype(BF16), p_r[:, R_KB:R_KC].astype(BF16), p_r[:, R_KC:R_KIDX].astype(BF16),
            jnp.dot(kv_lat_r, wkd_ref[...], preferred_element_type=F32).astype(BF16)]
    for k, out_ref in zip(keys[:3], (ka_ref, kb_ref, kc_ref)):
        for hd in range(N_HEADS):
            out_ref[hd] = k[:, hd * HEAD_DIM:(hd + 1) * HEAD_DIM]
    kd_ref[...] = keys[3]
    ki_ref[...] = p_r[:, R_KIDX:R_KIDX + IDX_DIM].astype(BF16)

    norms = []
    for k in keys:
        kf = k.astype(F32)
        sq = jnp.dot((kf * kf).astype(BF16), seg_ref[...], preferred_element_type=F32)
        norms.append(jnp.max(sq, axis=0, keepdims=True))
    kn_ref[...] = jnp.concatenate(norms + [jnp.zeros((8 - len(norms), 128), F32)], axis=0)


def _head_rows(w_cols, rows_per_head):
    d = w_cols.shape[0]
    w = w_cols.T.reshape(N_HEADS, HEAD_DIM, d)
    return jnp.concatenate([w, jnp.zeros((N_HEADS, rows_per_head - HEAD_DIM, d), w.dtype)], axis=1)


def _input_projection(x, gain, sc, sh, w_in, w_dq_up, w_didx_q, w_dkv_up):
    bsz, seq, d = x.shape
    w = GROUP_W
    o_fc = 9 * w
    o_gc = o_fc + N_HEADS
    o_ql = o_gc + w
    o_kv = o_ql + DSA_RANK
    o_ki = o_kv + DSA_RANK
    o_wi = o_ki + IDX_DIM
    q_scale_a = DIFF_HALF ** -0.5 * LOG2E
    q_scale = HEAD_DIM ** -0.5 * LOG2E
    wqa = (w_in[:, :w] * q_scale_a).T.reshape(N_HEADS, 2, DIFF_HALF, d)
    zero = jnp.zeros((N_HEADS, DIFF_HALF, d), w_in.dtype)
    wqa = jnp.stack([jnp.concatenate([wqa[:, 0], zero], axis=1), jnp.concatenate([zero, wqa[:, 1]], axis=1)], axis=1)
    values = lambda c0: _head_rows(w_in[:, c0:c0 + w], V_ROWS).reshape(V_GROUP, d)
    small = jnp.zeros((16, d), w_in.dtype)
    small = small.at[:IDX_HEADS].set((w_in[:, o_wi:o_wi + IDX_HEADS] * (IDX_HEADS ** -0.5 * IDX_DIM ** -0.5)).T)
    small = small.at[8:8 + N_HEADS].set(w_in[:, o_fc:o_gc].T)
    wt = jnp.concatenate([wqa.reshape(2 * w, d), values(2 * w),
                          (w_in[:, 3 * w:4 * w] * q_scale).T, values(5 * w),
                          (w_in[:, 6 * w:7 * w] * q_scale).T, values(8 * w),
                          w_in[:, o_gc:o_ql].T, w_in[:, o_ql:o_ki].T, small], axis=0).astype(BF16)
    assert wt.shape[0] == T_ROWS
    wr = jnp.concatenate([w_in[:, w:2 * w], w_in[:, 4 * w:5 * w], w_in[:, 7 * w:8 * w], w_in[:, o_ki:o_wi],
                          jnp.zeros((d, 128 - IDX_DIM), w_in.dtype), w_in[:, o_kv:o_ki]], axis=1).astype(BF16)
    assert wr.shape[1] == R_COLS
    wqd = (w_dq_up * q_scale).T.reshape(N_HEADS, HEAD_DIM, DSA_RANK)
    wqd = jnp.stack([jnp.zeros((N_HEADS, HEAD_DIM, DSA_RANK), wqd.dtype).at[hd].set(wqd[hd]).reshape(w, DSA_RANK)
                     for hd in range(N_HEADS)], axis=0).reshape(N_HEADS * w, DSA_RANK).astype(BF16)
    wqi = w_didx_q.T.astype(BF16)
    wvd = _head_rows(w_dkv_up[:, w:], V_ROWS).reshape(V_GROUP, DSA_RANK).astype(BF16)
    wkd = w_dkv_up[:, :w].astype(BF16)
    ones_rows = jnp.zeros((N_HEADS, V_ROWS, 1), F32).at[:, HEAD_DIM].set(1.0).reshape(V_GROUP, 1)
    seg = (jnp.arange(w)[:, None] // HEAD_DIM == jnp.arange(128)[None, :]).astype(BF16)

    tm = ROW_TILE
    nt = seq // tm
    feat = lambda rows: pl.BlockSpec((None, rows, tm), lambda b, i: (b, 0, i))
    keyh = pl.BlockSpec((None, N_HEADS, tm, HEAD_DIM), lambda b, i: (b, 0, i, 0))
    tok = lambda width: pl.BlockSpec((None, tm, width), lambda b, i: (b, i, 0))
    vec = pl.BlockSpec((None, 1, d), lambda b, i: (b, 0, 0))
    whole = lambda a: _resident(a.shape, lambda b, i: (0,) * a.ndim)
    head = lambda rows: pl.BlockSpec((None, N_HEADS, rows, tm), lambda b, i: (b, 0, 0, i))
    halves = pl.BlockSpec((None, N_HEADS, 2, HEAD_DIM, tm), lambda b, i: (b, 0, 0, 0, i))
    f_bf = lambda rows: jax.ShapeDtypeStruct((bsz, rows, seq), BF16)
    h_bf = lambda rows: jax.ShapeDtypeStruct((bsz, N_HEADS, rows, seq), BF16)
    k_bf = jax.ShapeDtypeStruct((bsz, N_HEADS, seq, HEAD_DIM), BF16)
    outs = pl.pallas_call(
        _inproj_kernel,
        grid=(bsz, nt),
        in_specs=[tok(d), pl.BlockSpec((1, d), lambda b, i: (0, 0)), vec, vec,
                  whole(wt), whole(wr), whole(wqd), whole(wqi), whole(wvd), whole(wkd), whole(ones_rows), whole(seg)],
        out_specs=[halves, head(V_ROWS), head(HEAD_DIM), head(V_ROWS), head(HEAD_DIM), head(V_ROWS), head(HEAD_DIM),
                   head(w), feat(w), head(V_ROWS), feat(16),
                   keyh, keyh, keyh, tok(IDX_DIM), tok(w),
                   pl.BlockSpec((None, None, 8, 128), lambda b, i: (b, i, 0, 0))],
        out_shape=[jax.ShapeDtypeStruct((bsz, N_HEADS, 2, HEAD_DIM, seq), BF16), h_bf(V_ROWS), h_bf(HEAD_DIM),
                   h_bf(V_ROWS), h_bf(HEAD_DIM), h_bf(V_ROWS),
                   jax.ShapeDtypeStruct((bsz, N_HEADS, HEAD_DIM, seq), F32),
                   h_bf(w), f_bf(w), h_bf(V_ROWS), jax.ShapeDtypeStruct((bsz, 16, seq), F32),
                   k_bf, k_bf, k_bf, jax.ShapeDtypeStruct((bsz, seq, IDX_DIM), BF16),
                   jax.ShapeDtypeStruct((bsz, seq, w), BF16),
                   jax.ShapeDtypeStruct((bsz, nt, 8, 128), F32)],
        scratch_shapes=[pltpu.VMEM((d, tm), BF16)],
        compiler_params=_cparams(2),
        name="input_projection",
    )(x, gain.reshape(1, d), sc, sh, wt, wr, wqd, wqi, wvd, wkd, ones_rows, seg)
    k_norm_max = jnp.sqrt(jnp.max(outs[-1], axis=1))[:, :4, :N_HEADS]
    return outs[:-1], k_norm_max


def _diff_kernel(t5_ref, bmax_ref, kmax_ref, lam_ref, q_ref, k_ref, v_ref, g_ref, o_ref, bias_ref, m_ref, acc_ref,
                 *, tile, head0, heads, group_heads, out_scale):
    b = pl.program_id(0)
    hg = pl.program_id(1)
    i = pl.program_id(2)

    @pl.when(i == 0)
    def _():
        for h in range(heads):
            _fill_t5_tile(bias_ref, 2 * h, t5_ref, head0 + hg * heads + h, 0, tile)
            _fill_t5_tile(bias_ref, 2 * h + 1, t5_ref, head0 + hg * heads + h, tile, tile)

    far_bias = [LOG2E * t5_ref[(T5_BUCKETS - 1) * N_BIAS_HEADS + head0 + hg * heads + h] for h in range(heads)]
    bounds = [_logit_bound(q_ref[h, c], kmax_ref[b * group_heads + hg * heads + h], bmax_ref[head0 + hg * heads + h])
              for h in range(heads) for c in range(2)]

    def run(online):
        def step(j, kind):
            k0 = pl.multiple_of(j * tile, tile)
            logits = [jnp.dot(k_ref[h, pl.ds(k0, tile), :], q_ref[h, c], preferred_element_type=F32)
                      for h in range(heads) for c in range(2)]
            for h in range(heads):
                vt = v_ref[h, :, pl.ds(k0, tile)]
                for c in range(2):
                    idx = 2 * h + c
                    s, shift = logits[idx], bounds[idx]
                    if kind != "far":
                        s = s + bias_ref[2 * h + (1 if kind == "near" else 0)]
                    elif online:
                        s = s + far_bias[h]
                    else:
                        shift = shift - far_bias[h]
                    if kind == "diag":
                        s = jnp.where(_causal_mask(tile), s, NEG)
                    _softmax_accumulate(s, vt, acc_ref, idx, shift=shift, m_ref=m_ref if online else None)

        step(i, "diag")

        @pl.when(i >= 1)
        def _():
            step(i - 1, "near")

        def far(j, carry):
            step(j, "far")
            return carry

        _unrolled_loop(jnp.maximum(i - 1, 0), 2, far, 0)

    _two_pass_attention(run, acc_ref, m_ref, 2 * heads)

    for h in range(heads):
        o = _normalized(acc_ref, 2 * h) - lam_ref[0] * _normalized(acc_ref, 2 * h + 1)
        y = o * lax.rsqrt(jnp.mean(o * o, axis=0, keepdims=True) + EPS)
        o_ref[h] = ((y * g_ref[...]) * out_scale).astype(o_ref.dtype)


def _diff_attention(q_t, k, v_aug, lam, subln_gain, t5_flat, bias_max, k_norm_max, out_scale):
    bsz, nh, _, dh, seq = q_t.shape
    tile = min(ATTN_TILE, seq)
    hps = DIFF_HEADS_PER_STEP
    assert tile >= T5_FAR_DIST and seq % tile == 0 and nh % hps == 0
    kern = functools.partial(_diff_kernel, tile=tile, head0=0, heads=hps, group_heads=nh, out_scale=out_scale)
    return pl.pallas_call(
        kern,
        grid=(bsz, nh // hps, seq // tile),
        in_specs=[_smem(), _smem(), _smem(), _smem(),
                  pl.BlockSpec((None, hps, 2, dh, tile), lambda b, h, i: (b, h, 0, 0, i)),
                  _resident((None, hps, seq, dh), lambda b, h, i: (b, h, 0, 0)),
                  _resident((None, hps, V_ROWS, seq), lambda b, h, i: (b, h, 0, 0)),
                  pl.BlockSpec((dh, 1), lambda b, h, i: (0, 0))],
        out_specs=pl.BlockSpec((None, hps, dh, tile), lambda b, h, i: (b, h, 0, i)),
        out_shape=jax.ShapeDtypeStruct((bsz, nh, dh, seq), BF16),
        scratch_shapes=[pltpu.VMEM((2 * hps, tile, tile), F32),
                        pltpu.VMEM((2 * hps, 1, tile), F32),
                        pltpu.VMEM((2 * hps, V_ROWS, tile), F32)],
        compiler_params=_cparams(3),
        name="diff_attention",
    )(t5_flat, bias_max, k_norm_max, lam.reshape(1), q_t, k, v_aug, subln_gain.reshape(dh, 1))


def _kmean_kernel(k_ref, o_ref, *, blocks):
    k = k_ref[...].astype(F32)
    o_ref[...] = jnp.mean(k.reshape(blocks, MOBA_BLOCK, k.shape[-1]), axis=1)


def _moba_kmean(k):
    bsz, nh, seq, dh = k.shape
    nb = seq // MOBA_BLOCK
    blocks = min(8, nb)
    return pl.pallas_call(
        functools.partial(_kmean_kernel, blocks=blocks),
        grid=(bsz, nh, nb // blocks),
        in_specs=[pl.BlockSpec((None, None, blocks * MOBA_BLOCK, dh), lambda b, h, i: (b, h, i, 0))],
        out_specs=pl.BlockSpec((None, None, blocks, dh), lambda b, h, i: (b, h, i, 0)),
        out_shape=jax.ShapeDtypeStruct((bsz, nh, nb, dh), F32),
        compiler_params=_cparams(3),
    )(k)


def _moba_kernel(t5_ref, bmax_ref, kmax_ref, q_ref, k_ref, v_ref, km_ref, o_ref, bias_ref, sel_ref, m_ref, acc_ref,
                 *, block, head0, heads, n_blocks):
    b = pl.program_id(0)
    t = pl.program_id(1)
    qt = 2 * block
    halves = (slice(0, block), slice(block, qt))

    @pl.when(t == 0)
    def _():
        for h in range(heads):
            _fill_t5_tile(bias_ref, 2 * h, t5_ref, head0 + h, 0, block)
            _fill_t5_tile(bias_ref, 2 * h + 1, t5_ref, head0 + h, block, block)

    far_bias = [LOG2E * t5_ref[(T5_BUCKETS - 1) * N_BIAS_HEADS + head0 + h] for h in range(heads)]
    bounds = [_logit_bound(q_ref[h], kmax_ref[b * heads + h], bmax_ref[head0 + h]) for h in range(heads)]

    nidx = lax.broadcasted_iota(I32, (n_blocks, qt), 0)
    own = 2 * t + jnp.where(lax.broadcasted_iota(I32, (n_blocks, qt), 1) >= block, 1, 0)
    for h in range(heads):
        gate = jnp.dot(km_ref[h], q_ref[h], preferred_element_type=F32)
        g = jnp.where(nidx < own, gate, NEG)
        sel = jnp.zeros(gate.shape, F32)
        for _ in range(min(MOBA_TOPK, n_blocks)):
            mx = jnp.max(g, axis=0, keepdims=True)
            first = jnp.min(jnp.where(g == mx, nidx, n_blocks), axis=0, keepdims=True)
            pick = nidx == first
            sel = jnp.where(pick, 1.0, sel)
            g = jnp.where(pick, -jnp.inf, g)
        sel_ref[h] = jnp.where(nidx < own, sel, 0.0)

    def run(online):
        def step(n, kinds):
            k0 = pl.multiple_of(n * block, block)
            active = [hf for hf in range(2) if kinds[hf] != "skip"]
            cols = slice(halves[active[0]].start, halves[active[-1]].stop)
            logits = [jnp.dot(k_ref[h, pl.ds(k0, block), :], q_ref[h, :, cols], preferred_element_type=F32)
                      for h in range(heads)]
            merged = len(active) == 2 and kinds[0] == kinds[1] == "far"
            segments = [(cols, "far")] if merged else [(halves[hf], kinds[hf]) for hf in active]
            for h in range(heads):
                vt = v_ref[h, :, pl.ds(k0, block)]
                for seg, kind in segments:
                    s = logits[h][:, seg.start - cols.start:seg.stop - cols.start]
                    shift = bounds[h][:, seg]
                    if kind != "far":
                        s = s + bias_ref[2 * h + (1 if kind == "near" else 0)]
                    elif online:
                        s = s + far_bias[h]
                    else:
                        shift = shift - far_bias[h]
                    mask = _causal_mask(block) if kind == "diag" else sel_ref[h, pl.ds(n, 1), seg] > 0.5
                    s = jnp.where(mask, s, NEG)
                    _softmax_accumulate(s, vt, acc_ref, h, shift=shift, m_ref=m_ref if online else None,
                                        keep=mask, cols=seg)

        step(2 * t + 1, ("skip", "diag"))
        step(2 * t, ("diag", "near"))

        @pl.when(t >= 1)
        def _():
            step(2 * t - 1, ("near", "far"))

        def far(n, carry):
            step(n, ("far", "far"))
            return carry

        _unrolled_loop(jnp.maximum(2 * t - 1, 0), 2, far, 0)

    _two_pass_attention(run, acc_ref, m_ref, heads)
    for h in range(heads):
        o_ref[h] = _normalized(acc_ref, h).astype(o_ref.dtype)


def _moba_attention(q_t, k, v_aug, kmean, t5_flat, bias_max, k_norm_max):
    bsz, nh, dh, seq = q_t.shape
    block = MOBA_BLOCK
    qt = 2 * block
    assert seq % qt == 0 and block >= T5_FAR_DIST
    nb = seq // block
    kern = functools.partial(_moba_kernel, block=block, head0=N_HEADS, heads=nh, n_blocks=nb)
    return pl.pallas_call(
        kern,
        grid=(bsz, seq // qt),
        in_specs=[_smem(), _smem(), _smem(),
                  pl.BlockSpec((None, nh, dh, qt), lambda b, i: (b, 0, 0, i)),
                  _resident((None, nh, seq, dh), lambda b, i: (b, 0, 0, 0)),
                  _resident((None, nh, V_ROWS, seq), lambda b, i: (b, 0, 0, 0)),
                  _resident((None, nh, nb, dh), lambda b, i: (b, 0, 0, 0))],
        out_specs=pl.BlockSpec((None, nh, dh, qt), lambda b, i: (b, 0, 0, i)),
        out_shape=jax.ShapeDtypeStruct((bsz, nh, dh, seq), BF16),
        scratch_shapes=[pltpu.VMEM((2 * nh, block, block), F32), pltpu.VMEM((nh, nb, qt), F32),
                        pltpu.VMEM((nh, 1, qt), F32),
                        pltpu.VMEM((nh, V_ROWS, qt), F32)],
        compiler_params=_cparams(2),
        name="moba_attention",
    )(t5_flat, bias_max, k_norm_max, q_t, k, v_aug, kmean)


def _split3(x):
    def trunc(v):
        bits = lax.bitcast_convert_type(v, I32)
        return lax.bitcast_convert_type(bits & jnp.int32(-65536), F32)
    hi = trunc(x)
    r1 = x - hi
    mid = trunc(r1)
    lo = trunc(r1 - mid)
    return hi, mid, lo


def _forget_cumsum_kernel(f_ref, hi_ref, mid_ref, lo_ref, *, heads, rows):
    upper = (lax.broadcasted_iota(I32, (128, 128), 0) <= lax.broadcasted_iota(I32, (128, 128), 1)).astype(F32)
    strict = (lax.broadcasted_iota(I32, (rows, rows), 1) < lax.broadcasted_iota(I32, (rows, rows), 0)).astype(F32)
    for h in range(heads):
        x = jnp.concatenate([f_ref[h:h + 1, r * 128:(r + 1) * 128] for r in range(rows)], axis=0)
        ls = jnp.minimum(x, 0.0) - jnp.log(1.0 + jnp.exp(-jnp.abs(x)))
        within = jnp.dot(ls, upper, preferred_element_type=F32, precision=lax.Precision.HIGHEST)
        before = jnp.dot(strict, within, preferred_element_type=F32, precision=lax.Precision.HIGHEST)
        cf = (within + before[:, 127:128]) * LOG2E
        for part, out_ref in zip(_split3(cf), (hi_ref, mid_ref, lo_ref)):
            for r in range(rows):
                out_ref[h:h + 1, r * 128:(r + 1) * 128] = part[r:r + 1, :]


def _forget_cumsum(f_logit):
    bsz, nh, seq = f_logit.shape
    spec = pl.BlockSpec((None, nh, seq), lambda b: (b, 0, 0))
    return pl.pallas_call(
        functools.partial(_forget_cumsum_kernel, heads=nh, rows=seq // 128),
        grid=(bsz,),
        in_specs=[spec],
        out_specs=[spec] * 3,
        out_shape=[jax.ShapeDtypeStruct((bsz, nh, seq), F32)] * 3,
        compiler_params=_cparams(1),
        name="forget_cumsum",
    )(f_logit)


def _forget_kernel(kmax_ref, q_ref, k_ref, v_ref, gate_ref, o_ref, m_ref, acc_ref, *, tile, heads, group_heads):
    b = pl.program_id(0)
    hg = pl.program_id(1)
    i = pl.program_id(2)
    bounds = [_logit_bound(q_ref[h, :HEAD_DIM, :], kmax_ref[b * group_heads + hg * heads + h], 0.0)
              for h in range(heads)]

    def run(online):
        def step(j, blocks, mask):
            k0 = pl.multiple_of(j * tile, tile)
            logits = [jnp.dot(k_ref[h, pl.ds(k0, blocks * tile), :], q_ref[h], preferred_element_type=F32)
                      for h in range(heads)]
            for h in range(heads):
                s = logits[h] if mask is None else jnp.where(mask, logits[h], NEG)
                _softmax_accumulate(s, v_ref[h, :, pl.ds(k0, blocks * tile)], acc_ref, h, shift=bounds[h],
                                    m_ref=m_ref if online else None)

        step(i, 1, _causal_mask(tile))
        n_big = i // FORGET_BLOCKS_PER_STEP

        def past_big(j, carry):
            step(j * FORGET_BLOCKS_PER_STEP, FORGET_BLOCKS_PER_STEP, None)
            return carry

        def past_single(j, carry):
            step(j, 1, None)
            return carry

        lax.fori_loop(0, n_big, past_big, 0)
        lax.fori_loop(n_big * FORGET_BLOCKS_PER_STEP, i, past_single, 0)

    _two_pass_attention(run, acc_ref, m_ref, heads)
    for h in range(heads):
        g = gate_ref[h]
        o_ref[h] = (_normalized(acc_ref, h) * (1.0 / (1.0 + jnp.exp(-g)))).astype(o_ref.dtype)


def _forgetting_attention(q_aug_t, k_aug, v_aug, gate_t, k_norm_max):
    bsz, nh, kdim, seq = q_aug_t.shape
    dh = gate_t.shape[2]
    tile = min(ATTN_TILE, seq)
    hps = FORGET_HEADS_PER_STEP
    assert seq % tile == 0 and nh % hps == 0
    return pl.pallas_call(
        functools.partial(_forget_kernel, tile=tile, heads=hps, group_heads=nh),
        grid=(bsz, nh // hps, seq // tile),
        in_specs=[_smem(),
                  pl.BlockSpec((None, hps, kdim, tile), lambda b, h, i: (b, h, 0, i)),
                  _resident((None, hps, seq, kdim), lambda b, h, i: (b, h, 0, 0)),
                  _resident((None, hps, V_ROWS, seq), lambda b, h, i: (b, h, 0, 0)),
                  pl.BlockSpec((None, hps, dh, tile), lambda b, h, i: (b, h, 0, i))],
        out_specs=pl.BlockSpec((None, hps, dh, tile), lambda b, h, i: (b, h, 0, i)),
        out_shape=jax.ShapeDtypeStruct((bsz, nh, dh, seq), BF16),
        scratch_shapes=[pltpu.VMEM((hps, 1, tile), F32),
                        pltpu.VMEM((hps, V_ROWS, tile), F32)],
        compiler_params=_cparams(3),
        name="forget_attention",
    )(k_norm_max, q_aug_t, k_aug, v_aug, gate_t)


def _key_to_float(u):
    ks = u ^ jnp.int32(-2147483648)
    bits = jnp.where(ks < 0, ks ^ jnp.int32(2147483647), ks)
    return lax.bitcast_convert_type(bits, F32)


def _truncate_to_bf16(x):
    bits = lax.bitcast_convert_type(x, I32) & jnp.int32(-65536)
    return lax.bitcast_convert_type(bits, F32).astype(BF16)


def _dsa_kernel(t5_ref, bmax_ref, kmax_ref, qi_ref, w_ref, ki_ref, q_ref, k_ref, v_ref, o_ref,
                sc_ref, hi_ref, bias_ref, tri_ref, m_ref, acc_ref, *, tile, head0, topk):
    b = pl.program_id(0)
    i = pl.program_id(1)

    @pl.when((b == 0) & (i == 0))
    def _():
        for h in range(N_HEADS):
            _fill_t5_tile(bias_ref, 2 * h, t5_ref, head0 + h, 0, tile)
            _fill_t5_tile(bias_ref, 2 * h + 1, t5_ref, head0 + h, tile, tile)
        tri_ref[...] = jnp.where(lax.broadcasted_iota(I32, (tile, tile), 1) <= lax.broadcasted_iota(I32, (tile, tile), 0),
                                 1.0, 0.0).astype(BF16)

    causal = _causal_mask(tile)

    def index_scores(j):
        k0 = pl.multiple_of(j * tile, tile)
        kt = ki_ref[pl.ds(k0, tile), :]
        raws = [jnp.dot(kt, qi_ref[h * IDX_DIM:(h + 1) * IDX_DIM, :], preferred_element_type=F32)
                for h in range(IDX_HEADS)]
        sc = jnp.zeros((tile, tile), F32)
        for h in range(IDX_HEADS):
            sc = sc + jnp.maximum(raws[h], 0.0) * w_ref[h:h + 1, :]
        return k0, sc

    def store_scores(k0, sc):
        sc_ref[pl.ds(k0, tile), :] = sc
        hi_ref[pl.ds(k0, tile), :] = _truncate_to_bf16(sc)

    def score_body(j, carry):
        store_scores(*index_scores(j))
        return carry

    _unrolled_loop(i, 4, score_body, 0)
    k0, sc = index_scores(i)
    store_scores(k0, jnp.where(causal, sc, NEG))

    def count(pred):
        def body(j, cnt):
            k0 = pl.multiple_of(j * tile, tile)
            ind = jnp.where(pred(sc_ref[pl.ds(k0, tile), :]), 1.0, 0.0)
            return cnt + jnp.sum(ind.reshape(tile // 8, 8, tile), axis=0)
        cnt = _unrolled_loop(i + 1, 4, body, jnp.zeros((8, tile), F32))
        return jnp.sum(cnt, axis=0, keepdims=True)

    def count_truncated(thr):
        def body(j, cnt):
            k0 = pl.multiple_of(j * tile, tile)
            ind = jnp.where(hi_ref[pl.ds(k0, tile), :] >= thr, jnp.ones((), BF16), jnp.zeros((), BF16))
            part = ind[0:BF16_ROWS]
            for r in range(1, tile // BF16_ROWS):
                part = part + ind[r * BF16_ROWS:(r + 1) * BF16_ROWS]
            return cnt + part.astype(F32)
        cnt = _unrolled_loop(i + 1, 4, body, jnp.zeros((BF16_ROWS, tile), F32))
        return jnp.sum(cnt, axis=0, keepdims=True)

    def high_bit(it, ans):
        cand = ans | jnp.left_shift(jnp.int32(1), 31 - it)
        return jnp.where(count_truncated(_truncate_to_bf16(_key_to_float(cand))) >= topk, cand, ans)

    ans = lax.fori_loop(0, 16, high_bit, jnp.zeros((1, tile), I32))

    above = count(lambda s: s > _key_to_float(ans))
    settled = above < topk
    need = jnp.where(settled, topk - above, 0.0)

    def low_bit(state):
        it, ans, open_f, need = state
        cand = ans | jnp.left_shift(jnp.int32(1), 31 - it)
        thr = _key_to_float(cand)
        cnt = count(lambda s: s >= thr)
        unsettled = open_f > 0.5
        ans = jnp.where(unsettled & (cnt >= topk), cand, ans)
        exact = unsettled & (cnt == topk)
        return it + 1, ans, jnp.where(exact, 0.0, open_f), jnp.where(exact, TAKE_ALL_TIES, need)

    _, ans, open_f, need = lax.while_loop(lambda st: (st[0] < 32) & (jnp.max(st[2]) > 0.5), low_bit,
                                          (jnp.int32(16), ans, jnp.where(settled, 0.0, 1.0), need))
    tau = _key_to_float(ans)
    need = jnp.where(open_f > 0.5, topk - count(lambda s: s > tau), need)

    qs = [q_ref[h] for h in range(N_HEADS)]

    far_bias = [LOG2E * t5_ref[(T5_BUCKETS - 1) * N_BIAS_HEADS + head0 + h] for h in range(N_HEADS)]
    bounds = [_logit_bound(qs[h], kmax_ref[b * N_HEADS + h], bmax_ref[head0 + h]) for h in range(N_HEADS)]

    def run(online):
        def attend(j, ties_seen, kind):
            k0 = pl.multiple_of(j * tile, tile)
            sc = sc_ref[pl.ds(k0, tile), :]
            eq = sc == tau
            eqf = jnp.where(eq, 1.0, 0.0)
            rank = ties_seen + jnp.dot(tri_ref[...], eqf.astype(BF16), preferred_element_type=F32)
            wgt = jnp.where(eq, jnp.where(rank <= need, 1.0, 0.0), jnp.where(sc > tau, 1.0, 0.0))
            if kind == "diag":
                wgt = jnp.where(causal, wgt, 0.0)
            keep = wgt > 0.5
            kt = k_ref[pl.ds(k0, tile), :]
            logits = [jnp.dot(kt, qs[h], preferred_element_type=F32) for h in range(N_HEADS)]
            for h in range(N_HEADS):
                s, shift = logits[h], bounds[h]
                if kind != "far":
                    s = s + bias_ref[2 * h + (1 if kind == "near" else 0)]
                elif online:
                    s = s + far_bias[h]
                else:
                    shift = shift - far_bias[h]
                s = jnp.where(keep, s, NEG)
                _softmax_accumulate(s, v_ref[h, :, pl.ds(k0, tile)], acc_ref, h, shift=shift,
                                    m_ref=m_ref if online else None, keep=keep)
            return ties_seen + jnp.sum(eqf, axis=0, keepdims=True)

        seen = _unrolled_loop(jnp.maximum(i - 1, 0), 4, lambda j, r: attend(j, r, "far"), jnp.zeros((1, tile), F32))
        seen = lax.cond(i >= 1, lambda r: attend(i - 1, r, "near"), lambda r: r, seen)
        attend(i, seen, "diag")

    _two_pass_attention(run, acc_ref, m_ref, N_HEADS)
    for h in range(N_HEADS):
        o_ref[h * HEAD_DIM:(h + 1) * HEAD_DIM, :] = _normalized(acc_ref, h).astype(o_ref.dtype)


def _dsa_attention(qi_t, w_t, k_idx, q_t, k, v_aug, t5_flat, bias_max, k_norm_max):
    bsz, _, width, seq = q_t.shape
    tile = min(SPARSE_TILE, seq)
    topk = min(DSA_TOPK_MAX, seq // 4)
    assert seq % tile == 0 and tile >= topk and tile >= T5_FAR_DIST
    kern = functools.partial(_dsa_kernel, tile=tile, head0=2 * N_HEADS, topk=topk)
    return pl.pallas_call(
        kern,
        grid=(bsz, seq // tile),
        in_specs=[_smem(), _smem(), _smem(),
                  pl.BlockSpec((None, qi_t.shape[1], tile), lambda b, i: (b, 0, i)),
                  pl.BlockSpec((None, 8, tile), lambda b, i: (b, 0, i)),
                  _resident((None, seq, k_idx.shape[2]), lambda b, i: (b, 0, 0)),
                  pl.BlockSpec((None, N_HEADS, width, tile), lambda b, i: (b, 0, 0, i)),
                  _resident((None, seq, width), lambda b, i: (b, 0, 0)),
                  _resident((None, N_HEADS, V_ROWS, seq), lambda b, i: (b, 0, 0, 0))],
        out_specs=pl.BlockSpec((None, width, tile), lambda b, i: (b, 0, i)),
        out_shape=jax.ShapeDtypeStruct((bsz, width, seq), BF16),
        scratch_shapes=[pltpu.VMEM((seq, tile), F32), pltpu.VMEM((seq, tile), BF16),
                        pltpu.VMEM((2 * N_HEADS, tile, tile), F32),
                        pltpu.VMEM((tile, tile), BF16),
                        pltpu.VMEM((N_HEADS, 1, tile), F32),
                        pltpu.VMEM((N_HEADS, V_ROWS, tile), F32)],
        compiler_params=_cparams(2),
        name="dsa_attention",
    )(t5_flat, bias_max, k_norm_max, qi_t, w_t, k_idx, q_t, k, v_aug)


def _outproj_kernel(x_ref, oa_ref, ob_ref, oc_ref, od_ref, w_ref, g_ref, y_ref):
    o_t = jnp.concatenate([r[...].reshape(GROUP_W, r.shape[-1]) for r in (oa_ref, ob_ref, oc_ref, od_ref)], axis=0)
    y = lax.dot_general(o_t, w_ref[...], (((0,), (0,)), ((), ())), preferred_element_type=F32)
    y_ref[...] = x_ref[...] + g_ref[...] * y


def _output_projection(x, o_groups, w_out, gate):
    bsz, seq, d = x.shape
    tm = ROW_TILE
    row = pl.BlockSpec((None, tm, d), lambda b, i: (b, i, 0))

    def grp(o):
        if o.ndim == 4:
            return pl.BlockSpec((None,) + o.shape[1:3] + (tm,), lambda b, i: (b, 0, 0, i))
        return pl.BlockSpec((None, GROUP_W, tm), lambda b, i: (b, 0, i))

    return pl.pallas_call(
        _outproj_kernel,
        grid=(bsz, seq // tm),
        in_specs=[row] + [grp(o) for o in o_groups] + [
                  _resident(w_out.shape, lambda b, i: (0, 0)),
                  pl.BlockSpec((None, 1, d), lambda b, i: (b, 0, 0))],
        out_specs=row,
        out_shape=jax.ShapeDtypeStruct(x.shape, F32),
        compiler_params=_cparams(2),
        name="output_projection",
    )(x, *o_groups, w_out.astype(BF16), gate)


HALO = 8


def _ffn_kernel(x_ref, xp_ref, gain_ref, sc_ref, sh_ref, g_ref, wup_ref, cw_ref, cb_ref, wd_ref, fg_ref,
                y_ref, h_ref, acc_ref, *, tm, chunk, n_chunks, final_norm):
    i = pl.program_id(1)

    def modulated(x):
        return ((_rms(x) * gain_ref[...]) * (1.0 + sc_ref[...]) + sh_ref[...]).astype(BF16)

    h_ref[HALO:, :] = modulated(x_ref[...])
    halo = modulated(xp_ref[...])
    h_ref[:HALO, :] = jnp.where(i > 0, halo, jnp.zeros_like(halo))
    acc_ref[...] = jnp.zeros(acc_ref.shape, F32)
    h = h_ref[...]

    def up(c0, half):
        col = pl.multiple_of(half * D_FF + c0, 128)
        return jnp.dot(h, wup_ref[:, pl.ds(col, chunk)], preferred_element_type=F32)

    def conv(u, c0, half):
        cw = cw_ref[half, :, pl.ds(c0, chunk)]
        return (cw[0:1] * u[HALO - 2:HALO - 2 + tm] + cw[1:2] * u[HALO - 1:HALO - 1 + tm]
                + cw[2:3] * u[HALO:HALO + tm]) + cb_ref[half, :, pl.ds(c0, chunk)]

    def chunks(first, count):
        c0s = [pl.multiple_of((first + n) * chunk, chunk) for n in range(count)]
        ups = [(up(c0, 0), up(c0, 1)) for c0 in c0s]
        acts = []
        for c0, (ug, uv) in zip(c0s, ups):
            gate = conv(ug, c0, 0)
            acts.append(((gate * (1.0 / (1.0 + jnp.exp(-gate)))) * conv(uv, c0, 1)).astype(BF16))
        a = acts[0] if count == 1 else jnp.concatenate(acts, axis=1)
        acc_ref[...] += jnp.dot(a, wd_ref[pl.ds(c0s[0], count * chunk), :], preferred_element_type=F32)

    def pair(p, carry):
        chunks(2 * p, 2)
        return carry

    lax.fori_loop(0, n_chunks // 2, pair, 0)
    if n_chunks % 2:
        chunks(n_chunks - 1, 1)

    y = x_ref[...] + g_ref[...] * acc_ref[...]
    if final_norm:
        y = _rms(y) * fg_ref[...]
    y_ref[...] = y


def _ffn(x, gain, sc, sh, gate, w_up, conv_w, conv_b, w_down, final_gain, final_norm):
    bsz, seq, d = x.shape
    tm = min(FFN_ROW_TILE, seq)
    fc = FFN_CHUNK
    n_chunks = D_FF // fc
    w_up_b = w_up.astype(BF16)
    cw = conv_w.reshape(3, 2, D_FF).transpose(1, 0, 2)
    cb = conv_b.reshape(2, 1, D_FF)
    vec = pl.BlockSpec((None, 1, d), lambda b, i: (b, 0, 0))
    one = pl.BlockSpec((1, d), lambda b, i: (0, 0))
    row = pl.BlockSpec((None, tm, d), lambda b, i: (b, i, 0))
    whole = lambda a: _resident(a.shape, lambda b, i: (0,) * a.ndim)
    w_down_b = w_down.astype(BF16)
    return pl.pallas_call(
        functools.partial(_ffn_kernel, tm=tm, chunk=fc, n_chunks=n_chunks, final_norm=final_norm),
        grid=(bsz, seq // tm),
        in_specs=[row,
                  pl.BlockSpec((None, HALO, d), lambda b, i: (b, jnp.maximum(i * (tm // HALO) - 1, 0), 0)),
                  one, vec, vec, vec, whole(w_up_b), whole(cw), whole(cb), whole(w_down_b), one],
        out_specs=row,
        out_shape=jax.ShapeDtypeStruct(x.shape, F32),
        scratch_shapes=[pltpu.VMEM((HALO + tm, d), BF16), pltpu.VMEM((tm, d), F32)],
        compiler_params=_cparams(2),
        name="conv_glu_ffn",
    )(x, x, gain.reshape(1, d), sc, sh, gate, w_up_b, cw, cb, w_down_b, final_gain.reshape(1, d))


def kernel(x, c, w_ada, b_ada, norm1_gain, w_in, f_bias, diff_lambda, diff_subln_gain, w_dq_up, w_didx_q,
           w_dkv_up, w_out, t5_table, norm2_gain, w_ffn_up, ffn_conv_w, ffn_conv_b, w_ffn_down, final_gain):
    bsz, seq, d = x.shape
    depth = w_ada.shape[0]
    t5_flat = t5_table.reshape(-1)
    bias_max = jnp.max(t5_table, axis=0) * LOG2E
    mod = _modulation(c, w_ada, b_ada)
    w = GROUP_W

    for l in range(depth):
        sh1, sc1, g1, sh2, sc2, g2 = [m[:, None, :] for m in jnp.split(mod[l], 6, axis=-1)]
        (qa, va, qb, vb, qc, vc, gc, qd, qi_t, vd, misc, ka, kb, kc, k_idx, kd), k_norm_max = _input_projection(
            x, norm1_gain[l], sc1, sh1, w_in[l], w_dq_up[l], w_didx_q[l], w_dkv_up[l])
        kmax = [k_norm_max[:, g].reshape(-1) for g in range(4)]

        lambda_init = 0.8 - 0.6 * math.exp(-0.3 * l)
        lq1, lk1, lq2, lk2 = diff_lambda[l]
        lam = jnp.exp(jnp.sum(lq1 * lk1)) - jnp.exp(jnp.sum(lq2 * lk2)) + lambda_init
        o_a = _diff_attention(qa, ka, va, lam, diff_subln_gain[l], t5_flat, bias_max, kmax[0], 1.0 - lambda_init)

        o_b = _moba_attention(qb, kb, vb, _moba_kmean(kb).astype(BF16), t5_flat, bias_max, kmax[1])

        hi, mid, lo = _forget_cumsum(misc[:, 8:8 + N_HEADS] + f_bias[l][None, :, None])
        ones = jnp.ones_like(hi)
        cparts = jnp.stack([hi, mid, lo], axis=2)
        onep = jnp.stack([ones] * 3, axis=2)
        padw = 2 * HEAD_DIM - HEAD_DIM - 6
        qc_aug = jnp.concatenate([qc, onep.astype(BF16), cparts.astype(BF16),
                                  jnp.zeros((bsz, N_HEADS, padw, seq), BF16)], axis=2)
        kc_aug = jnp.concatenate([kc, (-cparts).transpose(0, 1, 3, 2).astype(BF16),
                                  onep.transpose(0, 1, 3, 2).astype(BF16),
                                  jnp.zeros((bsz, N_HEADS, seq, padw), BF16)], axis=3)
        o_c = _forgetting_attention(qc_aug, kc_aug, vc, gc, kmax[2])

        o_d = _dsa_attention(qi_t, misc[:, :8], k_idx, qd, kd, vd, t5_flat, bias_max, kmax[3])

        x = _output_projection(x, [o_a, o_b, o_c, o_d], w_out[l], g1)
        x = _ffn(x, norm2_gain[l], sc2, sh2, g2, w_ffn_up[l], ffn_conv_w[l], ffn_conv_b[l], w_ffn_down[l],
                 final_gain, final_norm=(l == depth - 1))
    return x
```

```python
import functools
import math

import numpy as np
import jax
import jax.numpy as jnp
from jax import lax
from jax.experimental import pallas as pl
from jax.experimental.pallas import tpu as pltpu

F32 = jnp.float32
BF16 = jnp.bfloat16
I32 = jnp.int32

HEAD_DIM = 64
N_HEADS = 4
GROUP_W = N_HEADS * HEAD_DIM
DIFF_HALF = HEAD_DIM // 2
MOBA_BLOCK = 256
MOBA_TOPK = 3
DSA_RANK = 128
IDX_HEADS = 4
IDX_DIM = 64
DSA_TOPK_MAX = 256
T5_BUCKETS = 32
T5_MAX_DIST = 128
N_BIAS_HEADS = 12
D_FF = 2816
EPS = 1e-6
NEG = -1e30
LOG2E = math.log2(math.e)
V_ROWS = HEAD_DIM + 16
BOUND_SLACK = 1.0 + 2.0 ** -6
MIN_DENOMINATOR = 2.0 ** -60
BF16_ROWS = 16
TAKE_ALL_TIES = 1e9

V7X_VMEM_LIMIT_BYTES = 56 * 1024 * 1024

ATTN_TILE = 512
SPARSE_TILE = 256
ROW_TILE = 512
FFN_ROW_TILE = 1024
FFN_CHUNK = 256
FORGET_HEADS_PER_STEP = 4
DIFF_HEADS_PER_STEP = 2
FORGET_BLOCKS_PER_STEP = 2


def _t5_thresholds():
    d = np.arange(0, 4 * T5_MAX_DIST)
    max_exact = T5_BUCKETS // 2
    ratio = np.maximum(d, 1).astype(np.float32) / max_exact
    large = max_exact + (np.log(ratio) / math.log(T5_MAX_DIST / max_exact) * (T5_BUCKETS - max_exact)).astype(np.int32)
    bucket = np.where(d < max_exact, d, np.minimum(large, T5_BUCKETS - 1))
    return [int(np.argmax(bucket >= b)) for b in range(T5_BUCKETS)]


T5_THRESH = _t5_thresholds()
T5_FAR_DIST = T5_THRESH[-1]


def _cparams(n_axes):
    return pltpu.CompilerParams(dimension_semantics=("arbitrary",) * n_axes,
                                vmem_limit_bytes=V7X_VMEM_LIMIT_BYTES)


def _resident(block_shape, index_map):
    return pl.BlockSpec(block_shape, index_map, pipeline_mode=pl.Buffered(1))


def _smem():
    return pl.BlockSpec(memory_space=pltpu.SMEM)


def _fill_t5_tile(bias_ref, slot, t5_ref, head, delta, tile):
    rows = 64

    def body(r, carry):
        r0 = pl.multiple_of(r * rows, rows)
        kr = lax.broadcasted_iota(I32, (rows, tile), 0) + r0
        qc = lax.broadcasted_iota(I32, (rows, tile), 1)
        d = delta + qc - kr
        val = jnp.full((rows, tile), LOG2E * t5_ref[head], F32)
        for b in range(1, T5_BUCKETS):
            val = jnp.where(d >= T5_THRESH[b], LOG2E * t5_ref[b * N_BIAS_HEADS + head], val)
        bias_ref[slot, pl.ds(r0, rows), :] = val
        return carry

    lax.fori_loop(0, tile // rows, body, 0)


def _unrolled_loop(n, unroll, body, carry):
    def group(g, c):
        for u in range(unroll):
            c = body(g * unroll + u, c)
        return c

    n_groups = n // unroll
    carry = lax.fori_loop(0, n_groups, group, carry)
    return lax.fori_loop(n_groups * unroll, n, body, carry)


def _causal_mask(tile):
    kr = lax.broadcasted_iota(I32, (tile, tile), 0)
    qc = lax.broadcasted_iota(I32, (tile, tile), 1)
    return kr <= qc


def _softmax_accumulate(s, v_aug, acc_ref, idx, *, shift=None, m_ref=None, keep=None, cols=slice(None)):
    if m_ref is None:
        p = jnp.exp2(s - shift)
        acc_ref[idx, :, cols] += jnp.dot(v_aug, p.astype(BF16), preferred_element_type=F32)
        return
    m_old = m_ref[idx, :, cols]
    m_new = jnp.maximum(m_old, jnp.max(s, axis=0, keepdims=True))
    p = jnp.exp2(s - m_new)
    if keep is not None:
        p = jnp.where(keep, p, 0.0)
    acc_ref[idx, :, cols] = (jnp.exp2(m_old - m_new) * acc_ref[idx, :, cols]
                             + jnp.dot(v_aug, p.astype(BF16), preferred_element_type=F32))
    m_ref[idx, :, cols] = m_new


def _logit_bound(q, k_norm_max, extra):
    qf = q.astype(F32)
    q_norm = jnp.sqrt(jnp.sum(qf * qf, axis=0, keepdims=True))
    return q_norm * (k_norm_max * BOUND_SLACK) + (extra + 1.0)


def _denominators_ok(acc_ref, n_chains):
    low = acc_ref[0, HEAD_DIM:HEAD_DIM + 1, :]
    for idx in range(1, n_chains):
        low = jnp.minimum(low, acc_ref[idx, HEAD_DIM:HEAD_DIM + 1, :])
    return jnp.min(low) > MIN_DENOMINATOR


def _two_pass_attention(run, acc_ref, m_ref, n_chains):
    acc_ref[...] = jnp.zeros(acc_ref.shape, F32)
    run(False)

    @pl.when(jnp.logical_not(_denominators_ok(acc_ref, n_chains)))
    def _():
        acc_ref[...] = jnp.zeros(acc_ref.shape, F32)
        m_ref[...] = jnp.full(m_ref.shape, NEG, F32)
        run(True)


def _normalized(acc_ref, idx):
    return acc_ref[idx, :HEAD_DIM, :] / acc_ref[idx, HEAD_DIM:HEAD_DIM + 1, :]


def _mod_kernel(c_ref, w_ref, b_ref, o_ref):
    c = c_ref[...]
    cond = c * (1.0 / (1.0 + jnp.exp(-c)))
    o_ref[...] = jnp.dot(cond.astype(BF16), w_ref[...].astype(BF16), preferred_element_type=F32) + b_ref[...]


def _modulation(c, w_ada, b_ada):
    depth, d, n = w_ada.shape
    bsz = c.shape[0]
    rows = 8
    c_pad = jnp.zeros((rows, d), F32).at[:bsz].set(c)
    tn = 1024
    out = pl.pallas_call(
        _mod_kernel,
        grid=(depth, n // tn),
        in_specs=[pl.BlockSpec((rows, d), lambda l, j: (0, 0)),
                  pl.BlockSpec((None, d, tn), lambda l, j: (l, 0, j)),
                  pl.BlockSpec((None, 1, tn), lambda l, j: (l, 0, j))],
        out_specs=pl.BlockSpec((None, rows, tn), lambda l, j: (l, 0, j)),
        out_shape=jax.ShapeDtypeStruct((depth, rows, n), F32),
        compiler_params=_cparams(2),
        name="adaln_modulation",
    )(c_pad, w_ada, b_ada.reshape(depth, 1, n))
    return out[:, :bsz]


def _rms(x):
    return x * lax.rsqrt(jnp.mean(x * x, axis=-1, keepdims=True) + EPS)


V_GROUP = N_HEADS * V_ROWS
T_QA, T_VA = 0, 2 * GROUP_W
T_QB = T_VA + V_GROUP
T_VB = T_QB + GROUP_W
T_QC = T_VB + V_GROUP
T_VC = T_QC + GROUP_W
T_GC = T_VC + V_GROUP
T_QLAT = T_GC + GROUP_W
T_KVLAT = T_QLAT + DSA_RANK
T_MISC = T_KVLAT + DSA_RANK
T_ROWS = T_MISC + 16
R_KA, R_KB, R_KC, R_KIDX, R_KVLAT, R_COLS = 0, GROUP_W, 2 * GROUP_W, 3 * GROUP_W, 3 * GROUP_W + 128, 4 * GROUP_W


def _inproj_kernel(x_ref, gain_ref, sc_ref, sh_ref, wt_ref, wr_ref, wqd_ref, wqi_ref, wvd_ref, wkd_ref, ones_ref,
                   seg_ref, qa_ref, va_ref, qb_ref, vb_ref, qc_ref, vc_ref, gc_ref, qd_ref, qi_ref, vd_ref, misc_ref,
                   ka_ref, kb_ref, kc_ref, ki_ref, kd_ref, kn_ref, ht_ref):
    x = x_ref[...]
    h = (_rms(x) * gain_ref[...]) * (1.0 + sc_ref[...]) + sh_ref[...]
    h_r = h.astype(BF16)
    ht_ref[...] = h.T.astype(BF16)
    h_t = ht_ref[...]
    ones_rows = ones_ref[...]

    def rms_t(z):
        return z * lax.rsqrt(jnp.mean(z * z, axis=0, keepdims=True) + EPS)

    def proj_t(r0, r1):
        return jnp.dot(wt_ref[r0:r1, :], h_t, preferred_element_type=F32)

    def put(out_ref, rows):
        out_ref[...] = rows.astype(out_ref.dtype).reshape(out_ref.shape)

    put(qa_ref, proj_t(T_QA, T_VA))
    put(va_ref, proj_t(T_VA, T_QB) + ones_rows)
    put(qb_ref, proj_t(T_QB, T_VB))
    put(vb_ref, proj_t(T_VB, T_QC) + ones_rows)
    put(qc_ref, proj_t(T_QC, T_VC))
    put(vc_ref, proj_t(T_VC, T_GC) + ones_rows)
    put(gc_ref, proj_t(T_GC, T_QLAT))
    q_lat = rms_t(proj_t(T_QLAT, T_KVLAT)).astype(BF16)
    kv_lat_t = rms_t(proj_t(T_KVLAT, T_MISC)).astype(BF16)
    put(qd_ref, jnp.dot(wqd_ref[...], q_lat, preferred_element_type=F32))
    qi_ref[...] = jnp.dot(wqi_ref[...], q_lat, preferred_element_type=F32).astype(BF16)
    put(vd_ref, jnp.dot(wvd_ref[...], kv_lat_t, preferred_element_type=F32) + ones_rows)
    misc_ref[...] = proj_t(T_MISC, T_ROWS)

    p_r = jnp.dot(h_r, wr_ref[...], preferred_element_type=F32)
    kv_lat_r = _rms(p_r[:, R_KVLAT:R_COLS]).astype(BF16)
    keys = [p_r[:, R_KA:R_KB].astype(BF16), p_r[:, R_KB:R_KC].astype(BF16), p_r[:, R_KC:R_KIDX].astype(BF16),
            jnp.dot(kv_lat_r, wkd_ref[...], preferred_element_type=F32).astype(BF16)]
    for k, out_ref in zip(keys[:3], (ka_ref, kb_ref, kc_ref)):
        for hd in range(N_HEADS):
            out_ref[hd] = k[:, hd * HEAD_DIM:(hd + 1) * HEAD_DIM]
    kd_ref[...] = keys[3]
    ki_ref[...] = p_r[:, R_KIDX:R_KIDX + IDX_DIM].astype(BF16)

    norms = []
    for k in keys:
        kf = k.astype(F32)
        sq = jnp.dot((kf * kf).astype(BF16), seg_ref[...], preferred_element_type=F32)
        norms.append(jnp.max(sq, axis=0, keepdims=True))
    kn_ref[...] = jnp.concatenate(norms + [jnp.zeros((8 - len(norms), 128), F32)], axis=0)


def _head_rows(w_cols, rows_per_head):
    d = w_cols.shape[0]
    w = w_cols.T.reshape(N_HEADS, HEAD_DIM, d)
    return jnp.concatenate([w, jnp.zeros((N_HEADS, rows_per_head - HEAD_DIM, d), w.dtype)], axis=1)


def _input_projection(x, gain, sc, sh, w_in, w_dq_up, w_didx_q, w_dkv_up):
    bsz, seq, d = x.shape
    w = GROUP_W
    o_fc = 9 * w
    o_gc = o_fc + N_HEADS
    o_ql = o_gc + w
    o_kv = o_ql + DSA_RANK
    o_ki = o_kv + DSA_RANK
    o_wi = o_ki + IDX_DIM
    q_scale_a = DIFF_HALF ** -0.5 * LOG2E
    q_scale = HEAD_DIM ** -0.5 * LOG2E
    wqa = (w_in[:, :w] * q_scale_a).T.reshape(N_HEADS, 2, DIFF_HALF, d)
    zero = jnp.zeros((N_HEADS, DIFF_HALF, d), w_in.dtype)
    wqa = jnp.stack([jnp.concatenate([wqa[:, 0], zero], axis=1), jnp.concatenate([zero, wqa[:, 1]], axis=1)], axis=1)
    values = lambda c0: _head_rows(w_in[:, c0:c0 + w], V_ROWS).reshape(V_GROUP, d)
    small = jnp.zeros((16, d), w_in.dtype)
    small = small.at[:IDX_HEADS].set((w_in[:, o_wi:o_wi + IDX_HEADS] * (IDX_HEADS ** -0.5 * IDX_DIM ** -0.5)).T)
    small = small.at[8:8 + N_HEADS].set(w_in[:, o_fc:o_gc].T)
    wt = jnp.concatenate([wqa.reshape(2 * w, d), values(2 * w),
                          (w_in[:, 3 * w:4 * w] * q_scale).T, values(5 * w),
                          (w_in[:, 6 * w:7 * w] * q_scale).T, values(8 * w),
                          w_in[:, o_gc:o_ql].T, w_in[:, o_ql:o_ki].T, small], axis=0).astype(BF16)
    assert wt.shape[0] == T_ROWS
    wr = jnp.concatenate([w_in[:, w:2 * w], w_in[:, 4 * w:5 * w], w_in[:, 7 * w:8 * w], w_in[:, o_ki:o_wi],
                          jnp.zeros((d, 128 - IDX_DIM), w_in.dtype), w_in[:, o_kv:o_ki]], axis=1).astype(BF16)
    assert wr.shape[1] == R_COLS
    wqd = (w_dq_up * q_scale).T.reshape(N_HEADS, HEAD_DIM, DSA_RANK)
    wqd = jnp.stack([jnp.zeros((N_HEADS, HEAD_DIM, DSA_RANK), wqd.dtype).at[hd].set(wqd[hd]).reshape(w, DSA_RANK)
                     for hd in range(N_HEADS)], axis=0).reshape(N_HEADS * w, DSA_RANK).astype(BF16)
    wqi = w_didx_q.T.astype(BF16)
    wvd = _head_rows(w_dkv_up[:, w:], V_ROWS).reshape(V_GROUP, DSA_RANK).astype(BF16)
    wkd = w_dkv_up[:, :w].astype(BF16)
    ones_rows = jnp.zeros((N_HEADS, V_ROWS, 1), F32).at[:, HEAD_DIM].set(1.0).reshape(V_GROUP, 1)
    seg = (jnp.arange(w)[:, None] // HEAD_DIM == jnp.arange(128)[None, :]).astype(BF16)

    tm = ROW_TILE
    nt = seq // tm
    feat = lambda rows: pl.BlockSpec((None, rows, tm), lambda b, i: (b, 0, i))
    keyh = pl.BlockSpec((None, N_HEADS, tm, HEAD_DIM), lambda b, i: (b, 0, i, 0))
    tok = lambda width: pl.BlockSpec((None, tm, width), lambda b, i: (b, i, 0))
    vec = pl.BlockSpec((None, 1, d), lambda b, i: (b, 0, 0))
    whole = lambda a: _resident(a.shape, lambda b, i: (0,) * a.ndim)
    head = lambda rows: pl.BlockSpec((None, N_HEADS, rows, tm), lambda b, i: (b, 0, 0, i))
    halves = pl.BlockSpec((None, N_HEADS, 2, HEAD_DIM, tm), lambda b, i: (b, 0, 0, 0, i))
    f_bf = lambda rows: jax.ShapeDtypeStruct((bsz, rows, seq), BF16)
    h_bf = lambda rows: jax.ShapeDtypeStruct((bsz, N_HEADS, rows, seq), BF16)
    k_bf = jax.ShapeDtypeStruct((bsz, N_HEADS, seq, HEAD_DIM), BF16)
    outs = pl.pallas_call(
        _inproj_kernel,
        grid=(bsz, nt),
        in_specs=[tok(d), pl.BlockSpec((1, d), lambda b, i: (0, 0)), vec, vec,
                  whole(wt), whole(wr), whole(wqd), whole(wqi), whole(wvd), whole(wkd), whole(ones_rows), whole(seg)],
        out_specs=[halves, head(V_ROWS), head(HEAD_DIM), head(V_ROWS), head(HEAD_DIM), head(V_ROWS), head(HEAD_DIM),
                   head(w), feat(w), head(V_ROWS), feat(16),
                   keyh, keyh, keyh, tok(IDX_DIM), tok(w),
                   pl.BlockSpec((None, None, 8, 128), lambda b, i: (b, i, 0, 0))],
        out_shape=[jax.ShapeDtypeStruct((bsz, N_HEADS, 2, HEAD_DIM, seq), BF16), h_bf(V_ROWS), h_bf(HEAD_DIM),
                   h_bf(V_ROWS), h_bf(HEAD_DIM), h_bf(V_ROWS),
                   jax.ShapeDtypeStruct((bsz, N_HEADS, HEAD_DIM, seq), F32),
                   h_bf(w), f_bf(w), h_bf(V_ROWS), jax.ShapeDtypeStruct((bsz, 16, seq), F32),
                   k_bf, k_bf, k_bf, jax.ShapeDtypeStruct((bsz, seq, IDX_DIM), BF16),
                   jax.ShapeDtypeStruct((bsz, seq, w), BF16),
                   jax.ShapeDtypeStruct((bsz, nt, 8, 128), F32)],
        scratch_shapes=[pltpu.VMEM((d, tm), BF16)],
        compiler_params=_cparams(2),
        name="input_projection",
    )(x, gain.reshape(1, d), sc, sh, wt, wr, wqd, wqi, wvd, wkd, ones_rows, seg)
    k_norm_max = jnp.sqrt(jnp.max(outs[-1], axis=1))[:, :4, :N_HEADS]
    return outs[:-1], k_norm_max


def _diff_kernel(t5_ref, bmax_ref, kmax_ref, lam_ref, q_ref, k_ref, v_ref, g_ref, o_ref, bias_ref, m_ref, acc_ref,
                 *, tile, head0, heads, group_heads, out_scale):
    b = pl.program_id(0)
    hg = pl.program_id(1)
    i = pl.program_id(2)

    @pl.when(i == 0)
    def _():
        for h in range(heads):
            _fill_t5_tile(bias_ref, 2 * h, t5_ref, head0 + hg * heads + h, 0, tile)
            _fill_t5_tile(bias_ref, 2 * h + 1, t5_ref, head0 + hg * heads + h, tile, tile)

    far_bias = [LOG2E * t5_ref[(T5_BUCKETS - 1) * N_BIAS_HEADS + head0 + hg * heads + h] for h in range(heads)]
    bounds = [_logit_bound(q_ref[h, c], kmax_ref[b * group_heads + hg * heads + h], bmax_ref[head0 + hg * heads + h])
              for h in range(heads) for c in range(2)]

    def run(online):
        def step(j, kind):
            k0 = pl.multiple_of(j * tile, tile)
            logits = [jnp.dot(k_ref[h, pl.ds(k0, tile), :], q_ref[h, c], preferred_element_type=F32)
                      for h in range(heads) for c in range(2)]
            for h in range(heads):
                vt = v_ref[h, :, pl.ds(k0, tile)]
                for c in range(2):
                    idx = 2 * h + c
                    s, shift = logits[idx], bounds[idx]
                    if kind != "far":
                        s = s + bias_ref[2 * h + (1 if kind == "near" else 0)]
                    elif online:
                        s = s + far_bias[h]
                    else:
                        shift = shift - far_bias[h]
                    if kind == "diag":
                        s = jnp.where(_causal_mask(tile), s, NEG)
                    _softmax_accumulate(s, vt, acc_ref, idx, shift=shift, m_ref=m_ref if online else None)

        step(i, "diag")

        @pl.when(i >= 1)
        def _():
            step(i - 1, "near")

        def far(j, carry):
            step(j, "far")
            return carry

        _unrolled_loop(jnp.maximum(i - 1, 0), 2, far, 0)

    _two_pass_attention(run, acc_ref, m_ref, 2 * heads)

    for h in range(heads):
        o = _normalized(acc_ref, 2 * h) - lam_ref[0] * _normalized(acc_ref, 2 * h + 1)
        y = o * lax.rsqrt(jnp.mean(o * o, axis=0, keepdims=True) + EPS)
        o_ref[h] = ((y * g_ref[...]) * out_scale).astype(o_ref.dtype)


def _diff_attention(q_t, k, v_aug, lam, subln_gain, t5_flat, bias_max, k_norm_max, out_scale):
    bsz, nh, _, dh, seq = q_t.shape
    tile = min(ATTN_TILE, seq)
    hps = DIFF_HEADS_PER_STEP
    assert tile >= T5_FAR_DIST and seq % tile == 0 and nh % hps == 0
    kern = functools.partial(_diff_kernel, tile=tile, head0=0, heads=hps, group_heads=nh, out_scale=out_scale)
    return pl.pallas_call(
        kern,
        grid=(bsz, nh // hps, seq // tile),
        in_specs=[_smem(), _smem(), _smem(), _smem(),
                  pl.BlockSpec((None, hps, 2, dh, tile), lambda b, h, i: (b, h, 0, 0, i)),
                  _resident((None, hps, seq, dh), lambda b, h, i: (b, h, 0, 0)),
                  _resident((None, hps, V_ROWS, seq), lambda b, h, i: (b, h, 0, 0)),
                  pl.BlockSpec((dh, 1), lambda b, h, i: (0, 0))],
        out_specs=pl.BlockSpec((None, hps, dh, tile), lambda b, h, i: (b, h, 0, i)),
        out_shape=jax.ShapeDtypeStruct((bsz, nh, dh, seq), BF16),
        scratch_shapes=[pltpu.VMEM((2 * hps, tile, tile), F32),
                        pltpu.VMEM((2 * hps, 1, tile), F32),
                        pltpu.VMEM((2 * hps, V_ROWS, tile), F32)],
        compiler_params=_cparams(3),
        name="diff_attention",
    )(t5_flat, bias_max, k_norm_max, lam.reshape(1), q_t, k, v_aug, subln_gain.reshape(dh, 1))


def _kmean_kernel(k_ref, o_ref, *, blocks):
    k = k_ref[...].astype(F32)
    o_ref[...] = jnp.mean(k.reshape(blocks, MOBA_BLOCK, k.shape[-1]), axis=1)


def _moba_kmean(k):
    bsz, nh, seq, dh = k.shape
    nb = seq // MOBA_BLOCK
    blocks = min(8, nb)
    return pl.pallas_call(
        functools.partial(_kmean_kernel, blocks=blocks),
        grid=(bsz, nh, nb // blocks),
        in_specs=[pl.BlockSpec((None, None, blocks * MOBA_BLOCK, dh), lambda b, h, i: (b, h, i, 0))],
        out_specs=pl.BlockSpec((None, None, blocks, dh), lambda b, h, i: (b, h, i, 0)),
        out_shape=jax.ShapeDtypeStruct((bsz, nh, nb, dh), F32),
        compiler_params=_cparams(3),
    )(k)


def _moba_kernel(t5_ref, bmax_ref, kmax_ref, q_ref, k_ref, v_ref, km_ref, o_ref, bias_ref, sel_ref, m_ref, acc_ref,
                 *, block, head0, heads, n_blocks):
    b = pl.program_id(0)
    t = pl.program_id(1)
    qt = 2 * block
    halves = (slice(0, block), slice(block, qt))

    @pl.when(t == 0)
    def _():
        for h in range(heads):
            _fill_t5_tile(bias_ref, 2 * h, t5_ref, head0 + h, 0, block)
            _fill_t5_tile(bias_ref, 2 * h + 1, t5_ref, head0 + h, block, block)

    far_bias = [LOG2E * t5_ref[(T5_BUCKETS - 1) * N_BIAS_HEADS + head0 + h] for h in range(heads)]
    bounds = [_logit_bound(q_ref[h], kmax_ref[b * heads + h], bmax_ref[head0 + h]) for h in range(heads)]

    nidx = lax.broadcasted_iota(I32, (n_blocks, qt), 0)
    own = 2 * t + jnp.where(lax.broadcasted_iota(I32, (n_blocks, qt), 1) >= block, 1, 0)
    for h in range(heads):
        gate = jnp.dot(km_ref[h], q_ref[h], preferred_element_type=F32)
        g = jnp.where(nidx < own, gate, NEG)
        sel = jnp.zeros(gate.shape, F32)
        for _ in range(min(MOBA_TOPK, n_blocks)):
            mx = jnp.max(g, axis=0, keepdims=True)
            first = jnp.min(jnp.where(g == mx, nidx, n_blocks), axis=0, keepdims=True)
            pick = nidx == first
            sel = jnp.where(pick, 1.0, sel)
            g = jnp.where(pick, -jnp.inf, g)
        sel_ref[h] = jnp.where(nidx < own, sel, 0.0)

    def run(online):
        def step(n, kinds):
            k0 = pl.multiple_of(n * block, block)
            active = [hf for hf in range(2) if kinds[hf] != "skip"]
            cols = slice(halves[active[0]].start, halves[active[-1]].stop)
            logits = [jnp.dot(k_ref[h, pl.ds(k0, block), :], q_ref[h, :, cols], preferred_element_type=F32)
                      for h in range(heads)]
            merged = len(active) == 2 and kinds[0] == kinds[1] == "far"
            segments = [(cols, "far")] if merged else [(halves[hf], kinds[hf]) for hf in active]
            for h in range(heads):
                vt = v_ref[h, :, pl.ds(k0, block)]
                for seg, kind in segments:
                    s = logits[h][:, seg.start - cols.start:seg.stop - cols.start]
                    shift = bounds[h][:, seg]
                    if kind != "far":
                        s = s + bias_ref[2 * h + (1 if kind == "near" else 0)]
                    elif online:
                        s = s + far_bias[h]
                    else:
                        shift = shift - far_bias[h]
                    mask = _causal_mask(block) if kind == "diag" else sel_ref[h, pl.ds(n, 1), seg] > 0.5
                    s = jnp.where(mask, s, NEG)
                    _softmax_accumulate(s, vt, acc_ref, h, shift=shift, m_ref=m_ref if online else None,
                                        keep=mask, cols=seg)

        step(2 * t + 1, ("skip", "diag"))
        step(2 * t, ("diag", "near"))

        @pl.when(t >= 1)
        def _():
            step(2 * t - 1, ("near", "far"))

        def far(n, carry):
            step(n, ("far", "far"))
            return carry

        _unrolled_loop(jnp.maximum(2 * t - 1, 0), 2, far, 0)

    _two_pass_attention(run, acc_ref, m_ref, heads)
    for h in range(heads):
        o_ref[h] = _normalized(acc_ref, h).astype(o_ref.dtype)


def _moba_attention(q_t, k, v_aug, kmean, t5_flat, bias_max, k_norm_max):
    bsz, nh, dh, seq = q_t.shape
    block = MOBA_BLOCK
    qt = 2 * block
    assert seq % qt == 0 and block >= T5_FAR_DIST
    nb = seq // block
    kern = functools.partial(_moba_kernel, block=block, head0=N_HEADS, heads=nh, n_blocks=nb)
    return pl.pallas_call(
        kern,
        grid=(bsz, seq // qt),
        in_specs=[_smem(), _smem(), _smem(),
                  pl.BlockSpec((None, nh, dh, qt), lambda b, i: (b, 0, 0, i)),
                  _resident((None, nh, seq, dh), lambda b, i: (b, 0, 0, 0)),
                  _resident((None, nh, V_ROWS, seq), lambda b, i: (b, 0, 0, 0)),
                  _resident((None, nh, nb, dh), lambda b, i: (b, 0, 0, 0))],
        out_specs=pl.BlockSpec((None, nh, dh, qt), lambda b, i: (b, 0, 0, i)),
        out_shape=jax.ShapeDtypeStruct((bsz, nh, dh, seq), BF16),
        scratch_shapes=[pltpu.VMEM((2 * nh, block, block), F32), pltpu.VMEM((nh, nb, qt), F32),
                        pltpu.VMEM((nh, 1, qt), F32),
                        pltpu.VMEM((nh, V_ROWS, qt), F32)],
        compiler_params=_cparams(2),
        name="moba_attention",
    )(t5_flat, bias_max, k_norm_max, q_t, k, v_aug, kmean)


def _split3(x):
    def trunc(v):
        bits = lax.bitcast_convert_type(v, I32)
        return lax.bitcast_convert_type(bits & jnp.int32(-65536), F32)
    hi = trunc(x)
    r1 = x - hi
    mid = trunc(r1)
    lo = trunc(r1 - mid)
    return hi, mid, lo


def _forget_cumsum_kernel(f_ref, hi_ref, mid_ref, lo_ref, *, heads, rows):
    upper = (lax.broadcasted_iota(I32, (128, 128), 0) <= lax.broadcasted_iota(I32, (128, 128), 1)).astype(F32)
    strict = (lax.broadcasted_iota(I32, (rows, rows), 1) < lax.broadcasted_iota(I32, (rows, rows), 0)).astype(F32)
    for h in range(heads):
        x = jnp.concatenate([f_ref[h:h + 1, r * 128:(r + 1) * 128] for r in range(rows)], axis=0)
        ls = jnp.minimum(x, 0.0) - jnp.log(1.0 + jnp.exp(-jnp.abs(x)))
        within = jnp.dot(ls, upper, preferred_element_type=F32, precision=lax.Precision.HIGHEST)
        before = jnp.dot(strict, within, preferred_element_type=F32, precision=lax.Precision.HIGHEST)
        cf = (within + before[:, 127:128]) * LOG2E
        for part, out_ref in zip(_split3(cf), (hi_ref, mid_ref, lo_ref)):
            for r in range(rows):
                out_ref[h:h + 1, r * 128:(r + 1) * 128] = part[r:r + 1, :]


def _forget_cumsum(f_logit):
    bsz, nh, seq = f_logit.shape
    spec = pl.BlockSpec((None, nh, seq), lambda b: (b, 0, 0))
    return pl.pallas_call(
        functools.partial(_forget_cumsum_kernel, heads=nh, rows=seq // 128),
        grid=(bsz,),
        in_specs=[spec],
        out_specs=[spec] * 3,
        out_shape=[jax.ShapeDtypeStruct((bsz, nh, seq), F32)] * 3,
        compiler_params=_cparams(1),
        name="forget_cumsum",
    )(f_logit)


def _forget_kernel(kmax_ref, q_ref, c_ref, k_ref, v_ref, gate_ref, o_ref, m_ref, acc_ref,
                   *, tile, heads, group_heads):
    b = pl.program_id(0)
    hg = pl.program_id(1)
    i = pl.program_id(2)
    bounds = [_logit_bound(q_ref[h], kmax_ref[b * group_heads + hg * heads + h], 0.0) for h in range(heads)]
    pad = jnp.zeros((k_ref.shape[-1] - HEAD_DIM - c_ref.shape[1], tile), BF16)
    q_aug = [jnp.concatenate([q_ref[h], c_ref[h], pad], axis=0) for h in range(heads)]

    def run(online):
        def step(j, blocks, mask):
            k0 = pl.multiple_of(j * tile, tile)
            logits = [jnp.dot(k_ref[h, pl.ds(k0, blocks * tile), :], q_aug[h], preferred_element_type=F32)
                      for h in range(heads)]
            for h in range(heads):
                s = logits[h] if mask is None else jnp.where(mask, logits[h], NEG)
                _softmax_accumulate(s, v_ref[h, :, pl.ds(k0, blocks * tile)], acc_ref, h, shift=bounds[h],
                                    m_ref=m_ref if online else None)

        step(i, 1, _causal_mask(tile))
        n_big = i // FORGET_BLOCKS_PER_STEP

        def past_big(j, carry):
            step(j * FORGET_BLOCKS_PER_STEP, FORGET_BLOCKS_PER_STEP, None)
            return carry

        def past_single(j, carry):
            step(j, 1, None)
            return carry

        lax.fori_loop(0, n_big, past_big, 0)
        lax.fori_loop(n_big * FORGET_BLOCKS_PER_STEP, i, past_single, 0)

    _two_pass_attention(run, acc_ref, m_ref, heads)
    for h in range(heads):
        g = gate_ref[h]
        o_ref[h] = (_normalized(acc_ref, h) * (1.0 / (1.0 + jnp.exp(-g)))).astype(o_ref.dtype)


def _forgetting_attention(q_t, c_aug_t, k_aug, v_aug, gate_t, k_norm_max):
    bsz, nh, dh, seq = q_t.shape
    kdim = k_aug.shape[-1]
    tile = min(ATTN_TILE, seq)
    hps = FORGET_HEADS_PER_STEP
    assert seq % tile == 0 and nh % hps == 0
    return pl.pallas_call(
        functools.partial(_forget_kernel, tile=tile, heads=hps, group_heads=nh),
        grid=(bsz, nh // hps, seq // tile),
        in_specs=[_smem(),
                  pl.BlockSpec((None, hps, dh, tile), lambda b, h, i: (b, h, 0, i)),
                  pl.BlockSpec((None, hps, c_aug_t.shape[2], tile), lambda b, h, i: (b, h, 0, i)),
                  _resident((None, hps, seq, kdim), lambda b, h, i: (b, h, 0, 0)),
                  _resident((None, hps, V_ROWS, seq), lambda b, h, i: (b, h, 0, 0)),
                  pl.BlockSpec((None, hps, dh, tile), lambda b, h, i: (b, h, 0, i))],
        out_specs=pl.BlockSpec((None, hps, dh, tile), lambda b, h, i: (b, h, 0, i)),
        out_shape=jax.ShapeDtypeStruct((bsz, nh, dh, seq), BF16),
        scratch_shapes=[pltpu.VMEM((hps, 1, tile), F32),
                        pltpu.VMEM((hps, V_ROWS, tile), F32)],
        compiler_params=_cparams(3),
        name="forget_attention",
    )(k_norm_max, q_t, c_aug_t, k_aug, v_aug, gate_t)


def _key_to_float(u):
    ks = u ^ jnp.int32(-2147483648)
    bits = jnp.where(ks < 0, ks ^ jnp.int32(2147483647), ks)
    return lax.bitcast_convert_type(bits, F32)


def _truncate_to_bf16(x):
    bits = lax.bitcast_convert_type(x, I32) & jnp.int32(-65536)
    return lax.bitcast_convert_type(bits, F32).astype(BF16)


def _dsa_kernel(t5_ref, bmax_ref, kmax_ref, qi_ref, w_ref, ki_ref, q_ref, k_ref, v_ref, o_ref,
                sc_ref, hi_ref, bias_ref, tri_ref, m_ref, acc_ref, *, tile, head0, topk):
    b = pl.program_id(0)
    i = pl.program_id(1)

    @pl.when((b == 0) & (i == 0))
    def _():
        for h in range(N_HEADS):
            _fill_t5_tile(bias_ref, 2 * h, t5_ref, head0 + h, 0, tile)
            _fill_t5_tile(bias_ref, 2 * h + 1, t5_ref, head0 + h, tile, tile)
        tri_ref[...] = jnp.where(lax.broadcasted_iota(I32, (tile, tile), 1) <= lax.broadcasted_iota(I32, (tile, tile), 0),
                                 1.0, 0.0).astype(BF16)

    causal = _causal_mask(tile)

    def index_scores(j):
        k0 = pl.multiple_of(j * tile, tile)
        kt = ki_ref[pl.ds(k0, tile), :]
        raws = [jnp.dot(kt, qi_ref[h * IDX_DIM:(h + 1) * IDX_DIM, :], preferred_element_type=F32)
                for h in range(IDX_HEADS)]
        sc = jnp.zeros((tile, tile), F32)
        for h in range(IDX_HEADS):
            sc = sc + jnp.maximum(raws[h], 0.0) * w_ref[h:h + 1, :]
        return k0, sc

    def store_scores(k0, sc):
        sc_ref[pl.ds(k0, tile), :] = sc
        hi_ref[pl.ds(k0, tile), :] = _truncate_to_bf16(sc)

    def score_body(j, carry):
        store_scores(*index_scores(j))
        return carry

    _unrolled_loop(i, 4, score_body, 0)
    k0, sc = index_scores(i)
    store_scores(k0, jnp.where(causal, sc, NEG))

    def count(pred):
        def body(j, cnt):
            k0 = pl.multiple_of(j * tile, tile)
            ind = jnp.where(pred(sc_ref[pl.ds(k0, tile), :]), 1.0, 0.0)
            return cnt + jnp.sum(ind.reshape(tile // 8, 8, tile), axis=0)
        cnt = _unrolled_loop(i + 1, 4, body, jnp.zeros((8, tile), F32))
        return jnp.sum(cnt, axis=0, keepdims=True)

    def count_truncated(thr):
        def body(j, cnt):
            k0 = pl.multiple_of(j * tile, tile)
            ind = jnp.where(hi_ref[pl.ds(k0, tile), :] >= thr, jnp.ones((), BF16), jnp.zeros((), BF16))
            part = ind[0:BF16_ROWS]
            for r in range(1, tile // BF16_ROWS):
                part = part + ind[r * BF16_ROWS:(r + 1) * BF16_ROWS]
            return cnt + part.astype(F32)
        cnt = _unrolled_loop(i + 1, 4, body, jnp.zeros((BF16_ROWS, tile), F32))
        return jnp.sum(cnt, axis=0, keepdims=True)

    def high_bit(it, ans):
        cand = ans | jnp.left_shift(jnp.int32(1), 31 - it)
        return jnp.where(count_truncated(_truncate_to_bf16(_key_to_float(cand))) >= topk, cand, ans)

    ans = lax.fori_loop(0, 16, high_bit, jnp.zeros((1, tile), I32))

    above = count(lambda s: s > _key_to_float(ans))
    settled = above < topk
    need = jnp.where(settled, topk - above, 0.0)

    def low_bit(state):
        it, ans, open_f, need = state
        cand = ans | jnp.left_shift(jnp.int32(1), 31 - it)
        thr = _key_to_float(cand)
        cnt = count(lambda s: s >= thr)
        unsettled = open_f > 0.5
        ans = jnp.where(unsettled & (cnt >= topk), cand, ans)
        exact = unsettled & (cnt == topk)
        return it + 1, ans, jnp.where(exact, 0.0, open_f), jnp.where(exact, TAKE_ALL_TIES, need)

    _, ans, open_f, need = lax.while_loop(lambda st: (st[0] < 32) & (jnp.max(st[2]) > 0.5), low_bit,
                                          (jnp.int32(16), ans, jnp.where(settled, 0.0, 1.0), need))
    tau = _key_to_float(ans)
    need = jnp.where(open_f > 0.5, topk - count(lambda s: s > tau), need)

    qs = [q_ref[h] for h in range(N_HEADS)]

    far_bias = [LOG2E * t5_ref[(T5_BUCKETS - 1) * N_BIAS_HEADS + head0 + h] for h in range(N_HEADS)]
    bounds = [_logit_bound(qs[h], kmax_ref[b * N_HEADS + h], bmax_ref[head0 + h]) for h in range(N_HEADS)]

    def run(online):
        def attend(j, ties_seen, kind):
            k0 = pl.multiple_of(j * tile, tile)
            sc = sc_ref[pl.ds(k0, tile), :]
            eq = sc == tau
            eqf = jnp.where(eq, 1.0, 0.0)
            rank = ties_seen + jnp.dot(tri_ref[...], eqf.astype(BF16), preferred_element_type=F32)
            wgt = jnp.where(eq, jnp.where(rank <= need, 1.0, 0.0), jnp.where(sc > tau, 1.0, 0.0))
            if kind == "diag":
                wgt = jnp.where(causal, wgt, 0.0)
            keep = wgt > 0.5
            kt = k_ref[pl.ds(k0, tile), :]
            logits = [jnp.dot(kt, qs[h], preferred_element_type=F32) for h in range(N_HEADS)]
            for h in range(N_HEADS):
                s, shift = logits[h], bounds[h]
                if kind != "far":
                    s = s + bias_ref[2 * h + (1 if kind == "near" else 0)]
                elif online:
                    s = s + far_bias[h]
                else:
                    shift = shift - far_bias[h]
                s = jnp.where(keep, s, NEG)
                _softmax_accumulate(s, v_ref[h, :, pl.ds(k0, tile)], acc_ref, h, shift=shift,
                                    m_ref=m_ref if online else None, keep=keep)
            return ties_seen + jnp.sum(eqf, axis=0, keepdims=True)

        seen = _unrolled_loop(jnp.maximum(i - 1, 0), 4, lambda j, r: attend(j, r, "far"), jnp.zeros((1, tile), F32))
        seen = lax.cond(i >= 1, lambda r: attend(i - 1, r, "near"), lambda r: r, seen)
        attend(i, seen, "diag")

    _two_pass_attention(run, acc_ref, m_ref, N_HEADS)
    for h in range(N_HEADS):
        o_ref[h * HEAD_DIM:(h + 1) * HEAD_DIM, :] = _normalized(acc_ref, h).astype(o_ref.dtype)


def _dsa_attention(qi_t, w_t, k_idx, q_t, k, v_aug, t5_flat, bias_max, k_norm_max):
    bsz, _, width, seq = q_t.shape
    tile = min(SPARSE_TILE, seq)
    topk = min(DSA_TOPK_MAX, seq // 4)
    assert seq % tile == 0 and tile >= topk and tile >= T5_FAR_DIST
    kern = functools.partial(_dsa_kernel, tile=tile, head0=2 * N_HEADS, topk=topk)
    return pl.pallas_call(
        kern,
        grid=(bsz, seq // tile),
        in_specs=[_smem(), _smem(), _smem(),
                  pl.BlockSpec((None, qi_t.shape[1], tile), lambda b, i: (b, 0, i)),
                  pl.BlockSpec((None, 8, tile), lambda b, i: (b, 0, i)),
                  _resident((None, seq, k_idx.shape[2]), lambda b, i: (b, 0, 0)),
                  pl.BlockSpec((None, N_HEADS, width, tile), lambda b, i: (b, 0, 0, i)),
                  _resident((None, seq, width), lambda b, i: (b, 0, 0)),
                  _resident((None, N_HEADS, V_ROWS, seq), lambda b, i: (b, 0, 0, 0))],
        out_specs=pl.BlockSpec((None, width, tile), lambda b, i: (b, 0, i)),
        out_shape=jax.ShapeDtypeStruct((bsz, width, seq), BF16),
        scratch_shapes=[pltpu.VMEM((seq, tile), F32), pltpu.VMEM((seq, tile), BF16),
                        pltpu.VMEM((2 * N_HEADS, tile, tile), F32),
                        pltpu.VMEM((tile, tile), BF16),
                        pltpu.VMEM((N_HEADS, 1, tile), F32),
                        pltpu.VMEM((N_HEADS, V_ROWS, tile), F32)],
        compiler_params=_cparams(2),
        name="dsa_attention",
    )(t5_flat, bias_max, k_norm_max, qi_t, w_t, k_idx, q_t, k, v_aug)


def _outproj_kernel(x_ref, oa_ref, ob_ref, oc_ref, od_ref, w_ref, g_ref, y_ref):
    o_t = jnp.concatenate([r[...].reshape(GROUP_W, r.shape[-1]) for r in (oa_ref, ob_ref, oc_ref, od_ref)], axis=0)
    y = lax.dot_general(o_t, w_ref[...], (((0,), (0,)), ((), ())), preferred_element_type=F32)
    y_ref[...] = x_ref[...] + g_ref[...] * y


def _output_projection(x, o_groups, w_out, gate):
    bsz, seq, d = x.shape
    tm = ROW_TILE
    row = pl.BlockSpec((None, tm, d), lambda b, i: (b, i, 0))

    def grp(o):
        if o.ndim == 4:
            return pl.BlockSpec((None,) + o.shape[1:3] + (tm,), lambda b, i: (b, 0, 0, i))
        return pl.BlockSpec((None, GROUP_W, tm), lambda b, i: (b, 0, i))

    return pl.pallas_call(
        _outproj_kernel,
        grid=(bsz, seq // tm),
        in_specs=[row] + [grp(o) for o in o_groups] + [
                  _resident(w_out.shape, lambda b, i: (0, 0)),
                  pl.BlockSpec((None, 1, d), lambda b, i: (b, 0, 0))],
        out_specs=row,
        out_shape=jax.ShapeDtypeStruct(x.shape, F32),
        compiler_params=_cparams(2),
        name="output_projection",
    )(x, *o_groups, w_out.astype(BF16), gate)


HALO = 8


def _ffn_kernel(x_ref, xp_ref, gain_ref, sc_ref, sh_ref, g_ref, wup_ref, cw_ref, cb_ref, wd_ref, fg_ref,
                y_ref, h_ref, acc_ref, *, tm, chunk, n_chunks, final_norm):
    i = pl.program_id(1)

    def modulated(x):
        return ((_rms(x) * gain_ref[...]) * (1.0 + sc_ref[...]) + sh_ref[...]).astype(BF16)

    h_ref[HALO:, :] = modulated(x_ref[...])
    halo = modulated(xp_ref[...])
    h_ref[:HALO, :] = jnp.where(i > 0, halo, jnp.zeros_like(halo))
    acc_ref[...] = jnp.zeros(acc_ref.shape, F32)
    h = h_ref[...]

    def up(c0, half):
        col = pl.multiple_of(half * D_FF + c0, 128)
        return jnp.dot(h, wup_ref[:, pl.ds(col, chunk)], preferred_element_type=F32)

    def conv(u, c0, half):
        cw = cw_ref[half, :, pl.ds(c0, chunk)]
        return (cw[0:1] * u[HALO - 2:HALO - 2 + tm] + cw[1:2] * u[HALO - 1:HALO - 1 + tm]
                + cw[2:3] * u[HALO:HALO + tm]) + cb_ref[half, :, pl.ds(c0, chunk)]

    def chunks(first, count):
        c0s = [pl.multiple_of((first + n) * chunk, chunk) for n in range(count)]
        ups = [(up(c0, 0), up(c0, 1)) for c0 in c0s]
        acts = []
        for c0, (ug, uv) in zip(c0s, ups):
            gate = conv(ug, c0, 0)
            acts.append(((gate * (1.0 / (1.0 + jnp.exp(-gate)))) * conv(uv, c0, 1)).astype(BF16))
        a = acts[0] if count == 1 else jnp.concatenate(acts, axis=1)
        acc_ref[...] += jnp.dot(a, wd_ref[pl.ds(c0s[0], count * chunk), :], preferred_element_type=F32)

    def pair(p, carry):
        chunks(2 * p, 2)
        return carry

    lax.fori_loop(0, n_chunks // 2, pair, 0)
    if n_chunks % 2:
        chunks(n_chunks - 1, 1)

    y = x_ref[...] + g_ref[...] * acc_ref[...]
    if final_norm:
        y = _rms(y) * fg_ref[...]
    y_ref[...] = y


def _ffn(x, gain, sc, sh, gate, w_up, conv_w, conv_b, w_down, final_gain, final_norm):
    bsz, seq, d = x.shape
    tm = min(FFN_ROW_TILE, seq)
    fc = FFN_CHUNK
    n_chunks = D_FF // fc
    w_up_b = w_up.astype(BF16)
    cw = conv_w.reshape(3, 2, D_FF).transpose(1, 0, 2)
    cb = conv_b.reshape(2, 1, D_FF)
    vec = pl.BlockSpec((None, 1, d), lambda b, i: (b, 0, 0))
    one = pl.BlockSpec((1, d), lambda b, i: (0, 0))
    row = pl.BlockSpec((None, tm, d), lambda b, i: (b, i, 0))
    whole = lambda a: _resident(a.shape, lambda b, i: (0,) * a.ndim)
    w_down_b = w_down.astype(BF16)
    return pl.pallas_call(
        functools.partial(_ffn_kernel, tm=tm, chunk=fc, n_chunks=n_chunks, final_norm=final_norm),
        grid=(bsz, seq // tm),
        in_specs=[row,
                  pl.BlockSpec((None, HALO, d), lambda b, i: (b, jnp.maximum(i * (tm // HALO) - 1, 0), 0)),
                  one, vec, vec, vec, whole(w_up_b), whole(cw), whole(cb), whole(w_down_b), one],
        out_specs=row,
        out_shape=jax.ShapeDtypeStruct(x.shape, F32),
        scratch_shapes=[pltpu.VMEM((HALO + tm, d), BF16), pltpu.VMEM((tm, d), F32)],
        compiler_params=_cparams(2),
        name="conv_glu_ffn",
    )(x, x, gain.reshape(1, d), sc, sh, gate, w_up_b, cw, cb, w_down_b, final_gain.reshape(1, d))


def kernel(x, c, w_ada, b_ada, norm1_gain, w_in, f_bias, diff_lambda, diff_subln_gain, w_dq_up, w_didx_q,
           w_dkv_up, w_out, t5_table, norm2_gain, w_ffn_up, ffn_conv_w, ffn_conv_b, w_ffn_down, final_gain):
    bsz, seq, d = x.shape
    depth = w_ada.shape[0]
    t5_flat = t5_table.reshape(-1)
    bias_max = jnp.max(t5_table, axis=0) * LOG2E
    mod = _modulation(c, w_ada, b_ada)
    w = GROUP_W

    for l in range(depth):
        sh1, sc1, g1, sh2, sc2, g2 = [m[:, None, :] for m in jnp.split(mod[l], 6, axis=-1)]
        (qa, va, qb, vb, qc, vc, gc, qd, qi_t, vd, misc, ka, kb, kc, k_idx, kd), k_norm_max = _input_projection(
            x, norm1_gain[l], sc1, sh1, w_in[l], w_dq_up[l], w_didx_q[l], w_dkv_up[l])
        kmax = [k_norm_max[:, g].reshape(-1) for g in range(4)]

        lambda_init = 0.8 - 0.6 * math.exp(-0.3 * l)
        lq1, lk1, lq2, lk2 = diff_lambda[l]
        lam = jnp.exp(jnp.sum(lq1 * lk1)) - jnp.exp(jnp.sum(lq2 * lk2)) + lambda_init
        o_a = _diff_attention(qa, ka, va, lam, diff_subln_gain[l], t5_flat, bias_max, kmax[0], 1.0 - lambda_init)

        o_b = _moba_attention(qb, kb, vb, _moba_kmean(kb).astype(BF16), t5_flat, bias_max, kmax[1])

        hi, mid, lo = _forget_cumsum(misc[:, 8:8 + N_HEADS] + f_bias[l][None, :, None])
        ones = jnp.ones_like(hi)
        cparts = jnp.stack([hi, mid, lo], axis=2)
        onep = jnp.stack([ones] * 3, axis=2)
        padw = 2 * HEAD_DIM - HEAD_DIM - 6
        c_aug = jnp.concatenate([onep, cparts, jnp.zeros((bsz, N_HEADS, BF16_ROWS - 6, seq), F32)],
                                axis=2).astype(BF16)
        kc_aug = jnp.concatenate([kc, (-cparts).transpose(0, 1, 3, 2).astype(BF16),
                                  onep.transpose(0, 1, 3, 2).astype(BF16),
                                  jnp.zeros((bsz, N_HEADS, seq, padw), BF16)], axis=3)
        o_c = _forgetting_attention(qc, c_aug, kc_aug, vc, gc, kmax[2])

        o_d = _dsa_attention(qi_t, misc[:, :8], k_idx, qd, kd, vd, t5_flat, bias_max, kmax[3])

        x = _output_projection(x, [o_a, o_b, o_c, o_d], w_out[l], g1)
        x = _ffn(x, norm2_gain[l], sc2, sh2, g2, w_ffn_up[l], ffn_conv_w[l], ffn_conv_b[l], w_ffn_down[l],
                 final_gain, final_norm=(l == depth - 1))
    return x
```

```python
import functools
import math

import numpy as np
import jax
import jax.numpy as jnp
from jax import lax
from jax.experimental import pallas as pl
from jax.experimental.pallas import tpu as pltpu

F32 = jnp.float32
BF16 = jnp.bfloat16
I32 = jnp.int32

HEAD_DIM = 64
N_HEADS = 4
GROUP_W = N_HEADS * HEAD_DIM
DIFF_HALF = HEAD_DIM // 2
MOBA_BLOCK = 256
MOBA_TOPK = 3
DSA_RANK = 128
IDX_HEADS = 4
IDX_DIM = 64
DSA_TOPK_MAX = 256
T5_BUCKETS = 32
T5_MAX_DIST = 128
N_BIAS_HEADS = 12
D_FF = 2816
EPS = 1e-6
NEG = -1e30
LOG2E = math.log2(math.e)
V_ROWS = HEAD_DIM + 16
BOUND_SLACK = 1.0 + 2.0 ** -6
MIN_DENOMINATOR = 2.0 ** -60
BF16_ROWS = 16
TAKE_ALL_TIES = 1e9

V7X_VMEM_LIMIT_BYTES = 56 * 1024 * 1024

ATTN_TILE = 512
SPARSE_TILE = 256
ROW_TILE = 512
FFN_ROW_TILE = 1024
FFN_CHUNK = 256
FORGET_HEADS_PER_STEP = 4
DIFF_HEADS_PER_STEP = 2
FORGET_BLOCKS_PER_STEP = 2


def _t5_thresholds():
    d = np.arange(0, 4 * T5_MAX_DIST)
    max_exact = T5_BUCKETS // 2
    ratio = np.maximum(d, 1).astype(np.float32) / max_exact
    large = max_exact + (np.log(ratio) / math.log(T5_MAX_DIST / max_exact) * (T5_BUCKETS - max_exact)).astype(np.int32)
    bucket = np.where(d < max_exact, d, np.minimum(large, T5_BUCKETS - 1))
    return [int(np.argmax(bucket >= b)) for b in range(T5_BUCKETS)]


T5_THRESH = _t5_thresholds()
T5_FAR_DIST = T5_THRESH[-1]


def _cparams(n_axes):
    return pltpu.CompilerParams(dimension_semantics=("arbitrary",) * n_axes,
                                vmem_limit_bytes=V7X_VMEM_LIMIT_BYTES)


def _resident(block_shape, index_map):
    return pl.BlockSpec(block_shape, index_map, pipeline_mode=pl.Buffered(1))


def _smem():
    return pl.BlockSpec(memory_space=pltpu.SMEM)


def _fill_t5_tile(bias_ref, slot, t5_ref, head, delta, tile):
    rows = 64

    def body(r, carry):
        r0 = pl.multiple_of(r * rows, rows)
        kr = lax.broadcasted_iota(I32, (rows, tile), 0) + r0
        qc = lax.broadcasted_iota(I32, (rows, tile), 1)
        d = delta + qc - kr
        val = jnp.full((rows, tile), LOG2E * t5_ref[head], F32)
        for b in range(1, T5_BUCKETS):
            val = jnp.where(d >= T5_THRESH[b], LOG2E * t5_ref[b * N_BIAS_HEADS + head], val)
        bias_ref[slot, pl.ds(r0, rows), :] = val
        return carry

    lax.fori_loop(0, tile // rows, body, 0)


def _unrolled_loop(n, unroll, body, carry):
    def group(g, c):
        for u in range(unroll):
            c = body(g * unroll + u, c)
        return c

    n_groups = n // unroll
    carry = lax.fori_loop(0, n_groups, group, carry)
    return lax.fori_loop(n_groups * unroll, n, body, carry)


def _causal_mask(tile):
    kr = lax.broadcasted_iota(I32, (tile, tile), 0)
    qc = lax.broadcasted_iota(I32, (tile, tile), 1)
    return kr <= qc


def _softmax_accumulate(s, v_aug, acc_ref, idx, *, shift=None, m_ref=None, keep=None, cols=slice(None)):
    if m_ref is None:
        p = jnp.exp2(s - shift)
        acc_ref[idx, :, cols] += jnp.dot(v_aug, p.astype(BF16), preferred_element_type=F32)
        return
    m_old = m_ref[idx, :, cols]
    m_new = jnp.maximum(m_old, jnp.max(s, axis=0, keepdims=True))
    p = jnp.exp2(s - m_new)
    if keep is not None:
        p = jnp.where(keep, p, 0.0)
    acc_ref[idx, :, cols] = (jnp.exp2(m_old - m_new) * acc_ref[idx, :, cols]
                             + jnp.dot(v_aug, p.astype(BF16), preferred_element_type=F32))
    m_ref[idx, :, cols] = m_new


def _logit_bound(q, k_norm_max, extra):
    qf = q.astype(F32)
    q_norm = jnp.sqrt(jnp.sum(qf * qf, axis=0, keepdims=True))
    return q_norm * (k_norm_max * BOUND_SLACK) + (extra + 1.0)


def _denominators_ok(acc_ref, n_chains):
    low = acc_ref[0, HEAD_DIM:HEAD_DIM + 1, :]
    for idx in range(1, n_chains):
        low = jnp.minimum(low, acc_ref[idx, HEAD_DIM:HEAD_DIM + 1, :])
    return jnp.min(low) > MIN_DENOMINATOR


def _two_pass_attention(run, acc_ref, m_ref, n_chains):
    acc_ref[...] = jnp.zeros(acc_ref.shape, F32)
    run(False)

    @pl.when(jnp.logical_not(_denominators_ok(acc_ref, n_chains)))
    def _():
        acc_ref[...] = jnp.zeros(acc_ref.shape, F32)
        m_ref[...] = jnp.full(m_ref.shape, NEG, F32)
        run(True)


def _normalized(acc_ref, idx):
    return acc_ref[idx, :HEAD_DIM, :] / acc_ref[idx, HEAD_DIM:HEAD_DIM + 1, :]


def _mod_kernel(c_ref, w_ref, b_ref, o_ref):
    c = c_ref[...]
    cond = c * (1.0 / (1.0 + jnp.exp(-c)))
    o_ref[...] = jnp.dot(cond.astype(BF16), w_ref[...].astype(BF16), preferred_element_type=F32) + b_ref[...]


def _modulation(c, w_ada, b_ada):
    depth, d, n = w_ada.shape
    bsz = c.shape[0]
    rows = 8
    c_pad = jnp.zeros((rows, d), F32).at[:bsz].set(c)
    tn = 1024
    out = pl.pallas_call(
        _mod_kernel,
        grid=(depth, n // tn),
        in_specs=[pl.BlockSpec((rows, d), lambda l, j: (0, 0)),
                  pl.BlockSpec((None, d, tn), lambda l, j: (l, 0, j)),
                  pl.BlockSpec((None, 1, tn), lambda l, j: (l, 0, j))],
        out_specs=pl.BlockSpec((None, rows, tn), lambda l, j: (l, 0, j)),
        out_shape=jax.ShapeDtypeStruct((depth, rows, n), F32),
        compiler_params=_cparams(2),
        name="adaln_modulation",
    )(c_pad, w_ada, b_ada.reshape(depth, 1, n))
    return out[:, :bsz]


def _rms(x):
    return x * lax.rsqrt(jnp.mean(x * x, axis=-1, keepdims=True) + EPS)


V_GROUP = N_HEADS * V_ROWS
T_QA, T_VA = 0, 2 * GROUP_W
T_QB = T_VA + V_GROUP
T_VB = T_QB + GROUP_W
T_QC = T_VB + V_GROUP
T_VC = T_QC + GROUP_W
T_GC = T_VC + V_GROUP
T_QLAT = T_GC + GROUP_W
T_KVLAT = T_QLAT + DSA_RANK
T_MISC = T_KVLAT + DSA_RANK
T_ROWS = T_MISC + 16
R_KA, R_KB, R_KC, R_KIDX, R_KVLAT, R_COLS = 0, GROUP_W, 2 * GROUP_W, 3 * GROUP_W, 3 * GROUP_W + 128, 4 * GROUP_W


def _inproj_kernel(x_ref, gain_ref, sc_ref, sh_ref, wt_ref, wr_ref, wqd_ref, wqi_ref, wvd_ref, wkd_ref, ones_ref,
                   seg_ref, qa_ref, va_ref, qb_ref, vb_ref, qc_ref, vc_ref, gc_ref, qd_ref, qi_ref, vd_ref, misc_ref,
                   ka_ref, kb_ref, kc_ref, ki_ref, kd_ref, kn_ref, ht_ref):
    x = x_ref[...]
    h = (_rms(x) * gain_ref[...]) * (1.0 + sc_ref[...]) + sh_ref[...]
    h_r = h.astype(BF16)
    ht_ref[...] = h.T.astype(BF16)
    h_t = ht_ref[...]
    ones_rows = ones_ref[...]

    def rms_t(z):
        return z * lax.rsqrt(jnp.mean(z * z, axis=0, keepdims=True) + EPS)

    def proj_t(r0, r1):
        return jnp.dot(wt_ref[r0:r1, :], h_t, preferred_element_type=F32)

    def put(out_ref, rows):
        out_ref[...] = rows.astype(out_ref.dtype).reshape(out_ref.shape)

    put(qa_ref, proj_t(T_QA, T_VA))
    put(va_ref, proj_t(T_VA, T_QB) + ones_rows)
    put(qb_ref, proj_t(T_QB, T_VB))
    put(vb_ref, proj_t(T_VB, T_QC) + ones_rows)
    put(qc_ref, proj_t(T_QC, T_VC))
    put(vc_ref, proj_t(T_VC, T_GC) + ones_rows)
    put(gc_ref, proj_t(T_GC, T_QLAT))
    q_lat = rms_t(proj_t(T_QLAT, T_KVLAT)).astype(BF16)
    kv_lat_t = rms_t(proj_t(T_KVLAT, T_MISC)).astype(BF16)
    put(qd_ref, jnp.dot(wqd_ref[...], q_lat, preferred_element_type=F32))
    qi_ref[...] = jnp.dot(wqi_ref[...], q_lat, preferred_element_type=F32).astype(BF16)
    put(vd_ref, jnp.dot(wvd_ref[...], kv_lat_t, preferred_element_type=F32) + ones_rows)
    misc_ref[...] = proj_t(T_MISC, T_ROWS)

    p_r = jnp.dot(h_r, wr_ref[...], preferred_element_type=F32)
    kv_lat_r = _rms(p_r[:, R_KVLAT:R_COLS]).astype(BF16)
    keys = [p_r[:, R_KA:R_KB].astype(BF16), p_r[:, R_KB:R_KC].astype(BF16), p_r[:, R_KC:R_KIDX].astype(BF16),
            jnp.dot(kv_lat_r, wkd_ref[...], preferred_element_type=F32).astype(BF16)]
    for k, out_ref in zip(keys[:3], (ka_ref, kb_ref, kc_ref)):
        for hd in range(N_HEADS):
            out_ref[hd] = k[:, hd * HEAD_DIM:(hd + 1) * HEAD_DIM]
    kd_ref[...] = keys[3]
    ki_ref[...] = p_r[:, R_KIDX:R_KIDX + IDX_DIM].astype(BF16)

    norms = []
    for k in keys:
        kf = k.astype(F32)
        sq = jnp.dot((kf * kf).astype(BF16), seg_ref[...], preferred_element_type=F32)
        norms.append(jnp.max(sq, axis=0, keepdims=True))
    kn_ref[...] = jnp.concatenate(norms + [jnp.zeros((8 - len(norms), 128), F32)], axis=0)


def _head_rows(w_cols, rows_per_head):
    d = w_cols.shape[0]
    w = w_cols.T.reshape(N_HEADS, HEAD_DIM, d)
    return jnp.concatenate([w, jnp.zeros((N_HEADS, rows_per_head - HEAD_DIM, d), w.dtype)], axis=1)


def _input_projection(x, gain, sc, sh, w_in, w_dq_up, w_didx_q, w_dkv_up):
    bsz, seq, d = x.shape
    w = GROUP_W
    o_fc = 9 * w
    o_gc = o_fc + N_HEADS
    o_ql = o_gc + w
    o_kv = o_ql + DSA_RANK
    o_ki = o_kv + DSA_RANK
    o_wi = o_ki + IDX_DIM
    q_scale_a = DIFF_HALF ** -0.5 * LOG2E
    q_scale = HEAD_DIM ** -0.5 * LOG2E
    wqa = (w_in[:, :w] * q_scale_a).T.reshape(N_HEADS, 2, DIFF_HALF, d)
    zero = jnp.zeros((N_HEADS, DIFF_HALF, d), w_in.dtype)
    wqa = jnp.stack([jnp.concatenate([wqa[:, 0], zero], axis=1), jnp.concatenate([zero, wqa[:, 1]], axis=1)], axis=1)
    values = lambda c0: _head_rows(w_in[:, c0:c0 + w], V_ROWS).reshape(V_GROUP, d)
    small = jnp.zeros((16, d), w_in.dtype)
    small = small.at[:IDX_HEADS].set((w_in[:, o_wi:o_wi + IDX_HEADS] * (IDX_HEADS ** -0.5 * IDX_DIM ** -0.5)).T)
    small = small.at[8:8 + N_HEADS].set(w_in[:, o_fc:o_gc].T)
    wt = jnp.concatenate([wqa.reshape(2 * w, d), values(2 * w),
                          (w_in[:, 3 * w:4 * w] * q_scale).T, values(5 * w),
                          (w_in[:, 6 * w:7 * w] * q_scale).T, values(8 * w),
                          w_in[:, o_gc:o_ql].T, w_in[:, o_ql:o_ki].T, small], axis=0).astype(BF16)
    assert wt.shape[0] == T_ROWS
    wr = jnp.concatenate([w_in[:, w:2 * w], w_in[:, 4 * w:5 * w], w_in[:, 7 * w:8 * w], w_in[:, o_ki:o_wi],
                          jnp.zeros((d, 128 - IDX_DIM), w_in.dtype), w_in[:, o_kv:o_ki]], axis=1).astype(BF16)
    assert wr.shape[1] == R_COLS
    wqd = (w_dq_up * q_scale).T.reshape(N_HEADS, HEAD_DIM, DSA_RANK)
    wqd = jnp.stack([jnp.zeros((N_HEADS, HEAD_DIM, DSA_RANK), wqd.dtype).at[hd].set(wqd[hd]).reshape(w, DSA_RANK)
                     for hd in range(N_HEADS)], axis=0).reshape(N_HEADS * w, DSA_RANK).astype(BF16)
    wqi = w_didx_q.T.astype(BF16)
    wvd = _head_rows(w_dkv_up[:, w:], V_ROWS).reshape(V_GROUP, DSA_RANK).astype(BF16)
    wkd = w_dkv_up[:, :w].astype(BF16)
    ones_rows = jnp.zeros((N_HEADS, V_ROWS, 1), F32).at[:, HEAD_DIM].set(1.0).reshape(V_GROUP, 1)
    seg = (jnp.arange(w)[:, None] // HEAD_DIM == jnp.arange(128)[None, :]).astype(BF16)

    tm = ROW_TILE
    nt = seq // tm
    feat = lambda rows: pl.BlockSpec((None, rows, tm), lambda b, i: (b, 0, i))
    keyh = pl.BlockSpec((None, N_HEADS, tm, HEAD_DIM), lambda b, i: (b, 0, i, 0))
    tok = lambda width: pl.BlockSpec((None, tm, width), lambda b, i: (b, i, 0))
    vec = pl.BlockSpec((None, 1, d), lambda b, i: (b, 0, 0))
    whole = lambda a: _resident(a.shape, lambda b, i: (0,) * a.ndim)
    head = lambda rows: pl.BlockSpec((None, N_HEADS, rows, tm), lambda b, i: (b, 0, 0, i))
    halves = pl.BlockSpec((None, N_HEADS, 2, HEAD_DIM, tm), lambda b, i: (b, 0, 0, 0, i))
    f_bf = lambda rows: jax.ShapeDtypeStruct((bsz, rows, seq), BF16)
    h_bf = lambda rows: jax.ShapeDtypeStruct((bsz, N_HEADS, rows, seq), BF16)
    k_bf = jax.ShapeDtypeStruct((bsz, N_HEADS, seq, HEAD_DIM), BF16)
    outs = pl.pallas_call(
        _inproj_kernel,
        grid=(bsz, nt),
        in_specs=[tok(d), pl.BlockSpec((1, d), lambda b, i: (0, 0)), vec, vec,
                  whole(wt), whole(wr), whole(wqd), whole(wqi), whole(wvd), whole(wkd), whole(ones_rows), whole(seg)],
        out_specs=[halves, head(V_ROWS), head(HEAD_DIM), head(V_ROWS), head(HEAD_DIM), head(V_ROWS), head(HEAD_DIM),
                   head(w), feat(w), head(V_ROWS), feat(16),
                   keyh, keyh, keyh, tok(IDX_DIM), tok(w),
                   pl.BlockSpec((None, None, 8, 128), lambda b, i: (b, i, 0, 0))],
        out_shape=[jax.ShapeDtypeStruct((bsz, N_HEADS, 2, HEAD_DIM, seq), BF16), h_bf(V_ROWS), h_bf(HEAD_DIM),
                   h_bf(V_ROWS), h_bf(HEAD_DIM), h_bf(V_ROWS),
                   jax.ShapeDtypeStruct((bsz, N_HEADS, HEAD_DIM, seq), F32),
                   h_bf(w), f_bf(w), h_bf(V_ROWS), jax.ShapeDtypeStruct((bsz, 16, seq), F32),
                   k_bf, k_bf, k_bf, jax.ShapeDtypeStruct((bsz, seq, IDX_DIM), BF16),
                   jax.ShapeDtypeStruct((bsz, seq, w), BF16),
                   jax.ShapeDtypeStruct((bsz, nt, 8, 128), F32)],
        scratch_shapes=[pltpu.VMEM((d, tm), BF16)],
        compiler_params=_cparams(2),
        name="input_projection",
    )(x, gain.reshape(1, d), sc, sh, wt, wr, wqd, wqi, wvd, wkd, ones_rows, seg)
    k_norm_max = jnp.sqrt(jnp.max(outs[-1], axis=1))[:, :4, :N_HEADS]
    return outs[:-1], k_norm_max


def _diff_kernel(t5_ref, bmax_ref, kmax_ref, lam_ref, q_ref, k_ref, v_ref, g_ref, o_ref, bias_ref, m_ref, acc_ref,
                 *, tile, head0, heads, group_heads, out_scale):
    b = pl.program_id(0)
    hg = pl.program_id(1)
    i = pl.program_id(2)

    @pl.when(i == 0)
    def _():
        for h in range(heads):
            _fill_t5_tile(bias_ref, 2 * h, t5_ref, head0 + hg * heads + h, 0, tile)
            _fill_t5_tile(bias_ref, 2 * h + 1, t5_ref, head0 + hg * heads + h, tile, tile)

    far_bias = [LOG2E * t5_ref[(T5_BUCKETS - 1) * N_BIAS_HEADS + head0 + hg * heads + h] for h in range(heads)]
    bounds = [_logit_bound(q_ref[h, c], kmax_ref[b * group_heads + hg * heads + h], bmax_ref[head0 + hg * heads + h])
              for h in range(heads) for c in range(2)]

    def run(online):
        def step(j, kind):
            k0 = pl.multiple_of(j * tile, tile)
            logits = [jnp.dot(k_ref[h, pl.ds(k0, tile), :], q_ref[h, c], preferred_element_type=F32)
                      for h in range(heads) for c in range(2)]
            for h in range(heads):
                vt = v_ref[h, :, pl.ds(k0, tile)]
                for c in range(2):
                    idx = 2 * h + c
                    s, shift = logits[idx], bounds[idx]
                    if kind != "far":
                        s = s + bias_ref[2 * h + (1 if kind == "near" else 0)]
                    elif online:
                        s = s + far_bias[h]
                    else:
                        shift = shift - far_bias[h]
                    if kind == "diag":
                        s = jnp.where(_causal_mask(tile), s, NEG)
                    _softmax_accumulate(s, vt, acc_ref, idx, shift=shift, m_ref=m_ref if online else None)

        step(i, "diag")

        @pl.when(i >= 1)
        def _():
            step(i - 1, "near")

        def far(j, carry):
            step(j, "far")
            return carry

        _unrolled_loop(jnp.maximum(i - 1, 0), 2, far, 0)

    _two_pass_attention(run, acc_ref, m_ref, 2 * heads)

    for h in range(heads):
        o = _normalized(acc_ref, 2 * h) - lam_ref[0] * _normalized(acc_ref, 2 * h + 1)
        y = o * lax.rsqrt(jnp.mean(o * o, axis=0, keepdims=True) + EPS)
        o_ref[h] = ((y * g_ref[...]) * out_scale).astype(o_ref.dtype)


def _diff_attention(q_t, k, v_aug, lam, subln_gain, t5_flat, bias_max, k_norm_max, out_scale):
    bsz, nh, _, dh, seq = q_t.shape
    tile = min(ATTN_TILE, seq)
    hps = DIFF_HEADS_PER_STEP
    assert tile >= T5_FAR_DIST and seq % tile == 0 and nh % hps == 0
    kern = functools.partial(_diff_kernel, tile=tile, head0=0, heads=hps, group_heads=nh, out_scale=out_scale)
    return pl.pallas_call(
        kern,
        grid=(bsz, nh // hps, seq // tile),
        in_specs=[_smem(), _smem(), _smem(), _smem(),
                  pl.BlockSpec((None, hps, 2, dh, tile), lambda b, h, i: (b, h, 0, 0, i)),
                  _resident((None, hps, seq, dh), lambda b, h, i: (b, h, 0, 0)),
                  _resident((None, hps, V_ROWS, seq), lambda b, h, i: (b, h, 0, 0)),
                  pl.BlockSpec((dh, 1), lambda b, h, i: (0, 0))],
        out_specs=pl.BlockSpec((None, hps, dh, tile), lambda b, h, i: (b, h, 0, i)),
        out_shape=jax.ShapeDtypeStruct((bsz, nh, dh, seq), BF16),
        scratch_shapes=[pltpu.VMEM((2 * hps, tile, tile), F32),
                        pltpu.VMEM((2 * hps, 1, tile), F32),
                        pltpu.VMEM((2 * hps, V_ROWS, tile), F32)],
        compiler_params=_cparams(3),
        name="diff_attention",
    )(t5_flat, bias_max, k_norm_max, lam.reshape(1), q_t, k, v_aug, subln_gain.reshape(dh, 1))


def _kmean_kernel(k_ref, o_ref, *, blocks):
    k = k_ref[...].astype(F32)
    o_ref[...] = jnp.mean(k.reshape(blocks, MOBA_BLOCK, k.shape[-1]), axis=1)


def _moba_kmean(k):
    bsz, nh, seq, dh = k.shape
    nb = seq // MOBA_BLOCK
    blocks = min(8, nb)
    return pl.pallas_call(
        functools.partial(_kmean_kernel, blocks=blocks),
        grid=(bsz, nh, nb // blocks),
        in_specs=[pl.BlockSpec((None, None, blocks * MOBA_BLOCK, dh), lambda b, h, i: (b, h, i, 0))],
        out_specs=pl.BlockSpec((None, None, blocks, dh), lambda b, h, i: (b, h, i, 0)),
        out_shape=jax.ShapeDtypeStruct((bsz, nh, nb, dh), F32),
        compiler_params=_cparams(3),
    )(k)


def _moba_kernel(t5_ref, bmax_ref, kmax_ref, q_ref, k_ref, v_ref, km_ref, o_ref, bias_ref, sel_ref, m_ref, acc_ref,
                 *, block, head0, heads, n_blocks):
    b = pl.program_id(0)
    t = pl.program_id(1)
    qt = 2 * block
    halves = (slice(0, block), slice(block, qt))

    @pl.when(t == 0)
    def _():
        for h in range(heads):
            _fill_t5_tile(bias_ref, 2 * h, t5_ref, head0 + h, 0, block)
            _fill_t5_tile(bias_ref, 2 * h + 1, t5_ref, head0 + h, block, block)

    far_bias = [LOG2E * t5_ref[(T5_BUCKETS - 1) * N_BIAS_HEADS + head0 + h] for h in range(heads)]
    bounds = [_logit_bound(q_ref[h], kmax_ref[b * heads + h], bmax_ref[head0 + h]) for h in range(heads)]

    nidx = lax.broadcasted_iota(I32, (n_blocks, qt), 0)
    own = 2 * t + jnp.where(lax.broadcasted_iota(I32, (n_blocks, qt), 1) >= block, 1, 0)
    for h in range(heads):
        gate = jnp.dot(km_ref[h], q_ref[h], preferred_element_type=F32)
        g = jnp.where(nidx < own, gate, NEG)
        sel = jnp.zeros(gate.shape, F32)
        for _ in range(min(MOBA_TOPK, n_blocks)):
            mx = jnp.max(g, axis=0, keepdims=True)
            first = jnp.min(jnp.where(g == mx, nidx, n_blocks), axis=0, keepdims=True)
            pick = nidx == first
            sel = jnp.where(pick, 1.0, sel)
            g = jnp.where(pick, -jnp.inf, g)
        sel_ref[h] = jnp.where(nidx < own, sel, 0.0)

    def run(online):
        def step(n, kinds):
            k0 = pl.multiple_of(n * block, block)
            active = [hf for hf in range(2) if kinds[hf] != "skip"]
            cols = slice(halves[active[0]].start, halves[active[-1]].stop)
            logits = [jnp.dot(k_ref[h, pl.ds(k0, block), :], q_ref[h, :, cols], preferred_element_type=F32)
                      for h in range(heads)]
            merged = len(active) == 2 and kinds[0] == kinds[1] == "far"
            segments = [(cols, "far")] if merged else [(halves[hf], kinds[hf]) for hf in active]
            for h in range(heads):
                vt = v_ref[h, :, pl.ds(k0, block)]
                for seg, kind in segments:
                    s = logits[h][:, seg.start - cols.start:seg.stop - cols.start]
                    shift = bounds[h][:, seg]
                    if kind != "far":
                        s = s + bias_ref[2 * h + (1 if kind == "near" else 0)]
                    elif online:
                        s = s + far_bias[h]
                    else:
                        shift = shift - far_bias[h]
                    mask = _causal_mask(block) if kind == "diag" else sel_ref[h, pl.ds(n, 1), seg] > 0.5
                    s = jnp.where(mask, s, NEG)
                    _softmax_accumulate(s, vt, acc_ref, h, shift=shift, m_ref=m_ref if online else None,
                                        keep=mask, cols=seg)

        step(2 * t + 1, ("skip", "diag"))
        step(2 * t, ("diag", "near"))

        @pl.when(t >= 1)
        def _():
            step(2 * t - 1, ("near", "far"))

        def far(n, carry):
            step(n, ("far", "far"))
            return carry

        _unrolled_loop(jnp.maximum(2 * t - 1, 0), 4, far, 0)

    _two_pass_attention(run, acc_ref, m_ref, heads)
    for h in range(heads):
        o_ref[h] = _normalized(acc_ref, h).astype(o_ref.dtype)


def _moba_attention(q_t, k, v_aug, kmean, t5_flat, bias_max, k_norm_max):
    bsz, nh, dh, seq = q_t.shape
    block = MOBA_BLOCK
    qt = 2 * block
    assert seq % qt == 0 and block >= T5_FAR_DIST
    nb = seq // block
    kern = functools.partial(_moba_kernel, block=block, head0=N_HEADS, heads=nh, n_blocks=nb)
    return pl.pallas_call(
        kern,
        grid=(bsz, seq // qt),
        in_specs=[_smem(), _smem(), _smem(),
                  pl.BlockSpec((None, nh, dh, qt), lambda b, i: (b, 0, 0, i)),
                  _resident((None, nh, seq, dh), lambda b, i: (b, 0, 0, 0)),
                  _resident((None, nh, V_ROWS, seq), lambda b, i: (b, 0, 0, 0)),
                  _resident((None, nh, nb, dh), lambda b, i: (b, 0, 0, 0))],
        out_specs=pl.BlockSpec((None, nh, dh, qt), lambda b, i: (b, 0, 0, i)),
        out_shape=jax.ShapeDtypeStruct((bsz, nh, dh, seq), BF16),
        scratch_shapes=[pltpu.VMEM((2 * nh, block, block), F32), pltpu.VMEM((nh, nb, qt), F32),
                        pltpu.VMEM((nh, 1, qt), F32),
                        pltpu.VMEM((nh, V_ROWS, qt), F32)],
        compiler_params=_cparams(2),
        name="moba_attention",
    )(t5_flat, bias_max, k_norm_max, q_t, k, v_aug, kmean)


def _split3(x):
    def trunc(v):
        bits = lax.bitcast_convert_type(v, I32)
        return lax.bitcast_convert_type(bits & jnp.int32(-65536), F32)
    hi = trunc(x)
    r1 = x - hi
    mid = trunc(r1)
    lo = trunc(r1 - mid)
    return hi, mid, lo


def _forget_cumsum_kernel(f_ref, hi_ref, mid_ref, lo_ref, *, heads, rows):
    upper = (lax.broadcasted_iota(I32, (128, 128), 0) <= lax.broadcasted_iota(I32, (128, 128), 1)).astype(F32)
    strict = (lax.broadcasted_iota(I32, (rows, rows), 1) < lax.broadcasted_iota(I32, (rows, rows), 0)).astype(F32)
    for h in range(heads):
        x = jnp.concatenate([f_ref[h:h + 1, r * 128:(r + 1) * 128] for r in range(rows)], axis=0)
        ls = jnp.minimum(x, 0.0) - jnp.log(1.0 + jnp.exp(-jnp.abs(x)))
        within = jnp.dot(ls, upper, preferred_element_type=F32, precision=lax.Precision.HIGHEST)
        before = jnp.dot(strict, within, preferred_element_type=F32, precision=lax.Precision.HIGHEST)
        cf = (within + before[:, 127:128]) * LOG2E
        for part, out_ref in zip(_split3(cf), (hi_ref, mid_ref, lo_ref)):
            for r in range(rows):
                out_ref[h:h + 1, r * 128:(r + 1) * 128] = part[r:r + 1, :]


def _forget_cumsum(f_logit):
    bsz, nh, seq = f_logit.shape
    spec = pl.BlockSpec((None, nh, seq), lambda b: (b, 0, 0))
    return pl.pallas_call(
        functools.partial(_forget_cumsum_kernel, heads=nh, rows=seq // 128),
        grid=(bsz,),
        in_specs=[spec],
        out_specs=[spec] * 3,
        out_shape=[jax.ShapeDtypeStruct((bsz, nh, seq), F32)] * 3,
        compiler_params=_cparams(1),
        name="forget_cumsum",
    )(f_logit)


def _forget_kernel(kmax_ref, q_ref, c_ref, k_ref, v_ref, gate_ref, o_ref, m_ref, acc_ref,
                   *, tile, heads, group_heads):
    b = pl.program_id(0)
    hg = pl.program_id(1)
    i = pl.program_id(2)
    bounds = [_logit_bound(q_ref[h], kmax_ref[b * group_heads + hg * heads + h], 0.0) for h in range(heads)]
    pad = jnp.zeros((k_ref.shape[-1] - HEAD_DIM - c_ref.shape[1], tile), BF16)
    q_aug = [jnp.concatenate([q_ref[h], c_ref[h], pad], axis=0) for h in range(heads)]

    def run(online):
        def step(j, blocks, mask):
            k0 = pl.multiple_of(j * tile, tile)
            logits = [jnp.dot(k_ref[h, pl.ds(k0, blocks * tile), :], q_aug[h], preferred_element_type=F32)
                      for h in range(heads)]
            for h in range(heads):
                s = logits[h] if mask is None else jnp.where(mask, logits[h], NEG)
                _softmax_accumulate(s, v_ref[h, :, pl.ds(k0, blocks * tile)], acc_ref, h, shift=bounds[h],
                                    m_ref=m_ref if online else None)

        step(i, 1, _causal_mask(tile))
        n_big = i // FORGET_BLOCKS_PER_STEP

        def past_big(j, carry):
            step(j * FORGET_BLOCKS_PER_STEP, FORGET_BLOCKS_PER_STEP, None)
            return carry

        def past_single(j, carry):
            step(j, 1, None)
            return carry

        lax.fori_loop(0, n_big, past_big, 0)
        lax.fori_loop(n_big * FORGET_BLOCKS_PER_STEP, i, past_single, 0)

    _two_pass_attention(run, acc_ref, m_ref, heads)
    for h in range(heads):
        g = gate_ref[h]
        o_ref[h] = (_normalized(acc_ref, h) * (1.0 / (1.0 + jnp.exp(-g)))).astype(o_ref.dtype)


def _forgetting_attention(q_t, c_aug_t, k_aug, v_aug, gate_t, k_norm_max):
    bsz, nh, dh, seq = q_t.shape
    kdim = k_aug.shape[-1]
    tile = min(ATTN_TILE, seq)
    hps = FORGET_HEADS_PER_STEP
    assert seq % tile == 0 and nh % hps == 0
    return pl.pallas_call(
        functools.partial(_forget_kernel, tile=tile, heads=hps, group_heads=nh),
        grid=(bsz, nh // hps, seq // tile),
        in_specs=[_smem(),
                  pl.BlockSpec((None, hps, dh, tile), lambda b, h, i: (b, h, 0, i)),
                  pl.BlockSpec((None, hps, c_aug_t.shape[2], tile), lambda b, h, i: (b, h, 0, i)),
                  _resident((None, hps, seq, kdim), lambda b, h, i: (b, h, 0, 0)),
                  _resident((None, hps, V_ROWS, seq), lambda b, h, i: (b, h, 0, 0)),
                  pl.BlockSpec((None, hps, dh, tile), lambda b, h, i: (b, h, 0, i))],
        out_specs=pl.BlockSpec((None, hps, dh, tile), lambda b, h, i: (b, h, 0, i)),
        out_shape=jax.ShapeDtypeStruct((bsz, nh, dh, seq), BF16),
        scratch_shapes=[pltpu.VMEM((hps, 1, tile), F32),
                        pltpu.VMEM((hps, V_ROWS, tile), F32)],
        compiler_params=_cparams(3),
        name="forget_attention",
    )(k_norm_max, q_t, c_aug_t, k_aug, v_aug, gate_t)


def _key_to_float(u):
    ks = u ^ jnp.int32(-2147483648)
    bits = jnp.where(ks < 0, ks ^ jnp.int32(2147483647), ks)
    return lax.bitcast_convert_type(bits, F32)


def _truncate_to_bf16(x):
    bits = lax.bitcast_convert_type(x, I32) & jnp.int32(-65536)
    return lax.bitcast_convert_type(bits, F32).astype(BF16)


def _dsa_kernel(t5_ref, bmax_ref, kmax_ref, qi_ref, w_ref, ki_ref, q_ref, k_ref, v_ref, o_ref,
                sc_ref, hi_ref, bias_ref, tri_ref, m_ref, acc_ref, *, tile, head0, topk):
    b = pl.program_id(0)
    i = pl.program_id(1)

    @pl.when((b == 0) & (i == 0))
    def _():
        for h in range(N_HEADS):
            _fill_t5_tile(bias_ref, 2 * h, t5_ref, head0 + h, 0, tile)
            _fill_t5_tile(bias_ref, 2 * h + 1, t5_ref, head0 + h, tile, tile)
        tri_ref[...] = jnp.where(lax.broadcasted_iota(I32, (tile, tile), 1) <= lax.broadcasted_iota(I32, (tile, tile), 0),
                                 1.0, 0.0).astype(BF16)

    causal = _causal_mask(tile)

    def index_scores(j):
        k0 = pl.multiple_of(j * tile, tile)
        kt = ki_ref[pl.ds(k0, tile), :]
        raws = [jnp.dot(kt, qi_ref[h * IDX_DIM:(h + 1) * IDX_DIM, :], preferred_element_type=F32)
                for h in range(IDX_HEADS)]
        sc = jnp.zeros((tile, tile), F32)
        for h in range(IDX_HEADS):
            sc = sc + jnp.maximum(raws[h], 0.0) * w_ref[h:h + 1, :]
        return k0, sc

    def store_scores(k0, sc):
        sc_ref[pl.ds(k0, tile), :] = sc
        hi_ref[pl.ds(k0, tile), :] = _truncate_to_bf16(sc)

    def score_body(j, carry):
        store_scores(*index_scores(j))
        return carry

    _unrolled_loop(i, 4, score_body, 0)
    k0, sc = index_scores(i)
    store_scores(k0, jnp.where(causal, sc, NEG))

    def count(pred):
        def body(j, cnt):
            k0 = pl.multiple_of(j * tile, tile)
            ind = jnp.where(pred(sc_ref[pl.ds(k0, tile), :]), 1.0, 0.0)
            return cnt + jnp.sum(ind.reshape(tile // 8, 8, tile), axis=0)
        cnt = _unrolled_loop(i + 1, 4, body, jnp.zeros((8, tile), F32))
        return jnp.sum(cnt, axis=0, keepdims=True)

    def count_truncated(thr):
        def body(j, cnt):
            k0 = pl.multiple_of(j * tile, tile)
            ind = jnp.where(hi_ref[pl.ds(k0, tile), :] >= thr, jnp.ones((), BF16), jnp.zeros((), BF16))
            part = ind[0:BF16_ROWS]
            for r in range(1, tile // BF16_ROWS):
                part = part + ind[r * BF16_ROWS:(r + 1) * BF16_ROWS]
            return cnt + part.astype(F32)
        cnt = _unrolled_loop(i + 1, 4, body, jnp.zeros((BF16_ROWS, tile), F32))
        return jnp.sum(cnt, axis=0, keepdims=True)

    def high_bit(it, ans):
        cand = ans | jnp.left_shift(jnp.int32(1), 31 - it)
        return jnp.where(count_truncated(_truncate_to_bf16(_key_to_float(cand))) >= topk, cand, ans)

    ans = lax.fori_loop(0, 16, high_bit, jnp.zeros((1, tile), I32))

    above = count(lambda s: s > _key_to_float(ans))
    settled = above < topk
    need = jnp.where(settled, topk - above, 0.0)

    def low_bit(state):
        it, ans, open_f, need = state
        cand = ans | jnp.left_shift(jnp.int32(1), 31 - it)
        thr = _key_to_float(cand)
        cnt = count(lambda s: s >= thr)
        unsettled = open_f > 0.5
        ans = jnp.where(unsettled & (cnt >= topk), cand, ans)
        exact = unsettled & (cnt == topk)
        return it + 1, ans, jnp.where(exact, 0.0, open_f), jnp.where(exact, TAKE_ALL_TIES, need)

    _, ans, open_f, need = lax.while_loop(lambda st: (st[0] < 32) & (jnp.max(st[2]) > 0.5), low_bit,
                                          (jnp.int32(16), ans, jnp.where(settled, 0.0, 1.0), need))
    tau = _key_to_float(ans)
    need = jnp.where(open_f > 0.5, topk - count(lambda s: s > tau), need)

    qs = [q_ref[h] for h in range(N_HEADS)]

    far_bias = [LOG2E * t5_ref[(T5_BUCKETS - 1) * N_BIAS_HEADS + head0 + h] for h in range(N_HEADS)]
    bounds = [_logit_bound(qs[h], kmax_ref[b * N_HEADS + h], bmax_ref[head0 + h]) for h in range(N_HEADS)]

    def run(online):
        def attend(j, ties_seen, kind):
            k0 = pl.multiple_of(j * tile, tile)
            sc = sc_ref[pl.ds(k0, tile), :]
            eq = sc == tau
            eqf = jnp.where(eq, 1.0, 0.0)
            rank = ties_seen + jnp.dot(tri_ref[...], eqf.astype(BF16), preferred_element_type=F32)
            wgt = jnp.where(eq, jnp.where(rank <= need, 1.0, 0.0), jnp.where(sc > tau, 1.0, 0.0))
            if kind == "diag":
                wgt = jnp.where(causal, wgt, 0.0)
            keep = wgt > 0.5
            kt = k_ref[pl.ds(k0, tile), :]
            logits = [jnp.dot(kt, qs[h], preferred_element_type=F32) for h in range(N_HEADS)]
            for h in range(N_HEADS):
                s, shift = logits[h], bounds[h]
                if kind != "far":
                    s = s + bias_ref[2 * h + (1 if kind == "near" else 0)]
                elif online:
                    s = s + far_bias[h]
                else:
                    shift = shift - far_bias[h]
                s = jnp.where(keep, s, NEG)
                _softmax_accumulate(s, v_ref[h, :, pl.ds(k0, tile)], acc_ref, h, shift=shift,
                                    m_ref=m_ref if online else None, keep=keep)
            return ties_seen + jnp.sum(eqf, axis=0, keepdims=True)

        seen = _unrolled_loop(jnp.maximum(i - 1, 0), 4, lambda j, r: attend(j, r, "far"), jnp.zeros((1, tile), F32))
        seen = lax.cond(i >= 1, lambda r: attend(i - 1, r, "near"), lambda r: r, seen)
        attend(i, seen, "diag")

    _two_pass_attention(run, acc_ref, m_ref, N_HEADS)
    for h in range(N_HEADS):
        o_ref[h * HEAD_DIM:(h + 1) * HEAD_DIM, :] = _normalized(acc_ref, h).astype(o_ref.dtype)


def _dsa_attention(qi_t, w_t, k_idx, q_t, k, v_aug, t5_flat, bias_max, k_norm_max):
    bsz, _, width, seq = q_t.shape
    tile = min(SPARSE_TILE, seq)
    topk = min(DSA_TOPK_MAX, seq // 4)
    assert seq % tile == 0 and tile >= topk and tile >= T5_FAR_DIST
    kern = functools.partial(_dsa_kernel, tile=tile, head0=2 * N_HEADS, topk=topk)
    return pl.pallas_call(
        kern,
        grid=(bsz, seq // tile),
        in_specs=[_smem(), _smem(), _smem(),
                  pl.BlockSpec((None, qi_t.shape[1], tile), lambda b, i: (b, 0, i)),
                  pl.BlockSpec((None, 8, tile), lambda b, i: (b, 0, i)),
                  _resident((None, seq, k_idx.shape[2]), lambda b, i: (b, 0, 0)),
                  pl.BlockSpec((None, N_HEADS, width, tile), lambda b, i: (b, 0, 0, i)),
                  _resident((None, seq, width), lambda b, i: (b, 0, 0)),
                  _resident((None, N_HEADS, V_ROWS, seq), lambda b, i: (b, 0, 0, 0))],
        out_specs=pl.BlockSpec((None, width, tile), lambda b, i: (b, 0, i)),
        out_shape=jax.ShapeDtypeStruct((bsz, width, seq), BF16),
        scratch_shapes=[pltpu.VMEM((seq, tile), F32), pltpu.VMEM((seq, tile), BF16),
                        pltpu.VMEM((2 * N_HEADS, tile, tile), F32),
                        pltpu.VMEM((tile, tile), BF16),
                        pltpu.VMEM((N_HEADS, 1, tile), F32),
                        pltpu.VMEM((N_HEADS, V_ROWS, tile), F32)],
        compiler_params=_cparams(2),
        name="dsa_attention",
    )(t5_flat, bias_max, k_norm_max, qi_t, w_t, k_idx, q_t, k, v_aug)


def _outproj_kernel(x_ref, oa_ref, ob_ref, oc_ref, od_ref, w_ref, g_ref, y_ref):
    o_t = jnp.concatenate([r[...].reshape(GROUP_W, r.shape[-1]) for r in (oa_ref, ob_ref, oc_ref, od_ref)], axis=0)
    y = lax.dot_general(o_t, w_ref[...], (((0,), (0,)), ((), ())), preferred_element_type=F32)
    y_ref[...] = x_ref[...] + g_ref[...] * y


def _output_projection(x, o_groups, w_out, gate):
    bsz, seq, d = x.shape
    tm = ROW_TILE
    row = pl.BlockSpec((None, tm, d), lambda b, i: (b, i, 0))

    def grp(o):
        if o.ndim == 4:
            return pl.BlockSpec((None,) + o.shape[1:3] + (tm,), lambda b, i: (b, 0, 0, i))
        return pl.BlockSpec((None, GROUP_W, tm), lambda b, i: (b, 0, i))

    return pl.pallas_call(
        _outproj_kernel,
        grid=(bsz, seq // tm),
        in_specs=[row] + [grp(o) for o in o_groups] + [
                  _resident(w_out.shape, lambda b, i: (0, 0)),
                  pl.BlockSpec((None, 1, d), lambda b, i: (b, 0, 0))],
        out_specs=row,
        out_shape=jax.ShapeDtypeStruct(x.shape, F32),
        compiler_params=_cparams(2),
        name="output_projection",
    )(x, *o_groups, w_out.astype(BF16), gate)


HALO = 8


def _ffn_kernel(x_ref, xp_ref, gain_ref, sc_ref, sh_ref, g_ref, wup_ref, cw_ref, cb_ref, wd_ref, fg_ref,
                y_ref, h_ref, acc_ref, *, tm, chunk, n_chunks, final_norm):
    i = pl.program_id(1)

    def modulated(x):
        return ((_rms(x) * gain_ref[...]) * (1.0 + sc_ref[...]) + sh_ref[...]).astype(BF16)

    h_ref[HALO:, :] = modulated(x_ref[...])
    halo = modulated(xp_ref[...])
    h_ref[:HALO, :] = jnp.where(i > 0, halo, jnp.zeros_like(halo))
    acc_ref[...] = jnp.zeros(acc_ref.shape, F32)
    h = h_ref[...]

    def up(c0, half):
        col = pl.multiple_of(half * D_FF + c0, 128)
        return jnp.dot(h, wup_ref[:, pl.ds(col, chunk)], preferred_element_type=F32)

    def conv(u, c0, half):
        cw = cw_ref[half, :, pl.ds(c0, chunk)]
        return (cw[0:1] * u[HALO - 2:HALO - 2 + tm] + cw[1:2] * u[HALO - 1:HALO - 1 + tm]
                + cw[2:3] * u[HALO:HALO + tm]) + cb_ref[half, :, pl.ds(c0, chunk)]

    def chunks(first, count):
        c0s = [pl.multiple_of((first + n) * chunk, chunk) for n in range(count)]
        ups = [(up(c0, 0), up(c0, 1)) for c0 in c0s]
        acts = []
        for c0, (ug, uv) in zip(c0s, ups):
            gate = conv(ug, c0, 0)
            acts.append(((gate * (1.0 / (1.0 + jnp.exp(-gate)))) * conv(uv, c0, 1)).astype(BF16))
        a = acts[0] if count == 1 else jnp.concatenate(acts, axis=1)
        acc_ref[...] += jnp.dot(a, wd_ref[pl.ds(c0s[0], count * chunk), :], preferred_element_type=F32)

    def pair(p, carry):
        chunks(2 * p, 2)
        return carry

    lax.fori_loop(0, n_chunks // 2, pair, 0)
    if n_chunks % 2:
        chunks(n_chunks - 1, 1)

    y = x_ref[...] + g_ref[...] * acc_ref[...]
    if final_norm:
        y = _rms(y) * fg_ref[...]
    y_ref[...] = y


def _ffn(x, gain, sc, sh, gate, w_up, conv_w, conv_b, w_down, final_gain, final_norm):
    bsz, seq, d = x.shape
    tm = min(FFN_ROW_TILE, seq)
    fc = FFN_CHUNK
    n_chunks = D_FF // fc
    w_up_b = w_up.astype(BF16)
    cw = conv_w.reshape(3, 2, D_FF).transpose(1, 0, 2)
    cb = conv_b.reshape(2, 1, D_FF)
    vec = pl.BlockSpec((None, 1, d), lambda b, i: (b, 0, 0))
    one = pl.BlockSpec((1, d), lambda b, i: (0, 0))
    row = pl.BlockSpec((None, tm, d), lambda b, i: (b, i, 0))
    whole = lambda a: _resident(a.shape, lambda b, i: (0,) * a.ndim)
    w_down_b = w_down.astype(BF16)
    return pl.pallas_call(
        functools.partial(_ffn_kernel, tm=tm, chunk=fc, n_chunks=n_chunks, final_norm=final_norm),
        grid=(bsz, seq // tm),
        in_specs=[row,
                  pl.BlockSpec((None, HALO, d), lambda b, i: (b, jnp.maximum(i * (tm // HALO) - 1, 0), 0)),
                  one, vec, vec, vec, whole(w_up_b), whole(cw), whole(cb), whole(w_down_b), one],
        out_specs=row,
        out_shape=jax.ShapeDtypeStruct(x.shape, F32),
        scratch_shapes=[pltpu.VMEM((HALO + tm, d), BF16), pltpu.VMEM((tm, d), F32)],
        compiler_params=_cparams(2),
        name="conv_glu_ffn",
    )(x, x, gain.reshape(1, d), sc, sh, gate, w_up_b, cw, cb, w_down_b, final_gain.reshape(1, d))


def kernel(x, c, w_ada, b_ada, norm1_gain, w_in, f_bias, diff_lambda, diff_subln_gain, w_dq_up, w_didx_q,
           w_dkv_up, w_out, t5_table, norm2_gain, w_ffn_up, ffn_conv_w, ffn_conv_b, w_ffn_down, final_gain):
    bsz, seq, d = x.shape
    depth = w_ada.shape[0]
    t5_flat = t5_table.reshape(-1)
    bias_max = jnp.max(t5_table, axis=0) * LOG2E
    mod = _modulation(c, w_ada, b_ada)
    w = GROUP_W

    for l in range(depth):
        sh1, sc1, g1, sh2, sc2, g2 = [m[:, None, :] for m in jnp.split(mod[l], 6, axis=-1)]
        (qa, va, qb, vb, qc, vc, gc, qd, qi_t, vd, misc, ka, kb, kc, k_idx, kd), k_norm_max = _input_projection(
            x, norm1_gain[l], sc1, sh1, w_in[l], w_dq_up[l], w_didx_q[l], w_dkv_up[l])
        kmax = [k_norm_max[:, g].reshape(-1) for g in range(4)]

        lambda_init = 0.8 - 0.6 * math.exp(-0.3 * l)
        lq1, lk1, lq2, lk2 = diff_lambda[l]
        lam = jnp.exp(jnp.sum(lq1 * lk1)) - jnp.exp(jnp.sum(lq2 * lk2)) + lambda_init
        o_a = _diff_attention(qa, ka, va, lam, diff_subln_gain[l], t5_flat, bias_max, kmax[0], 1.0 - lambda_init)

        o_b = _moba_attention(qb, kb, vb, _moba_kmean(kb).astype(BF16), t5_flat, bias_max, kmax[1])

        hi, mid, lo = _forget_cumsum(misc[:, 8:8 + N_HEADS] + f_bias[l][None, :, None])
        ones = jnp.ones_like(hi)
        cparts = jnp.stack([hi, mid, lo], axis=2)
        onep = jnp.stack([ones] * 3, axis=2)
        padw = 2 * HEAD_DIM - HEAD_DIM - 6
        c_aug = jnp.concatenate([onep, cparts, jnp.zeros((bsz, N_HEADS, BF16_ROWS - 6, seq), F32)],
                                axis=2).astype(BF16)
        kc_aug = jnp.concatenate([kc, (-cparts).transpose(0, 1, 3, 2).astype(BF16),
                                  onep.transpose(0, 1, 3, 2).astype(BF16),
                                  jnp.zeros((bsz, N_HEADS, seq, padw), BF16)], axis=3)
        o_c = _forgetting_attention(qc, c_aug, kc_aug, vc, gc, kmax[2])

        o_d = _dsa_attention(qi_t, misc[:, :8], k_idx, qd, kd, vd, t5_flat, bias_max, kmax[3])

        x = _output_projection(x, [o_a, o_b, o_c, o_d], w_out[l], g1)
        x = _ffn(x, norm2_gain[l], sc2, sh2, g2, w_ffn_up[l], ffn_conv_w[l], ffn_conv_b[l], w_ffn_down[l],
                 final_gain, final_norm=(l == depth - 1))
    return x
```
